```python
import jax, jax.numpy as jnp
from jax import lax
import numpy as np

D_MODEL = 1024
BATCH = 4
SEQ = 8192
DEPTH = 2

GRID_W = 64
CTX_LEN = 256
N_MIXERS = 2
N_EVEN = (DEPTH + 1) // 2
N_ODD = DEPTH // 2

CHUNK = 128
A_WIDTH = D_MODEL
A_GROUPS = 16
A_GROUP_CH = A_WIDTH // A_GROUPS

HEAD_DIM = 64
N_Q_HEADS = D_MODEL // HEAD_DIM
N_KV_HEADS = 4
Q_PER_KV = N_Q_HEADS // N_KV_HEADS
QKV_WIDTH = (N_Q_HEADS + 2 * N_KV_HEADS) * HEAD_DIM
WINDOW = 128
BLOCK = 128
ROPE_BASE = 10000.0

D_FF = (7 * D_MODEL) // 2
N_EXPERTS = 8
TOP_K = 2

EPS = 1e-6
NEG_INF = -1e30

kernel_name = 'hybrid_chunkmlp_swa_moe_dit'


def rms_norm(x, g):
    xf = x.astype(jnp.float32)
    y = xf * lax.rsqrt(jnp.mean(xf * xf, axis=-1, keepdims=True) + EPS)
    return (y * g.astype(jnp.float32)).astype(x.dtype)


def modulate(x, shift, scale):
    return x * (1 + scale) + shift


def rope_1d(x, pos):
    half = x.shape[-1] // 2
    inv = ROPE_BASE ** (-jnp.arange(half, dtype=jnp.float32) / half)
    ang = pos.astype(jnp.float32)[:, None] * inv[None, :]
    cos = jnp.cos(ang)[:, None, :]
    sin = jnp.sin(ang)[:, None, :]
    xf = x.astype(jnp.float32)
    x1, x2 = xf[..., :half], xf[..., half:]
    return jnp.concatenate([x1 * cos - x2 * sin, x1 * sin + x2 * cos], axis=-1).astype(x.dtype)


def axial_rope(x, row, col):
    half = x.shape[-1] // 2
    return jnp.concatenate([rope_1d(x[..., :half], row), rope_1d(x[..., half:], col)], axis=-1)


def chunk_token_mlp(x, w_in, v_g, v_b, w_s, b_s, w_out):
    B, S, _ = x.shape
    uv = jax.nn.gelu(x @ w_in)
    u, v = uv[..., :A_WIDTH], uv[..., A_WIDTH:]
    vf = v.astype(jnp.float32)
    mu = jnp.mean(vf, axis=-1, keepdims=True)
    var = jnp.mean(jnp.square(vf - mu), axis=-1, keepdims=True)
    v = ((vf - mu) * lax.rsqrt(var + EPS) * v_g.astype(jnp.float32) + v_b.astype(jnp.float32)).astype(x.dtype)
    v = v.reshape(B, S // CHUNK, CHUNK, A_GROUPS, A_GROUP_CH)
    s = jnp.einsum('gpq,bnqgc->bnpgc', w_s, v) + b_s.T[None, None, :, :, None]
    return (u * s.reshape(B, S, A_WIDTH)) @ w_out


def split_qkv(y):
    B, S, _ = y.shape
    nq = N_Q_HEADS * HEAD_DIM
    nk = N_KV_HEADS * HEAD_DIM
    q = y[..., :nq].reshape(B, S, N_Q_HEADS, HEAD_DIM)
    k = y[..., nq:nq + nk].reshape(B, S, N_KV_HEADS, HEAD_DIM)
    v = y[..., nq + nk:].reshape(B, S, N_KV_HEADS, HEAD_DIM)
    return q, k, v


def window_attention(hn, zn, w_qkv, sink, w_out, with_ctx_out):
    B, S, _ = hn.shape
    L = zn.shape[1]
    rows = S // GRID_W
    row = jnp.repeat(jnp.arange(rows, dtype=jnp.int32), GRID_W)
    col = jnp.tile(jnp.arange(GRID_W, dtype=jnp.int32), rows)
    q, k, v = split_qkv(hn @ w_qkv)
    qc, kc, vc = split_qkv(zn @ w_qkv)
    q = axial_rope(q, row, col)
    k = axial_rope(k, row, col)
    scale = HEAD_DIM ** -0.5
    nb = S // BLOCK
    kw = BLOCK + 2 * WINDOW
    kp = jnp.pad(k, ((0, 0), (WINDOW, WINDOW), (0, 0), (0, 0)))
    vp = jnp.pad(v, ((0, 0), (WINDOW, WINDOW), (0, 0), (0, 0)))
    qs = jnp.moveaxis(q.reshape(B, nb, BLOCK, N_KV_HEADS, Q_PER_KV, HEAD_DIM), 1, 0)
    sink_l = sink.astype(jnp.float32).reshape(N_KV_HEADS, Q_PER_KV)[None, :, :, None, None]
    q_idx = jnp.arange(BLOCK)[:, None]
    k_rel = jnp.arange(kw)[None, :] - WINDOW
    band = jnp.abs(q_idx - k_rel) <= WINDOW

    def attend_block(args):
        n, qb = args
        kb = lax.dynamic_slice_in_dim(kp, n * BLOCK, kw, axis=1)
        vb = lax.dynamic_slice_in_dim(vp, n * BLOCK, kw, axis=1)
        kpos = n * BLOCK + k_rel
        valid = band & (kpos >= 0) & (kpos < S)
        s_loc = jnp.einsum('bqhgd,bkhd->bhgqk', qb, kb).astype(jnp.float32) * scale
        s_loc = jnp.where(valid, s_loc, NEG_INF)
        s_ctx = jnp.einsum('bqhgd,bkhd->bhgqk', qb, kc).astype(jnp.float32) * scale
        s_sink = jnp.broadcast_to(sink_l, s_loc.shape[:-1] + (1,))
        p = jax.nn.softmax(jnp.concatenate([s_loc, s_ctx, s_sink], axis=-1), axis=-1).astype(qb.dtype)
        return (jnp.einsum('bhgqk,bkhd->bqhgd', p[..., :kw], vb)
                + jnp.einsum('bhgqk,bkhd->bqhgd', p[..., kw:kw + L], vc))

    o = lax.map(attend_block, (jnp.arange(nb, dtype=jnp.int32), qs))
    y = jnp.moveaxis(o, 0, 1).reshape(B, S, N_Q_HEADS * HEAD_DIM) @ w_out
    if not with_ctx_out:
        return y, None
    qcg = qc.reshape(B, L, N_KV_HEADS, Q_PER_KV, HEAD_DIM)
    s_c = jnp.einsum('bqhgd,bkhd->bhgqk', qcg, kc).astype(jnp.float32) * scale
    s_cs = jnp.broadcast_to(sink_l, s_c.shape[:-1] + (1,))
    p_c = jax.nn.softmax(jnp.concatenate([s_c, s_cs], axis=-1), axis=-1).astype(qc.dtype)
    o_c = jnp.einsum('bhgqk,bkhd->bqhgd', p_c[..., :L], vc)
    yc = o_c.reshape(B, L, N_Q_HEADS * HEAD_DIM) @ w_out
    return y, yc


def swiglu(x, w_gate, w_up, w_down):
    return (jax.nn.silu(x @ w_gate) * (x @ w_up)) @ w_down


def moe_swiglu(x, w_router, w_gate, w_up, w_down):
    B, S, D = x.shape
    xt = x.reshape(-1, D)
    logits = (xt @ w_router).astype(jnp.float32)
    top_v, top_i = lax.top_k(logits, TOP_K)
    top_w = jax.nn.softmax(top_v, axis=-1)
    gates = jnp.sum(jax.nn.one_hot(top_i, N_EXPERTS, dtype=jnp.float32) * top_w[..., None], axis=1)
    gates = gates.astype(x.dtype)
    y = jnp.zeros_like(xt)
    for e in range(N_EXPERTS):
        y = y + gates[:, e:e + 1] * swiglu(xt, w_gate[e], w_up[e], w_down[e])
    return y.reshape(B, S, D)


def setup_inputs(seed: int = 0) -> dict:
    key = jax.random.key(seed)
    ks = jax.random.split(key, 24)
    f32 = jnp.float32
    D = D_MODEL
    nrm = lambda k, shape, s: jax.random.normal(k, shape, f32) * s
    return {
        'x': nrm(ks[0], (BATCH, SEQ, D), 1.0),
        'c': nrm(ks[1], (BATCH, D), 1.0),
        'ctx': nrm(ks[2], (BATCH, CTX_LEN, D), 1.0),
        'c_ctx': nrm(ks[3], (D,), 1.0),
        'ada_w': nrm(ks[4], (DEPTH, D, 6 * D), 0.5 * D ** -0.5),
        'ada_b': nrm(ks[5], (DEPTH, 6 * D), 0.01),
        'norm_g': 1.0 + nrm(ks[6], (DEPTH, 2, D), 0.02),
        'final_g': 1.0 + nrm(ks[7], (D,), 0.02),
        'a_w_in': nrm(ks[8], (N_EVEN, D, 2 * A_WIDTH), D ** -0.5),
        'a_v_g': 1.0 + nrm(ks[9], (N_EVEN, A_WIDTH), 0.02),
        'a_v_b': nrm(ks[10], (N_EVEN, A_WIDTH), 0.02),
        'a_w_s': nrm(ks[11], (N_EVEN, A_GROUPS, CHUNK, CHUNK), CHUNK ** -0.5),
        'a_b_s': 1.0 + nrm(ks[12], (N_EVEN, A_GROUPS, CHUNK), 0.02),
        'a_w_out': nrm(ks[13], (N_EVEN, A_WIDTH, D), A_WIDTH ** -0.5),
        'b_w_qkv': nrm(ks[14], (N_ODD, D, QKV_WIDTH), D ** -0.5),
        'b_sink': nrm(ks[15], (N_ODD, N_Q_HEADS), 0.5),
        'b_w_out': nrm(ks[16], (N_ODD, N_Q_HEADS * HEAD_DIM, D), (N_Q_HEADS * HEAD_DIM) ** -0.5),
        'ffn_w_gate': nrm(ks[17], (N_EVEN, D, D_FF), D ** -0.5),
        'ffn_w_up': nrm(ks[18], (N_EVEN, D, D_FF), D ** -0.5),
        'ffn_w_down': nrm(ks[19], (N_EVEN, D_FF, D), D_FF ** -0.5),
        'moe_w_router': nrm(ks[20], (N_ODD, D, N_EXPERTS), D ** -0.5),
        'moe_w_gate': nrm(ks[21], (N_ODD, N_EXPERTS, D, D_FF), D ** -0.5),
        'moe_w_up': nrm(ks[22], (N_ODD, N_EXPERTS, D, D_FF), D ** -0.5),
        'moe_w_down': nrm(ks[23], (N_ODD, N_EXPERTS, D_FF, D), D_FF ** -0.5),
    }


def reference(x, c, ctx, c_ctx, ada_w, ada_b, norm_g, final_g,
              a_w_in, a_v_g, a_v_b, a_w_s, a_b_s, a_w_out,
              b_w_qkv, b_sink, b_w_out,
              ffn_w_gate, ffn_w_up, ffn_w_down,
              moe_w_router, moe_w_gate, moe_w_up, moe_w_down):
    h = x
    z = ctx
    ada_in = jax.nn.silu(c)
    ada_in_ctx = jax.nn.silu(c_ctx)[None]
    for i in range(DEPTH):
        j = i // 2
        last = i == DEPTH - 1
        mod = (ada_in @ ada_w[i] + ada_b[i])[:, None, :]
        mod_c = (ada_in_ctx @ ada_w[i] + ada_b[i])[:, None, :]
        sh1, sc1, g1, sh2, sc2, g2 = jnp.split(mod, 6, axis=-1)
        sh1c, sc1c, g1c, sh2c, sc2c, g2c = jnp.split(mod_c, 6, axis=-1)
        hn = modulate(rms_norm(h, norm_g[i, 0]), sh1, sc1)
        zn = modulate(rms_norm(z, norm_g[i, 0]), sh1c, sc1c)
        if i % N_MIXERS == 0:
            dh = chunk_token_mlp(hn, a_w_in[j], a_v_g[j], a_v_b[j], a_w_s[j], a_b_s[j], a_w_out[j])
            dz = None if last else chunk_token_mlp(zn, a_w_in[j], a_v_g[j], a_v_b[j], a_w_s[j], a_b_s[j], a_w_out[j])
        else:
            dh, dz = window_attention(hn, zn, b_w_qkv[j], b_sink[j], b_w_out[j], not last)
        if i % 2 == 0:
            channel = lambda t: swiglu(t, ffn_w_gate[j], ffn_w_up[j], ffn_w_down[j])
        else:
            channel = lambda t: moe_swiglu(t, moe_w_router[j], moe_w_gate[j], moe_w_up[j], moe_w_down[j])
        h = h + g1 * dh
        h = h + g2 * channel(modulate(rms_norm(h, norm_g[i, 1]), sh2, sc2))
        if not last:
            z = z + g1c * dz
            z = z + g2c * channel(modulate(rms_norm(z, norm_g[i, 1]), sh2c, sc2c))
    return rms_norm(h, final_g)
```

```python
import functools

import jax
import jax.numpy as jnp
from jax import lax
from jax.experimental import pallas as pl
from jax.experimental.pallas import tpu as pltpu

F32 = jnp.float32
BF16 = jnp.bfloat16

EPS = 1e-6
NEG_INF = -1e30
LANES = 128
HEAD_DIM = 64
Q_PER_KV = 4
WINDOW = 128
GRID_W = 64
ROPE_BASE = 10000.0
CHUNK = 128
TOP_K = 2
VMEM_LIMIT_BYTES = 56 * 1024 * 1024
HIGHEST = lax.Precision.HIGHEST


def _params(*semantics):
    return pltpu.CompilerParams(dimension_semantics=semantics, vmem_limit_bytes=VMEM_LIMIT_BYTES)


def _rms_mod(x, gain, scale, shift):
    ms = jnp.mean(x * x, axis=-1, keepdims=True)
    return (x * lax.rsqrt(ms + EPS) * gain) * (1.0 + scale) + shift


def _full(shape):
    return pl.BlockSpec(shape, lambda *_: (0,) * len(shape))


def _ada_body(c_ref, w_ref, b_ref, o_ref):
    c = c_ref[...]
    a = c * jax.nn.sigmoid(c)
    o_ref[...] = jnp.dot(a, w_ref[...], precision=HIGHEST, preferred_element_type=F32) + b_ref[...]


def _ada_mod(c_rows, ada_w, ada_b, tn=1536):
    depth, d, n = ada_w.shape
    rows = c_rows.shape[0]
    return pl.pallas_call(
        _ada_body,
        grid=(depth, n // tn),
        in_specs=[
            pl.BlockSpec((rows, d), lambda l, j: (0, 0)),
            pl.BlockSpec((None, d, tn), lambda l, j: (l, 0, j)),
            pl.BlockSpec((None, 1, tn), lambda l, j: (l, 0, j)),
        ],
        out_specs=pl.BlockSpec((None, rows, tn), lambda l, j: (l, 0, j)),
        out_shape=jax.ShapeDtypeStruct((depth, rows, n), F32),
        compiler_params=_params("arbitrary", "arbitrary"),
        name="ada_mod",
    )(c_rows, ada_w, ada_b.reshape(depth, 1, n))


def _mixer_a_body(mod_ref, ng_ref, x_ref, win_ref, vg_ref, vb_ref, wpair_ref, bias_ref, wout_ref,
                  o_ref, vlo_ref, vhi_ref, gated_ref):
    tm, d = x_ref.shape
    a_width = wout_ref.shape[0]
    x = x_ref[...]
    hn = _rms_mod(x, ng_ref[...], mod_ref[1:2, :], mod_ref[0:1, :])
    uv = jnp.dot(hn.astype(BF16), win_ref[...], preferred_element_type=F32)
    uv = jax.nn.gelu(uv, approximate=True)
    v = uv[:, a_width:]
    mu = jnp.mean(v, axis=-1, keepdims=True)
    vc = v - mu
    var = jnp.mean(vc * vc, axis=-1, keepdims=True)
    vn = vc * lax.rsqrt(var + EPS) * vg_ref[...] + vb_ref[...]
    lane = lax.broadcasted_iota(jnp.int32, vn.shape, 1)
    low = (lane & (LANES - 1)) < (LANES // 2)
    vlo_ref[...] = jnp.where(low, vn, 0.0).astype(BF16)
    vhi_ref[...] = jnp.where(low, 0.0, vn).astype(BF16)
    for c in range(tm // CHUNK):
        rows = slice(c * CHUNK, (c + 1) * CHUNK)
        cols = []
        for j in range(a_width // LANES):
            lanes = slice(j * LANES, (j + 1) * LANES)
            rhs = jnp.concatenate([vlo_ref[rows, lanes], vhi_ref[rows, lanes]], axis=0)
            cols.append(jnp.dot(wpair_ref[j], rhs, preferred_element_type=F32))
        s = jnp.concatenate(cols, axis=1) + bias_ref[...]
        gated_ref[rows, :] = (uv[rows, :a_width] * s).astype(BF16)
    dh = jnp.dot(gated_ref[...], wout_ref[...], preferred_element_type=F32)
    o_ref[...] = x + mod_ref[2:3, :] * dh


def _mixer_a(x, mod_l, mod_row, ng, w_in, v_g, v_b, wpair, bias, w_out, tm):
    b, s, d = x.shape
    tm = min(tm, s)
    a_width = w_out.shape[0]
    return pl.pallas_call(
        _mixer_a_body,
        grid=(b, s // tm),
        in_specs=[
            pl.BlockSpec((None, 6, d), lambda bi, i: (mod_row(bi), 0, 0)),
            _full((1, d)),
            pl.BlockSpec((None, tm, d), lambda bi, i: (bi, i, 0)),
            _full(w_in.shape), _full((1, a_width)), _full((1, a_width)),
            _full(wpair.shape), _full(bias.shape), _full(w_out.shape),
        ],
        out_specs=pl.BlockSpec((None, tm, d), lambda bi, i: (bi, i, 0)),
        out_shape=jax.ShapeDtypeStruct(x.shape, F32),
        scratch_shapes=[pltpu.VMEM((tm, a_width), BF16)] * 3,
        compiler_params=_params("arbitrary", "arbitrary"),
        name="mixer_a",
    )(mod_l, ng, x, w_in, v_g, v_b, wpair, bias, w_out)


def _swiglu_part(xb, wg, wu, wd):
    g = jnp.dot(xb, wg, preferred_element_type=F32)
    u = jnp.dot(xb, wu, preferred_element_type=F32)
    a = (g * jax.nn.sigmoid(g) * u).astype(BF16)
    return jnp.dot(a, wd, preferred_element_type=F32)


def _dense_ffn_body(mod_ref, ng_ref, x_ref, wg_ref, wu_ref, wd_ref, o_ref, xn_ref, acc_ref):
    j = pl.program_id(2)

    @pl.when(j == 0)
    def _():
        xn_ref[...] = _rms_mod(x_ref[...], ng_ref[...], mod_ref[4:5, :], mod_ref[3:4, :]).astype(BF16)
        acc_ref[...] = jnp.zeros_like(acc_ref)

    acc_ref[...] += _swiglu_part(xn_ref[...], wg_ref[...], wu_ref[...], wd_ref[...])

    @pl.when(j == pl.num_programs(2) - 1)
    def _():
        o_ref[...] = x_ref[...] + mod_ref[5:6, :] * acc_ref[...]


def _dense_ffn(x, mod_l, mod_row, ng, w_gate, w_up, w_down, tm, tf):
    b, s, d = x.shape
    tm = min(tm, s)
    f = w_gate.shape[1]
    return pl.pallas_call(
        _dense_ffn_body,
        grid=(b, s // tm, f // tf),
        in_specs=[
            pl.BlockSpec((None, 6, d), lambda bi, i, j: (mod_row(bi), 0, 0)),
            _full((1, d)),
            pl.BlockSpec((None, tm, d), lambda bi, i, j: (bi, i, 0)),
            pl.BlockSpec((d, tf), lambda bi, i, j: (0, j)),
            pl.BlockSpec((d, tf), lambda bi, i, j: (0, j)),
            pl.BlockSpec((tf, d), lambda bi, i, j: (j, 0)),
        ],
        out_specs=pl.BlockSpec((None, tm, d), lambda bi, i, j: (bi, i, 0)),
        out_shape=jax.ShapeDtypeStruct(x.shape, F32),
        scratch_shapes=[pltpu.VMEM((tm, d), BF16), pltpu.VMEM((tm, d), F32)],
        compiler_params=_params("arbitrary", "arbitrary", "arbitrary"),
        name="dense_ffn",
    )(mod_l, ng, x, w_gate, w_up, w_down)


def _rope(y, cos, sin_up, sin_dn):
    w = y.shape[1]
    reps = w // LANES
    tile = lambda t: jnp.concatenate([t] * reps, axis=1)
    up = pltpu.roll(y, w - HEAD_DIM // 4, axis=1)
    dn = pltpu.roll(y, HEAD_DIM // 4, axis=1)
    return y * tile(cos) + up * tile(sin_up) + dn * tile(sin_dn)


def _qkv_body(mod_ref, ng_ref, x_ref, w_ref, cos_ref, sup_ref, sdn_ref, *out_refs, nq, nk):
    hn = _rms_mod(x_ref[...], ng_ref[...], mod_ref[1:2, :], mod_ref[0:1, :])
    y = jnp.dot(hn.astype(BF16), w_ref[...], preferred_element_type=F32)
    cos, sup, sdn = cos_ref[...], sup_ref[...], sdn_ref[...]
    if nq:
        q_ref, k_ref, v_ref = out_refs
        q_ref[...] = (_rope(y[:, :nq], cos, sup, sdn) * HEAD_DIM ** -0.5).astype(BF16)
    else:
        k_ref, v_ref = out_refs
    k_ref[...] = _rope(y[:, nq:nq + nk], cos, sup, sdn).astype(BF16)
    v_ref[...] = y[:, nq + nk:].astype(BF16)


def _qkv(x, mod_l, mod_row, ng, w, tables, nq, nk, nv, tm):
    b, s, d = x.shape
    tm = min(tm, s)
    widths = ([nq] if nq else []) + [nk, nv]
    tab_spec = pl.BlockSpec((tm, LANES), lambda bi, i: (i, 0))
    return pl.pallas_call(
        functools.partial(_qkv_body, nq=nq, nk=nk),
        grid=(b, s // tm),
        in_specs=[
            pl.BlockSpec((None, 6, d), lambda bi, i: (mod_row(bi), 0, 0)),
            _full((1, d)),
            pl.BlockSpec((None, tm, d), lambda bi, i: (bi, i, 0)),
            _full(w.shape), tab_spec, tab_spec, tab_spec,
        ],
        out_specs=[pl.BlockSpec((None, tm, n), lambda bi, i: (bi, i, 0)) for n in widths],
        out_shape=[jax.ShapeDtypeStruct((b, s, n), BF16) for n in widths],
        compiler_params=_params("arbitrary", "arbitrary"),
        name="qkv_rope" if nq else "kv_ctx",
    )(mod_l, ng, x, w, *tables)


def _attn_body(sink_ref, q_ref, kt_ref, kp_ref, kn_ref, vt_ref, vp_ref, vn_ref, kc_ref, vc_ref,
               o_ref, kwin_ref, vwin_ref, *, seq):
    tq = q_ref.shape[0]
    nq = tq // WINDOW
    i = pl.program_id(1)
    kwin_ref[0:WINDOW, :] = kp_ref[...]
    kwin_ref[WINDOW:WINDOW + tq, :] = kt_ref[...]
    kwin_ref[WINDOW + tq:, :] = kn_ref[...]
    vwin_ref[0:WINDOW, :] = vp_ref[...]
    vwin_ref[WINDOW:WINDOW + tq, :] = vt_ref[...]
    vwin_ref[WINDOW + tq:, :] = vn_ref[...]
    kw = 3 * WINDOW
    rows = Q_PER_KV * WINDOW
    n_pairs = vt_ref.shape[1] // LANES
    nt = (((1,), (1,)), ((), ()))
    qi = lax.broadcasted_iota(jnp.int32, (rows, kw), 0) & (WINDOW - 1)
    kr = lax.broadcasted_iota(jnp.int32, (rows, kw), 1)
    band = jnp.abs(qi - kr + WINDOW) <= WINDOW
    lane = lax.broadcasted_iota(jnp.int32, (rows, LANES), 1)
    low = lane < HEAD_DIM

    def block(jj, carry):
        r0 = pl.multiple_of(jj * WINDOW, WINDOW)
        kpos = (i * tq + r0 - WINDOW) + kr
        valid = band & (kpos >= 0) & (kpos < seq)
        kwin = kwin_ref[pl.ds(r0, kw), :]
        vwin = vwin_ref[pl.ds(r0, kw), :]
        for p in range(n_pairs):
            q4 = jnp.concatenate(
                [q_ref[pl.ds(r0, WINDOW), (p * Q_PER_KV + g) * LANES:(p * Q_PER_KV + g + 1) * LANES]
                 for g in range(Q_PER_KV)], axis=0)
            v_loc = vwin[:, p * LANES:(p + 1) * LANES]
            v_ctx = vc_ref[:, p * LANES:(p + 1) * LANES]
            halves = []
            for half in range(2):
                h = 2 * p + half
                s_loc = lax.dot_general(q4, kwin[:, h * LANES:(h + 1) * LANES], nt,
                                        preferred_element_type=F32)
                s_ctx = lax.dot_general(q4, kc_ref[:, h * LANES:(h + 1) * LANES], nt,
                                        preferred_element_type=F32)
                s_loc = jnp.where(valid, s_loc, NEG_INF)
                sink = jnp.concatenate(
                    [jnp.full((WINDOW, 1), sink_ref[h * Q_PER_KV + g], F32) for g in range(Q_PER_KV)],
                    axis=0)
                m = jnp.maximum(jnp.maximum(jnp.max(s_loc, axis=1, keepdims=True),
                                            jnp.max(s_ctx, axis=1, keepdims=True)), sink)
                p_loc = jnp.exp(s_loc - m)
                p_ctx = jnp.exp(s_ctx - m)
                denom = (jnp.sum(p_loc, axis=1, keepdims=True) + jnp.sum(p_ctx, axis=1, keepdims=True)
                         + jnp.exp(sink - m))
                o = (jnp.dot(p_loc.astype(BF16), v_loc, preferred_element_type=F32)
                     + jnp.dot(p_ctx.astype(BF16), v_ctx, preferred_element_type=F32))
                halves.append(o / denom)
            o4 = jnp.where(low, halves[0], halves[1]).astype(o_ref.dtype)
            for g in range(Q_PER_KV):
                c0 = (p * Q_PER_KV + g) * LANES
                o_ref[pl.ds(r0, WINDOW), c0:c0 + LANES] = o4[g * WINDOW:(g + 1) * WINDOW, :]
        return carry

    lax.fori_loop(0, nq, block, 0)


def _attention(q, k, v, kc, vc, sink, tq):
    b, s, dq = q.shape
    nk, nv = k.shape[2], v.shape[2]
    ctx = kc.shape[1]
    nq = tq // WINDOW
    nb = s // WINDOW
    tile = lambda n: pl.BlockSpec((None, tq, n), lambda bi, i: (bi, i, 0))
    prev = lambda n: pl.BlockSpec((None, WINDOW, n), lambda bi, i: (bi, jnp.maximum(i * nq - 1, 0), 0))
    nxt = lambda n: pl.BlockSpec((None, WINDOW, n), lambda bi, i: (bi, jnp.minimum((i + 1) * nq, nb - 1), 0))
    cblk = lambda n: pl.BlockSpec((None, ctx, n), lambda bi, i: (bi, 0, 0))
    return pl.pallas_call(
        functools.partial(_attn_body, seq=s),
        grid=(b, s // tq),
        in_specs=[
            pl.BlockSpec(memory_space=pltpu.SMEM),
            tile(dq), tile(nk), prev(nk), nxt(nk), tile(nv), prev(nv), nxt(nv), cblk(nk), cblk(nv),
        ],
        out_specs=tile(dq),
        out_shape=jax.ShapeDtypeStruct(q.shape, BF16),
        scratch_shapes=[pltpu.VMEM((tq + 2 * WINDOW, nk), BF16), pltpu.VMEM((tq + 2 * WINDOW, nv), BF16)],
        compiler_params=_params("arbitrary", "arbitrary"),
        name="window_attn",
    )(sink, q, k, k, k, v, v, v, kc, vc)


def _attn_out_body(mod_ref, ng_ref, h_ref, o_ref, wout_ref, wr_ref, h1_ref, xn_ref, gates_ref):
    dh = jnp.dot(o_ref[...], wout_ref[...], preferred_element_type=F32)
    h1 = h_ref[...] + mod_ref[2:3, :] * dh
    h1_ref[...] = h1
    xn = _rms_mod(h1, ng_ref[...], mod_ref[4:5, :], mod_ref[3:4, :])
    xn_ref[...] = xn.astype(BF16)
    logits = jnp.dot(xn, wr_ref[...], precision=HIGHEST, preferred_element_type=F32)
    n_exp = logits.shape[1]
    idx = lax.broadcasted_iota(jnp.int32, logits.shape, 1)
    m1 = jnp.max(logits, axis=1, keepdims=True)
    i1 = jnp.min(jnp.where(logits == m1, idx, n_exp), axis=1, keepdims=True)
    rest = jnp.where(idx == i1, -jnp.inf, logits)
    m2 = jnp.max(rest, axis=1, keepdims=True)
    i2 = jnp.min(jnp.where(rest == m2, idx, n_exp), axis=1, keepdims=True)
    e2 = jnp.exp(m2 - m1)
    w1 = 1.0 / (1.0 + e2)
    w2 = e2 / (1.0 + e2)
    gates_ref[...] = jnp.where(idx == i1, w1, 0.0) + jnp.where(idx == i2, w2, 0.0)


def _attn_out(h, o, mod_l, ng, w_out, w_router, tm):
    b, s, d = h.shape
    n_exp = w_router.shape[1]
    tok = lambda n: pl.BlockSpec((None, tm, n), lambda bi, i: (bi, i, 0))
    return pl.pallas_call(
        _attn_out_body,
        grid=(b, s // tm),
        in_specs=[
            pl.BlockSpec((None, 6, d), lambda bi, i: (bi, 0, 0)),
            _full((1, d)), tok(d), tok(o.shape[2]), _full(w_out.shape), _full(w_router.shape),
        ],
        out_specs=[tok(d), tok(d), tok(n_exp)],
        out_shape=[jax.ShapeDtypeStruct(h.shape, F32), jax.ShapeDtypeStruct(h.shape, BF16),
                   jax.ShapeDtypeStruct((b, s, n_exp), F32)],
        compiler_params=_params("arbitrary", "arbitrary"),
        name="attn_out_router",
    )(mod_l, ng, h, o, w_out, w_router)


def _moe_body(mod_ref, fg_ref, h_ref, x_ref, gates_ref, wg_ref, wu_ref, wd_ref, o_ref, acc_ref):
    e = pl.program_id(2)
    j = pl.program_id(3)

    @pl.when((e == 0) & (j == 0))
    def _():
        acc_ref[...] = jnp.zeros_like(acc_ref)

    gates = gates_ref[...]
    sel = lax.broadcasted_iota(jnp.int32, gates.shape, 1) == e
    gate = jnp.sum(jnp.where(sel, gates, 0.0), axis=1, keepdims=True)
    acc_ref[...] += gate * _swiglu_part(x_ref[...], wg_ref[...], wu_ref[...], wd_ref[...])

    @pl.when((e == pl.num_programs(2) - 1) & (j == pl.num_programs(3) - 1))
    def _():
        h2 = h_ref[...] + mod_ref[5:6, :] * acc_ref[...]
        ms = jnp.mean(h2 * h2, axis=-1, keepdims=True)
        o_ref[...] = h2 * lax.rsqrt(ms + EPS) * fg_ref[...]


def _moe(h1, xn, gates, mod_l, final_g, w_gate, w_up, w_down, tm, tf):
    b, s, d = h1.shape
    n_exp, _, f = w_gate.shape
    tok = lambda n: pl.BlockSpec((None, tm, n), lambda bi, i, e, j: (bi, i, 0))
    return pl.pallas_call(
        _moe_body,
        grid=(b, s // tm, n_exp, f // tf),
        in_specs=[
            pl.BlockSpec((None, 6, d), lambda bi, i, e, j: (bi, 0, 0)),
            _full((1, d)), tok(d), tok(d), tok(n_exp),
            pl.BlockSpec((None, d, tf), lambda bi, i, e, j: (e, 0, j)),
            pl.BlockSpec((None, d, tf), lambda bi, i, e, j: (e, 0, j)),
            pl.BlockSpec((None, tf, d), lambda bi, i, e, j: (e, j, 0)),
        ],
        out_specs=tok(d),
        out_shape=jax.ShapeDtypeStruct(h1.shape, F32),
        scratch_shapes=[pltpu.VMEM((tm, d), F32)],
        compiler_params=_params("arbitrary", "arbitrary", "arbitrary", "arbitrary"),
        name="moe_ffn",
    )(mod_l, final_g, h1, xn, gates, w_gate, w_up, w_down)


def _rope_tables(seq):
    pos = jnp.arange(seq, dtype=jnp.int32)
    lane = jnp.arange(LANES, dtype=jnp.int32) % HEAD_DIM
    quarter = HEAD_DIM // 4
    inv = ROPE_BASE ** (-(lane % quarter).astype(F32) / quarter)
    coord = jnp.where((lane < HEAD_DIM // 2)[None, :], (pos // GRID_W)[:, None], (pos % GRID_W)[:, None])
    ang = coord.astype(F32) * inv[None, :]
    second = ((lane % (HEAD_DIM // 2)) >= quarter)[None, :]
    sin = jnp.sin(ang)
    return jnp.cos(ang), jnp.where(second, 0.0, -sin), jnp.where(second, sin, 0.0)


def _head_order(n_q_heads):
    order = []
    for p in range(n_q_heads // (2 * Q_PER_KV)):
        for g in range(Q_PER_KV):
            order += [2 * p * Q_PER_KV + g, (2 * p + 1) * Q_PER_KV + g]
    return order


def kernel(x, c, ctx, c_ctx, ada_w, ada_b, norm_g, final_g, a_w_in, a_v_g, a_v_b, a_w_s, a_b_s, a_w_out,
           b_w_qkv, b_sink, b_w_out, ffn_w_gate, ffn_w_up, ffn_w_down,
           moe_w_router, moe_w_gate, moe_w_up, moe_w_down):
    batch, seq, d = x.shape
    ctx_len = ctx.shape[1]
    depth = ada_w.shape[0]
    assert depth == 2 and seq % 1024 == 0 and ctx_len % CHUNK == 0

    mod_rows = 8
    c_rows = jnp.concatenate([c, c_ctx[None, :], jnp.zeros((mod_rows - batch - 1, d), F32)], axis=0)
    mod = _ada_mod(c_rows, ada_w, ada_b).reshape(depth, mod_rows, 6, d)
    lat_row = lambda bi: bi
    ctx_row = lambda bi: batch

    a_width = a_w_out.shape[1]
    groups = a_w_s.shape[1]
    w_in = a_w_in[0].astype(BF16)
    w_s = a_w_s[0]
    wpair = jnp.concatenate([w_s[0::2], w_s[1::2]], axis=2).astype(BF16)
    bias = jnp.repeat(a_b_s[0].T, a_width // groups, axis=1)
    w_out_a = a_w_out[0].astype(BF16)
    ng0 = norm_g[0, 0][None, :]
    ng1 = norm_g[0, 1][None, :]
    wg0, wu0, wd0 = (w[0].astype(BF16) for w in (ffn_w_gate, ffn_w_up, ffn_w_down))
    vg, vb = a_v_g[0][None, :], a_v_b[0][None, :]

    def layer0(t, row):
        t = _mixer_a(t, mod[0], row, ng0, w_in, vg, vb, wpair, bias, w_out_a, tm=512)
        return _dense_ffn(t, mod[0], row, ng1, wg0, wu0, wd0, tm=1024, tf=512)

    h = layer0(x, lat_row)
    z = layer0(ctx, ctx_row)

    w_qkv = b_w_qkv[0]
    n_heads = b_sink.shape[1]
    nq = n_heads * HEAD_DIM
    n_kv = n_heads // Q_PER_KV
    nkv = n_kv * HEAD_DIM
    order = jnp.array(_head_order(n_heads), dtype=jnp.int32)
    wq = w_qkv[:, :nq].reshape(d, n_heads, HEAD_DIM)[:, order].reshape(d, nq)
    wk = w_qkv[:, nq:nq + nkv].reshape(d, n_kv, HEAD_DIM)
    zero = jnp.zeros_like(wk)
    even = (jnp.arange(n_kv) % 2 == 0)[None, :, None]
    wk_ext = jnp.concatenate([jnp.where(even, wk, zero), jnp.where(even, zero, wk)], axis=2).reshape(d, 2 * nkv)
    wv = w_qkv[:, nq + nkv:]
    w_lat = jnp.concatenate([wq, wk_ext, wv], axis=1).astype(BF16)
    w_ctx = jnp.concatenate([wk_ext, wv], axis=1).astype(BF16)
    w_out_b = b_w_out[0].reshape(n_heads, HEAD_DIM, d)[order].reshape(nq, d).astype(BF16)
    sink = b_sink[0]
    ng0 = norm_g[1, 0][None, :]
    ng1 = norm_g[1, 1][None, :]

    tables = _rope_tables(seq)
    q, k, v = _qkv(h, mod[1], lat_row, ng0, w_lat, tables, nq, 2 * nkv, nkv, tm=512)
    no_rope = (jnp.ones((ctx_len, LANES), F32), jnp.zeros((ctx_len, LANES), F32), jnp.zeros((ctx_len, LANES), F32))
    kc, vc = _qkv(z, mod[1], ctx_row, ng0, w_ctx, no_rope, 0, 2 * nkv, nkv, tm=ctx_len)
    o = _attention(q, k, v, kc, vc, sink, tq=512)
    h1, xn, gates = _attn_out(h, o, mod[1], ng1, w_out_b, moe_w_router[0], tm=512)
    wg1, wu1, wd1 = (w[0].astype(BF16) for w in (moe_w_gate, moe_w_up, moe_w_down))
    return _moe(h1, xn, gates, mod[1], final_g[None, :], wg1, wu1, wd1, tm=1024, tf=512)
```

```python
import functools

import jax
import jax.numpy as jnp
from jax import lax
from jax.experimental import pallas as pl
from jax.experimental.pallas import tpu as pltpu

F32 = jnp.float32
BF16 = jnp.bfloat16

EPS = 1e-6
NEG_INF = -1e30
LANES = 128
HEAD_DIM = 64
Q_PER_KV = 4
WINDOW = 128
GRID_W = 64
ROPE_BASE = 10000.0
CHUNK = 128
TOP_K = 2
VMEM_LIMIT_BYTES = 56 * 1024 * 1024
HIGHEST = lax.Precision.HIGHEST


def _params(*semantics):
    return pltpu.CompilerParams(dimension_semantics=semantics, vmem_limit_bytes=VMEM_LIMIT_BYTES)


def _rms_mod(x, gain, scale, shift):
    ms = jnp.mean(x * x, axis=-1, keepdims=True)
    return (x * lax.rsqrt(ms + EPS) * gain) * (1.0 + scale) + shift


def _full(shape):
    return pl.BlockSpec(shape, lambda *_: (0,) * len(shape))


def _ada_body(c_ref, w_ref, b_ref, o_ref):
    c = c_ref[...]
    a = c * jax.nn.sigmoid(c)
    o_ref[...] = jnp.dot(a, w_ref[...], precision=HIGHEST, preferred_element_type=F32) + b_ref[...]


def _ada_mod(c_rows, ada_w, ada_b, tn=1536):
    depth, d, n = ada_w.shape
    rows = c_rows.shape[0]
    return pl.pallas_call(
        _ada_body,
        grid=(depth, n // tn),
        in_specs=[
            pl.BlockSpec((rows, d), lambda l, j: (0, 0)),
            pl.BlockSpec((None, d, tn), lambda l, j: (l, 0, j)),
            pl.BlockSpec((None, 1, tn), lambda l, j: (l, 0, j)),
        ],
        out_specs=pl.BlockSpec((None, rows, tn), lambda l, j: (l, 0, j)),
        out_shape=jax.ShapeDtypeStruct((depth, rows, n), F32),
        compiler_params=_params("arbitrary", "arbitrary"),
        name="ada_mod",
    )(c_rows, ada_w, ada_b.reshape(depth, 1, n))


def _mixer_a_body(mod_ref, ng_ref, x_ref, win_ref, vg_ref, vb_ref, wpair_ref, bias_ref, wout_ref,
                  o_ref, vlo_ref, vhi_ref, gated_ref):
    tm, d = x_ref.shape
    a_width = wout_ref.shape[0]
    x = x_ref[...]
    hn = _rms_mod(x, ng_ref[...], mod_ref[1:2, :], mod_ref[0:1, :])
    uv = jnp.dot(hn.astype(BF16), win_ref[...], preferred_element_type=F32)
    uv = jax.nn.gelu(uv, approximate=True)
    v = uv[:, a_width:]
    mu = jnp.mean(v, axis=-1, keepdims=True)
    vc = v - mu
    var = jnp.mean(vc * vc, axis=-1, keepdims=True)
    vn = vc * lax.rsqrt(var + EPS) * vg_ref[...] + vb_ref[...]
    lane = lax.broadcasted_iota(jnp.int32, vn.shape, 1)
    low = (lane & (LANES - 1)) < (LANES // 2)
    vlo_ref[...] = jnp.where(low, vn, 0.0).astype(BF16)
    vhi_ref[...] = jnp.where(low, 0.0, vn).astype(BF16)
    for c in range(tm // CHUNK):
        rows = slice(c * CHUNK, (c + 1) * CHUNK)
        cols = []
        for j in range(a_width // LANES):
            lanes = slice(j * LANES, (j + 1) * LANES)
            rhs = jnp.concatenate([vlo_ref[rows, lanes], vhi_ref[rows, lanes]], axis=0)
            cols.append(jnp.dot(wpair_ref[j], rhs, preferred_element_type=F32))
        s = jnp.concatenate(cols, axis=1) + bias_ref[...]
        gated_ref[rows, :] = (uv[rows, :a_width] * s).astype(BF16)
    dh = jnp.dot(gated_ref[...], wout_ref[...], preferred_element_type=F32)
    o_ref[...] = x + mod_ref[2:3, :] * dh


def _mixer_a(x, mod_l, mod_row, ng, w_in, v_g, v_b, wpair, bias, w_out, tm):
    b, s, d = x.shape
    tm = min(tm, s)
    a_width = w_out.shape[0]
    return pl.pallas_call(
        _mixer_a_body,
        grid=(b, s // tm),
        in_specs=[
            pl.BlockSpec((None, 6, d), lambda bi, i: (mod_row(bi), 0, 0)),
            _full((1, d)),
            pl.BlockSpec((None, tm, d), lambda bi, i: (bi, i, 0)),
            _full(w_in.shape), _full((1, a_width)), _full((1, a_width)),
            _full(wpair.shape), _full(bias.shape), _full(w_out.shape),
        ],
        out_specs=pl.BlockSpec((None, tm, d), lambda bi, i: (bi, i, 0)),
        out_shape=jax.ShapeDtypeStruct(x.shape, F32),
        scratch_shapes=[pltpu.VMEM((tm, a_width), BF16)] * 3,
        compiler_params=_params("arbitrary", "arbitrary"),
        name="mixer_a",
    )(mod_l, ng, x, w_in, v_g, v_b, wpair, bias, w_out)


def _swiglu_part(xb, wg, wu, wd):
    g = jnp.dot(xb, wg, preferred_element_type=F32)
    u = jnp.dot(xb, wu, preferred_element_type=F32)
    a = (g * jax.nn.sigmoid(g) * u).astype(BF16)
    return jnp.dot(a, wd, preferred_element_type=F32)


def _dense_ffn_body(mod_ref, ng_ref, x_ref, wg_ref, wu_ref, wd_ref, o_ref, xn_ref, acc_ref):
    j = pl.program_id(2)

    @pl.when(j == 0)
    def _():
        xn_ref[...] = _rms_mod(x_ref[...], ng_ref[...], mod_ref[4:5, :], mod_ref[3:4, :]).astype(BF16)
        acc_ref[...] = jnp.zeros_like(acc_ref)

    acc_ref[...] += _swiglu_part(xn_ref[...], wg_ref[...], wu_ref[...], wd_ref[...])

    @pl.when(j == pl.num_programs(2) - 1)
    def _():
        o_ref[...] = x_ref[...] + mod_ref[5:6, :] * acc_ref[...]


def _dense_ffn(x, mod_l, mod_row, ng, w_gate, w_up, w_down, tm, tf):
    b, s, d = x.shape
    tm = min(tm, s)
    f = w_gate.shape[1]
    return pl.pallas_call(
        _dense_ffn_body,
        grid=(b, s // tm, f // tf),
        in_specs=[
            pl.BlockSpec((None, 6, d), lambda bi, i, j: (mod_row(bi), 0, 0)),
            _full((1, d)),
            pl.BlockSpec((None, tm, d), lambda bi, i, j: (bi, i, 0)),
            pl.BlockSpec((d, tf), lambda bi, i, j: (0, j)),
            pl.BlockSpec((d, tf), lambda bi, i, j: (0, j)),
            pl.BlockSpec((tf, d), lambda bi, i, j: (j, 0)),
        ],
        out_specs=pl.BlockSpec((None, tm, d), lambda bi, i, j: (bi, i, 0)),
        out_shape=jax.ShapeDtypeStruct(x.shape, F32),
        scratch_shapes=[pltpu.VMEM((tm, d), BF16), pltpu.VMEM((tm, d), F32)],
        compiler_params=_params("arbitrary", "arbitrary", "arbitrary"),
        name="dense_ffn",
    )(mod_l, ng, x, w_gate, w_up, w_down)


def _rope(y, cos, sin_up, sin_dn):
    w = y.shape[1]
    reps = w // LANES
    tile = lambda t: jnp.concatenate([t] * reps, axis=1)
    up = pltpu.roll(y, w - HEAD_DIM // 4, axis=1)
    dn = pltpu.roll(y, HEAD_DIM // 4, axis=1)
    return y * tile(cos) + up * tile(sin_up) + dn * tile(sin_dn)


def _qkv_body(mod_ref, ng_ref, x_ref, w_ref, cos_ref, sup_ref, sdn_ref, *out_refs, nq, nk):
    hn = _rms_mod(x_ref[...], ng_ref[...], mod_ref[1:2, :], mod_ref[0:1, :])
    y = jnp.dot(hn.astype(BF16), w_ref[...], preferred_element_type=F32)
    cos, sup, sdn = cos_ref[...], sup_ref[...], sdn_ref[...]
    if nq:
        q_ref, k_ref, v_ref = out_refs
        q_ref[...] = (_rope(y[:, :nq], cos, sup, sdn) * HEAD_DIM ** -0.5).astype(BF16)
    else:
        k_ref, v_ref = out_refs
    k_ref[...] = _rope(y[:, nq:nq + nk], cos, sup, sdn).astype(BF16)
    v_ref[...] = y[:, nq + nk:].astype(BF16)


def _qkv(x, mod_l, mod_row, ng, w, tables, nq, nk, nv, tm):
    b, s, d = x.shape
    tm = min(tm, s)
    widths = ([nq] if nq else []) + [nk, nv]
    tab_spec = pl.BlockSpec((tm, LANES), lambda bi, i: (i, 0))
    return pl.pallas_call(
        functools.partial(_qkv_body, nq=nq, nk=nk),
        grid=(b, s // tm),
        in_specs=[
            pl.BlockSpec((None, 6, d), lambda bi, i: (mod_row(bi), 0, 0)),
            _full((1, d)),
            pl.BlockSpec((None, tm, d), lambda bi, i: (bi, i, 0)),
            _full(w.shape), tab_spec, tab_spec, tab_spec,
        ],
        out_specs=[pl.BlockSpec((None, tm, n), lambda bi, i: (bi, i, 0)) for n in widths],
        out_shape=[jax.ShapeDtypeStruct((b, s, n), BF16) for n in widths],
        compiler_params=_params("arbitrary", "arbitrary"),
        name="qkv_rope" if nq else "kv_ctx",
    )(mod_l, ng, x, w, *tables)


def _attn_body(sink_ref, q_ref, kt_ref, kp_ref, kn_ref, vt_ref, vp_ref, vn_ref, kc_ref, vc_ref,
               o_ref, kwin_ref, vwin_ref, *, seq):
    tq = q_ref.shape[0]
    nq = tq // WINDOW
    i = pl.program_id(1)
    kwin_ref[0:WINDOW, :] = kp_ref[...]
    kwin_ref[WINDOW:WINDOW + tq, :] = kt_ref[...]
    kwin_ref[WINDOW + tq:, :] = kn_ref[...]
    vwin_ref[0:WINDOW, :] = vp_ref[...]
    vwin_ref[WINDOW:WINDOW + tq, :] = vt_ref[...]
    vwin_ref[WINDOW + tq:, :] = vn_ref[...]
    kw = 3 * WINDOW
    rows = Q_PER_KV * WINDOW
    n_pairs = vt_ref.shape[1] // LANES
    nt = (((1,), (1,)), ((), ()))
    qi = lax.broadcasted_iota(jnp.int32, (rows, kw), 0) & (WINDOW - 1)
    kr = lax.broadcasted_iota(jnp.int32, (rows, kw), 1)
    band = jnp.abs(qi - kr + WINDOW) <= WINDOW
    lane = lax.broadcasted_iota(jnp.int32, (rows, LANES), 1)
    low = lane < HEAD_DIM

    def block(jj, carry):
        r0 = pl.multiple_of(jj * WINDOW, WINDOW)
        kpos = (i * tq + r0 - WINDOW) + kr
        valid = band & (kpos >= 0) & (kpos < seq)
        kwin = kwin_ref[pl.ds(r0, kw), :]
        vwin = vwin_ref[pl.ds(r0, kw), :]
        for p in range(n_pairs):
            q4 = jnp.concatenate(
                [q_ref[pl.ds(r0, WINDOW), (p * Q_PER_KV + g) * LANES:(p * Q_PER_KV + g + 1) * LANES]
                 for g in range(Q_PER_KV)], axis=0)
            v_loc = vwin[:, p * LANES:(p + 1) * LANES]
            v_ctx = vc_ref[:, p * LANES:(p + 1) * LANES]
            halves = []
            for half in range(2):
                h = 2 * p + half
                s_loc = lax.dot_general(q4, kwin[:, h * LANES:(h + 1) * LANES], nt,
                                        preferred_element_type=F32)
                s_ctx = lax.dot_general(q4, kc_ref[:, h * LANES:(h + 1) * LANES], nt,
                                        preferred_element_type=F32)
                s_loc = jnp.where(valid, s_loc, NEG_INF)
                sink = jnp.concatenate(
                    [jnp.full((WINDOW, 1), sink_ref[h * Q_PER_KV + g], F32) for g in range(Q_PER_KV)],
                    axis=0)
                m = jnp.maximum(jnp.maximum(jnp.max(s_loc, axis=1, keepdims=True),
                                            jnp.max(s_ctx, axis=1, keepdims=True)), sink)
                p_loc = jnp.exp(s_loc - m)
                p_ctx = jnp.exp(s_ctx - m)
                denom = (jnp.sum(p_loc, axis=1, keepdims=True) + jnp.sum(p_ctx, axis=1, keepdims=True)
                         + jnp.exp(sink - m))
                o = (jnp.dot(p_loc.astype(BF16), v_loc, preferred_element_type=F32)
                     + jnp.dot(p_ctx.astype(BF16), v_ctx, preferred_element_type=F32))
                halves.append(o / denom)
            o4 = jnp.where(low, halves[0], halves[1]).astype(o_ref.dtype)
            for g in range(Q_PER_KV):
                c0 = (p * Q_PER_KV + g) * LANES
                o_ref[pl.ds(r0, WINDOW), c0:c0 + LANES] = o4[g * WINDOW:(g + 1) * WINDOW, :]
        return carry

    lax.fori_loop(0, nq, block, 0)


def _attention(q, k, v, kc, vc, sink, tq):
    b, s, dq = q.shape
    nk, nv = k.shape[2], v.shape[2]
    ctx = kc.shape[1]
    nq = tq // WINDOW
    nb = s // WINDOW
    tile = lambda n: pl.BlockSpec((None, tq, n), lambda bi, i: (bi, i, 0))
    prev = lambda n: pl.BlockSpec((None, WINDOW, n), lambda bi, i: (bi, jnp.maximum(i * nq - 1, 0), 0))
    nxt = lambda n: pl.BlockSpec((None, WINDOW, n), lambda bi, i: (bi, jnp.minimum((i + 1) * nq, nb - 1), 0))
    cblk = lambda n: pl.BlockSpec((None, ctx, n), lambda bi, i: (bi, 0, 0))
    return pl.pallas_call(
        functools.partial(_attn_body, seq=s),
        grid=(b, s // tq),
        in_specs=[
            pl.BlockSpec(memory_space=pltpu.SMEM),
            tile(dq), tile(nk), prev(nk), nxt(nk), tile(nv), prev(nv), nxt(nv), cblk(nk), cblk(nv),
        ],
        out_specs=tile(dq),
        out_shape=jax.ShapeDtypeStruct(q.shape, BF16),
        scratch_shapes=[pltpu.VMEM((tq + 2 * WINDOW, nk), BF16), pltpu.VMEM((tq + 2 * WINDOW, nv), BF16)],
        compiler_params=_params("arbitrary", "arbitrary"),
        name="window_attn",
    )(sink, q, k, k, k, v, v, v, kc, vc)


R_E1, R_E2, R_W1, R_W2, R_RANK1, R_RANK2 = range(6)
ROUTE_COLS = 8


def _attn_out_body(mod_ref, ng_ref, h_ref, o_ref, wout_ref, wrt_ref, h1_ref, xn_ref, route_ref, cnt_ref):
    tm = h_ref.shape[0]
    dh = jnp.dot(o_ref[...], wout_ref[...], preferred_element_type=F32)
    h1 = h_ref[...] + mod_ref[2:3, :] * dh
    h1_ref[...] = h1
    xn = _rms_mod(h1, ng_ref[...], mod_ref[4:5, :], mod_ref[3:4, :])
    xn_ref[...] = xn.astype(BF16)
    n_exp = wrt_ref.shape[0]
    idx = lax.broadcasted_iota(jnp.int32, (tm, n_exp), 1)
    logits = jnp.zeros((tm, n_exp), F32)
    for e in range(n_exp):
        col = jnp.sum(xn * wrt_ref[e:e + 1, :], axis=1, keepdims=True)
        logits = jnp.where(idx == e, col, logits)
    m1 = jnp.max(logits, axis=1, keepdims=True)
    i1 = jnp.min(jnp.where(logits == m1, idx, n_exp), axis=1, keepdims=True)
    rest = jnp.where(idx == i1, -jnp.inf, logits)
    m2 = jnp.max(rest, axis=1, keepdims=True)
    i2 = jnp.min(jnp.where(rest == m2, idx, n_exp), axis=1, keepdims=True)
    e2 = jnp.exp(m2 - m1)
    w1 = 1.0 / (1.0 + e2)
    w2 = e2 / (1.0 + e2)
    chosen = jnp.where((idx == i1) | (idx == i2), 1.0, 0.0)
    row = lax.broadcasted_iota(jnp.int32, (tm, tm), 0)
    col = lax.broadcasted_iota(jnp.int32, (tm, tm), 1)
    earlier = jnp.where(col < row, 1.0, 0.0).astype(BF16)
    before = jnp.dot(earlier, chosen.astype(BF16), preferred_element_type=F32)
    rank1 = jnp.sum(jnp.where(idx == i1, before, 0.0), axis=1, keepdims=True)
    rank2 = jnp.sum(jnp.where(idx == i2, before, 0.0), axis=1, keepdims=True)
    rec = jnp.zeros((tm, ROUTE_COLS), F32)
    rcol = lax.broadcasted_iota(jnp.int32, (tm, ROUTE_COLS), 1)
    for c, val in ((R_E1, i1.astype(F32)), (R_E2, i2.astype(F32)), (R_W1, w1), (R_W2, w2),
                   (R_RANK1, rank1), (R_RANK2, rank2)):
        rec = jnp.where(rcol == c, val, rec)
    route_ref[...] = rec
    cnt_ref[...] = jnp.sum(chosen, axis=0, keepdims=True)


def _attn_out(h, o, mod_l, ng, w_out, w_router_t, tm):
    b, s, d = h.shape
    n_exp = w_router_t.shape[0]
    tok = lambda n: pl.BlockSpec((None, tm, n), lambda bi, i: (bi, i, 0))
    return pl.pallas_call(
        _attn_out_body,
        grid=(b, s // tm),
        in_specs=[
            pl.BlockSpec((None, 6, d), lambda bi, i: (bi, 0, 0)),
            _full((1, d)), tok(d), tok(o.shape[2]), _full(w_out.shape), _full(w_router_t.shape),
        ],
        out_specs=[tok(d), tok(d), tok(ROUTE_COLS),
                   pl.BlockSpec((None, None, 1, n_exp), lambda bi, i: (bi, i, 0, 0))],
        out_shape=[jax.ShapeDtypeStruct(h.shape, F32), jax.ShapeDtypeStruct(h.shape, BF16),
                   jax.ShapeDtypeStruct((b, s, ROUTE_COLS), F32),
                   jax.ShapeDtypeStruct((b, s // tm, 1, n_exp), F32)],
        compiler_params=_params("arbitrary", "arbitrary"),
        name="attn_out_router",
    )(mod_l, ng, h, o, w_out, w_router_t)


ROW_ALIGN = 8
GATE_LANES = LANES
G_W1, G_W2, G_E1 = 0, 3, 6


def _chunk_sizes(max_rows):
    sizes = []
    size = ROW_ALIGN
    while size <= max_rows:
        sizes.append(size)
        size *= 2
    return tuple(reversed(sizes))


def _chunked_copies(n, src, src_row, dst, dst_row, sems, e, sizes):
    pairs = []
    for k, size in enumerate(sizes):
        done = n & (-2 * size)
        copy = pltpu.make_async_copy(
            src.at[pl.ds(pl.multiple_of(src_row + done, ROW_ALIGN), size)],
            dst.at[pl.ds(pl.multiple_of(dst_row + done, ROW_ALIGN), size)],
            sems.at[e, k])
        pairs.append(((n & size) != 0, copy))
    return pairs


def _start_all(pairs):
    for cond, copy in pairs:
        pl.when(cond)(copy.start)


def _wait_all(pairs):
    for cond, copy in pairs:
        pl.when(cond)(copy.wait)


def _selection(route, ls_ref, base, n_exp, r_cap):
    tm = route.shape[0]
    e1 = route[:, R_E1:R_E1 + 1].astype(jnp.int32)
    e2 = route[:, R_E2:R_E2 + 1].astype(jnp.int32)
    row1 = route[:, R_RANK1:R_RANK1 + 1].astype(jnp.int32)
    row2 = route[:, R_RANK2:R_RANK2 + 1].astype(jnp.int32)
    for e in range(n_exp):
        start = ls_ref[base + e]
        row1 = row1 + jnp.where(e1 == e, start, 0)
        row2 = row2 + jnp.where(e2 == e, start, 0)
    col = lax.broadcasted_iota(jnp.int32, (tm, r_cap), 1)
    return jnp.where((col == row1) | (col == row2), 1.0, 0.0).astype(BF16)


def _split3(w):
    a = w.astype(BF16).astype(F32)
    r = w - a
    b = r.astype(BF16).astype(F32)
    return a, b, r - b


def _dispatch_body(ls_ref, off_ref, n8_ref, tail_off_ref, tail_n_ref, na_ref, x_ref, route_ref, xs_ref,
                   local_ref, sems, *, n_exp, sizes, tm_moe, min_active):
    t = pl.program_id(0)
    base = t * n_exp
    tm, d = x_ref.shape
    r_cap = local_ref.shape[0]
    route = route_ref[...]
    sel = _selection(route, ls_ref, base, n_exp, r_cap)
    lane = lax.broadcasted_iota(jnp.int32, (tm, GATE_LANES), 1)
    extra = jnp.zeros((tm, GATE_LANES), F32)
    terms = _split3(route[:, R_W1:R_W1 + 1]) + _split3(route[:, R_W2:R_W2 + 1]) + (route[:, R_E1:R_E1 + 1],)
    for k, term in enumerate(terms):
        extra = jnp.where(lane == k, term, extra)
    xa = jnp.concatenate([x_ref[...], extra.astype(BF16)], axis=1)
    tn = (((0,), (0,)), ((), ()))
    rows = lax.dot_general(sel, xa, tn, preferred_element_type=F32)
    ridx = lax.broadcasted_iota(jnp.int32, (r_cap, 1), 0)
    owner = jnp.zeros((r_cap, 1), jnp.int32)
    for e in range(1, n_exp):
        owner = owner + jnp.where(ridx >= ls_ref[base + e], 1, 0)
    g = rows[:, d:]
    gate1 = g[:, G_W1:G_W1 + 1] + g[:, G_W1 + 1:G_W1 + 2] + g[:, G_W1 + 2:G_W1 + 3]
    gate2 = g[:, G_W2:G_W2 + 1] + g[:, G_W2 + 1:G_W2 + 2] + g[:, G_W2 + 2:G_W2 + 3]
    gate = jnp.where(g[:, G_E1:G_E1 + 1].astype(jnp.int32) == owner, gate1, gate2)
    local_ref[:, :d] = rows[:, :d]
    local_ref[:, d:] = jnp.broadcast_to(gate, (r_cap, GATE_LANES))

    pairs = []
    for e in range(n_exp):
        pairs += _chunked_copies(n8_ref[base + e], local_ref, ls_ref[base + e], xs_ref, off_ref[base + e],
                                 sems, e, sizes[0])
    _start_all(pairs)
    _wait_all(pairs)

    @pl.when(t == pl.num_programs(0) - 1)
    def _():
        local_ref[...] = jnp.zeros_like(local_ref)
        tails = []
        for e in range(n_exp):
            tails += _chunked_copies(tail_n_ref[e], local_ref, 0, xs_ref, tail_off_ref[e], sems, e, sizes[1])
        _start_all(tails)
        _wait_all(tails)
        n_tiles = xs_ref.shape[0] // tm_moe
        spare = []
        for k in range(n_tiles - min_active):
            tile = na_ref[0] + k
            copy = pltpu.make_async_copy(
                local_ref.at[pl.ds(0, tm_moe)],
                xs_ref.at[pl.ds(pl.multiple_of(jnp.minimum(tile, n_tiles - 1) * tm_moe, tm_moe), tm_moe)],
                sems.at[k % n_exp, k // n_exp])
            spare.append((tile < n_tiles, copy))
        _start_all(spare)
        _wait_all(spare)


def _dispatch(xn, route, tables, n_rows, tm, tm_moe):
    n_tok, d = xn.shape
    n_exp = tables["tail_n"].shape[0]
    r_cap = 2 * tm + LANES
    sizes = _chunk_sizes(tm), _chunk_sizes(tm_moe - ROW_ALIGN)
    assert tm_moe <= r_cap
    min_active = TOP_K * n_tok // tm_moe
    n_sems = max(len(sizes[0]), len(sizes[1]))
    assert n_rows // tm_moe - min_active <= n_exp * n_sems
    grid_spec = pltpu.PrefetchScalarGridSpec(
        num_scalar_prefetch=6,
        grid=(n_tok // tm,),
        in_specs=[pl.BlockSpec((tm, d), lambda t, *_: (t, 0)),
                  pl.BlockSpec((tm, ROUTE_COLS), lambda t, *_: (t, 0))],
        out_specs=pl.BlockSpec(memory_space=pl.ANY),
        scratch_shapes=[pltpu.VMEM((r_cap, d + GATE_LANES), F32),
                        pltpu.SemaphoreType.DMA((n_exp, n_sems))],
    )
    return pl.pallas_call(
        functools.partial(_dispatch_body, n_exp=n_exp, sizes=sizes, tm_moe=tm_moe, min_active=min_active),
        grid_spec=grid_spec,
        out_shape=jax.ShapeDtypeStruct((n_rows, d + GATE_LANES), F32),
        compiler_params=_params("arbitrary"),
        name="moe_dispatch",
    )(tables["ls"], tables["off"], tables["n8"], tables["tail_off"], tables["tail_n"], tables["n_active"],
      xn, route)


def _moe_body(te_ref, na_ref, x_ref, wg_ref, wu_ref, wd_ref, o_ref, xb_ref, acc_ref):
    i = pl.program_id(0)
    j = pl.program_id(1)
    d = o_ref.shape[1]

    @pl.when(i < na_ref[0])
    def _():
        @pl.when(j == 0)
        def _():
            xb_ref[...] = x_ref[:, :d].astype(BF16)
            acc_ref[...] = jnp.zeros_like(acc_ref)

        acc_ref[...] += _swiglu_part(xb_ref[...], wg_ref[...], wu_ref[...], wd_ref[...])

        @pl.when(j == pl.num_programs(1) - 1)
        def _():
            o_ref[...] = acc_ref[...] * x_ref[:, d:d + 1]

    @pl.when((i >= na_ref[0]) & (j == 0))
    def _():
        o_ref[...] = jnp.zeros_like(o_ref)


def _moe(xs, tile_expert, n_active, w_gate, w_up, w_down, tm, tf):
    n_rows, width = xs.shape
    n_exp, d, f = w_gate.shape
    nj = f // tf
    row = lambda i, j, te, na: (i, 0)
    jj = lambda i, j, na: jnp.where(i < na[0], j, nj - 1)
    grid_spec = pltpu.PrefetchScalarGridSpec(
        num_scalar_prefetch=2,
        grid=(n_rows // tm, nj),
        in_specs=[
            pl.BlockSpec((tm, width), row),
            pl.BlockSpec((None, d, tf), lambda i, j, te, na: (te[i], 0, jj(i, j, na))),
            pl.BlockSpec((None, d, tf), lambda i, j, te, na: (te[i], 0, jj(i, j, na))),
            pl.BlockSpec((None, tf, d), lambda i, j, te, na: (te[i], jj(i, j, na), 0)),
        ],
        out_specs=pl.BlockSpec((tm, d), row),
        scratch_shapes=[pltpu.VMEM((tm, d), BF16), pltpu.VMEM((tm, d), F32)],
    )
    return pl.pallas_call(
        _moe_body,
        grid_spec=grid_spec,
        out_shape=jax.ShapeDtypeStruct((n_rows, d), F32),
        compiler_params=_params("arbitrary", "arbitrary"),
        name="moe_ffn",
    )(tile_expert, n_active, xs, w_gate, w_up, w_down)


def _combine_body(ls_ref, off_ref, n8_ref, route_ref, h_ref, mod_ref, fg_ref, ys_ref, o_ref,
                  local_ref, sems, *, n_exp, sizes):
    t = pl.program_id(0) * pl.num_programs(1) + pl.program_id(1)
    base = t * n_exp
    r_cap = local_ref.shape[0]
    local_ref[...] = jnp.zeros_like(local_ref)
    pairs = []
    for e in range(n_exp):
        pairs += _chunked_copies(n8_ref[base + e], ys_ref, off_ref[base + e], local_ref, ls_ref[base + e],
                                 sems, e, sizes)
    _start_all(pairs)
    sel = _selection(route_ref[...], ls_ref, base, n_exp, r_cap)
    _wait_all(pairs)
    ys = local_ref[...]
    hi = ys.astype(BF16)
    lo = (ys - hi.astype(F32)).astype(BF16)
    y = jnp.dot(sel, hi, preferred_element_type=F32) + jnp.dot(sel, lo, preferred_element_type=F32)
    h2 = h_ref[...] + mod_ref[5:6, :] * y
    ms = jnp.mean(h2 * h2, axis=-1, keepdims=True)
    o_ref[...] = h2 * lax.rsqrt(ms + EPS) * fg_ref[...]


def _combine(ys, route, h1, mod_l, final_g, tables, tm):
    b, s, d = h1.shape
    n_exp = tables["tail_n"].shape[0]
    r_cap = 2 * tm + LANES
    sizes = _chunk_sizes(tm)
    tok = lambda n: pl.BlockSpec((None, tm, n), lambda bi, i, *_: (bi, i, 0))
    grid_spec = pltpu.PrefetchScalarGridSpec(
        num_scalar_prefetch=3,
        grid=(b, s // tm),
        in_specs=[tok(ROUTE_COLS), tok(d),
                  pl.BlockSpec((None, 6, d), lambda bi, i, *_: (bi, 0, 0)),
                  pl.BlockSpec((1, d), lambda bi, i, *_: (0, 0)),
                  pl.BlockSpec(memory_space=pl.ANY)],
        out_specs=tok(d),
        scratch_shapes=[pltpu.VMEM((r_cap, d), F32), pltpu.SemaphoreType.DMA((n_exp, len(sizes)))],
    )
    return pl.pallas_call(
        functools.partial(_combine_body, n_exp=n_exp, sizes=sizes),
        grid_spec=grid_spec,
        out_shape=jax.ShapeDtypeStruct(h1.shape, F32),
        compiler_params=_params("arbitrary", "arbitrary"),
        name="moe_combine",
    )(tables["ls"], tables["off"], tables["n8"], route, h1, mod_l, final_g, ys)


def _routing_tables(cnt, tm_moe, n_tiles_moe):
    n = cnt.astype(jnp.int32)
    n8 = (n + ROW_ALIGN - 1) // ROW_ALIGN * ROW_ALIGN
    ls = jnp.cumsum(n8, axis=1) - n8
    rows = jnp.sum(n8, axis=0)
    rows_pad = (rows + tm_moe - 1) // tm_moe * tm_moe
    ends = jnp.cumsum(rows_pad)
    ebase = ends - rows_pad
    off = ebase[None, :] + jnp.cumsum(n8, axis=0) - n8
    n_active = ends[-1] // tm_moe
    tile_start = jnp.arange(n_tiles_moe, dtype=jnp.int32) * tm_moe
    te = jnp.sum((tile_start[:, None] >= ends[None, :]).astype(jnp.int32), axis=1)
    te = jnp.where(tile_start < ends[-1], te, te[n_active - 1])
    flat = lambda a: a.reshape(-1).astype(jnp.int32)
    return dict(ls=flat(ls), off=flat(off), n8=flat(n8), tail_off=flat(ebase + rows),
                tail_n=flat(rows_pad - rows), tile_expert=te, n_active=flat(n_active))


def _rope_tables(seq):
    pos = jnp.arange(seq, dtype=jnp.int32)
    lane = jnp.arange(LANES, dtype=jnp.int32) % HEAD_DIM
    quarter = HEAD_DIM // 4
    inv = ROPE_BASE ** (-(lane % quarter).astype(F32) / quarter)
    coord = jnp.where((lane < HEAD_DIM // 2)[None, :], (pos // GRID_W)[:, None], (pos % GRID_W)[:, None])
    ang = coord.astype(F32) * inv[None, :]
    second = ((lane % (HEAD_DIM // 2)) >= quarter)[None, :]
    sin = jnp.sin(ang)
    return jnp.cos(ang), jnp.where(second, 0.0, -sin), jnp.where(second, sin, 0.0)


def _head_order(n_q_heads):
    order = []
    for p in range(n_q_heads // (2 * Q_PER_KV)):
        for g in range(Q_PER_KV):
            order += [2 * p * Q_PER_KV + g, (2 * p + 1) * Q_PER_KV + g]
    return order


def kernel(x, c, ctx, c_ctx, ada_w, ada_b, norm_g, final_g, a_w_in, a_v_g, a_v_b, a_w_s, a_b_s, a_w_out,
           b_w_qkv, b_sink, b_w_out, ffn_w_gate, ffn_w_up, ffn_w_down,
           moe_w_router, moe_w_gate, moe_w_up, moe_w_down):
    batch, seq, d = x.shape
    ctx_len = ctx.shape[1]
    depth = ada_w.shape[0]
    assert depth == 2 and seq % 1024 == 0 and ctx_len % CHUNK == 0

    mod_rows = 8
    c_rows = jnp.concatenate([c, c_ctx[None, :], jnp.zeros((mod_rows - batch - 1, d), F32)], axis=0)
    mod = _ada_mod(c_rows, ada_w, ada_b).reshape(depth, mod_rows, 6, d)
    lat_row = lambda bi: bi
    ctx_row = lambda bi: batch

    a_width = a_w_out.shape[1]
    groups = a_w_s.shape[1]
    w_in = a_w_in[0].astype(BF16)
    w_s = a_w_s[0]
    wpair = jnp.concatenate([w_s[0::2], w_s[1::2]], axis=2).astype(BF16)
    bias = jnp.repeat(a_b_s[0].T, a_width // groups, axis=1)
    w_out_a = a_w_out[0].astype(BF16)
    ng0 = norm_g[0, 0][None, :]
    ng1 = norm_g[0, 1][None, :]
    wg0, wu0, wd0 = (w[0].astype(BF16) for w in (ffn_w_gate, ffn_w_up, ffn_w_down))
    vg, vb = a_v_g[0][None, :], a_v_b[0][None, :]

    def layer0(t, row):
        t = _mixer_a(t, mod[0], row, ng0, w_in, vg, vb, wpair, bias, w_out_a, tm=512)
        return _dense_ffn(t, mod[0], row, ng1, wg0, wu0, wd0, tm=1024, tf=512)

    h = layer0(x, lat_row)
    z = layer0(ctx, ctx_row)

    w_qkv = b_w_qkv[0]
    n_heads = b_sink.shape[1]
    nq = n_heads * HEAD_DIM
    n_kv = n_heads // Q_PER_KV
    nkv = n_kv * HEAD_DIM
    order = jnp.array(_head_order(n_heads), dtype=jnp.int32)
    wq = w_qkv[:, :nq].reshape(d, n_heads, HEAD_DIM)[:, order].reshape(d, nq)
    wk = w_qkv[:, nq:nq + nkv].reshape(d, n_kv, HEAD_DIM)
    zero = jnp.zeros_like(wk)
    even = (jnp.arange(n_kv) % 2 == 0)[None, :, None]
    wk_ext = jnp.concatenate([jnp.where(even, wk, zero), jnp.where(even, zero, wk)], axis=2).reshape(d, 2 * nkv)
    wv = w_qkv[:, nq + nkv:]
    w_lat = jnp.concatenate([wq, wk_ext, wv], axis=1).astype(BF16)
    w_ctx = jnp.concatenate([wk_ext, wv], axis=1).astype(BF16)
    w_out_b = b_w_out[0].reshape(n_heads, HEAD_DIM, d)[order].reshape(nq, d).astype(BF16)
    sink = b_sink[0]
    ng0 = norm_g[1, 0][None, :]
    ng1 = norm_g[1, 1][None, :]

    tables = _rope_tables(seq)
    q, k, v = _qkv(h, mod[1], lat_row, ng0, w_lat, tables, nq, 2 * nkv, nkv, tm=512)
    no_rope = (jnp.ones((ctx_len, LANES), F32), jnp.zeros((ctx_len, LANES), F32), jnp.zeros((ctx_len, LANES), F32))
    kc, vc = _qkv(z, mod[1], ctx_row, ng0, w_ctx, no_rope, 0, 2 * nkv, nkv, tm=ctx_len)
    o = _attention(q, k, v, kc, vc, sink, tq=512)
    tm_tok, tm_moe = 512, 1024
    h1, xn, route, cnt = _attn_out(h, o, mod[1], ng1, w_out_b, moe_w_router[0].T, tm=tm_tok)
    n_exp = moe_w_router.shape[2]
    n_tok = batch * seq
    n_tok_tiles = n_tok // tm_tok
    max_rows = TOP_K * n_tok + n_tok_tiles * n_exp * (ROW_ALIGN - 1) + n_exp * (tm_moe - 1)
    n_moe_tiles = -(-max_rows // tm_moe)
    tables = _routing_tables(cnt.reshape(n_tok_tiles, n_exp), tm_moe, n_moe_tiles)
    xs = _dispatch(xn.reshape(n_tok, d), route.reshape(n_tok, ROUTE_COLS), tables, n_moe_tiles * tm_moe,
                   tm_tok, tm_moe)
    wg1, wu1, wd1 = (w[0].astype(BF16) for w in (moe_w_gate, moe_w_up, moe_w_down))
    ys = _moe(xs, tables["tile_expert"], tables["n_active"], wg1, wu1, wd1, tm=tm_moe, tf=512)
    return _combine(ys, route, h1, mod[1], final_g[None, :], tables, tm_tok)
```

```python
import functools

import jax
import jax.numpy as jnp
from jax import lax
from jax.experimental import pallas as pl
from jax.experimental.pallas import tpu as pltpu

F32 = jnp.float32
BF16 = jnp.bfloat16

EPS = 1e-6
NEG_INF = -1e30
LANES = 128
HEAD_DIM = 64
Q_PER_KV = 4
WINDOW = 128
GRID_W = 64
ROPE_BASE = 10000.0
CHUNK = 128
TOP_K = 2
VMEM_LIMIT_BYTES = 56 * 1024 * 1024
HIGHEST = lax.Precision.HIGHEST


def _params(*semantics):
    return pltpu.CompilerParams(dimension_semantics=semantics, vmem_limit_bytes=VMEM_LIMIT_BYTES)


def _rms_mod(x, gain, scale, shift):
    ms = jnp.mean(x * x, axis=-1, keepdims=True)
    return (x * lax.rsqrt(ms + EPS) * gain) * (1.0 + scale) + shift


def _full(shape):
    return pl.BlockSpec(shape, lambda *_: (0,) * len(shape))


def _ada_body(c_ref, w_ref, b_ref, o_ref):
    c = c_ref[...]
    a = c * jax.nn.sigmoid(c)
    o_ref[...] = jnp.dot(a, w_ref[...], precision=HIGHEST, preferred_element_type=F32) + b_ref[...]


def _ada_mod(c_rows, ada_w, ada_b, tn=1536):
    depth, d, n = ada_w.shape
    rows = c_rows.shape[0]
    return pl.pallas_call(
        _ada_body,
        grid=(depth, n // tn),
        in_specs=[
            pl.BlockSpec((rows, d), lambda l, j: (0, 0)),
            pl.BlockSpec((None, d, tn), lambda l, j: (l, 0, j)),
            pl.BlockSpec((None, 1, tn), lambda l, j: (l, 0, j)),
        ],
        out_specs=pl.BlockSpec((None, rows, tn), lambda l, j: (l, 0, j)),
        out_shape=jax.ShapeDtypeStruct((depth, rows, n), F32),
        compiler_params=_params("arbitrary", "arbitrary"),
        name="ada_mod",
    )(c_rows, ada_w, ada_b.reshape(depth, 1, n))


def _mixer_a_body(mod_ref, ng_ref, x_ref, win_ref, vg_ref, vb_ref, wpair_ref, bias_ref, wout_ref,
                  o_ref, vlo_ref, vhi_ref, gated_ref):
    tm, d = x_ref.shape
    a_width = wout_ref.shape[0]
    x = x_ref[...]
    hn = _rms_mod(x, ng_ref[...], mod_ref[1:2, :], mod_ref[0:1, :])
    uv = jnp.dot(hn.astype(BF16), win_ref[...], preferred_element_type=F32)
    uv = jax.nn.gelu(uv, approximate=True)
    v = uv[:, a_width:]
    mu = jnp.mean(v, axis=-1, keepdims=True)
    vc = v - mu
    var = jnp.mean(vc * vc, axis=-1, keepdims=True)
    vn = vc * lax.rsqrt(var + EPS) * vg_ref[...] + vb_ref[...]
    lane = lax.broadcasted_iota(jnp.int32, vn.shape, 1)
    low = (lane & (LANES - 1)) < (LANES // 2)
    vlo_ref[...] = jnp.where(low, vn, 0.0).astype(BF16)
    vhi_ref[...] = jnp.where(low, 0.0, vn).astype(BF16)
    for c in range(tm // CHUNK):
        rows = slice(c * CHUNK, (c + 1) * CHUNK)
        cols = []
        for j in range(a_width // LANES):
            lanes = slice(j * LANES, (j + 1) * LANES)
            rhs = jnp.concatenate([vlo_ref[rows, lanes], vhi_ref[rows, lanes]], axis=0)
            cols.append(jnp.dot(wpair_ref[j], rhs, preferred_element_type=F32))
        s = jnp.concatenate(cols, axis=1) + bias_ref[...]
        gated_ref[rows, :] = (uv[rows, :a_width] * s).astype(BF16)
    dh = jnp.dot(gated_ref[...], wout_ref[...], preferred_element_type=F32)
    o_ref[...] = x + mod_ref[2:3, :] * dh


def _mixer_a(x, mod_l, mod_row, ng, w_in, v_g, v_b, wpair, bias, w_out, tm):
    b, s, d = x.shape
    tm = min(tm, s)
    a_width = w_out.shape[0]
    return pl.pallas_call(
        _mixer_a_body,
        grid=(b, s // tm),
        in_specs=[
            pl.BlockSpec((None, 6, d), lambda bi, i: (mod_row(bi), 0, 0)),
            _full((1, d)),
            pl.BlockSpec((None, tm, d), lambda bi, i: (bi, i, 0)),
            _full(w_in.shape), _full((1, a_width)), _full((1, a_width)),
            _full(wpair.shape), _full(bias.shape), _full(w_out.shape),
        ],
        out_specs=pl.BlockSpec((None, tm, d), lambda bi, i: (bi, i, 0)),
        out_shape=jax.ShapeDtypeStruct(x.shape, F32),
        scratch_shapes=[pltpu.VMEM((tm, a_width), BF16)] * 3,
        compiler_params=_params("arbitrary", "arbitrary"),
        name="mixer_a",
    )(mod_l, ng, x, w_in, v_g, v_b, wpair, bias, w_out)


def _swiglu_part(xb, wg, wu, wd):
    g = jnp.dot(xb, wg, preferred_element_type=F32)
    u = jnp.dot(xb, wu, preferred_element_type=F32)
    a = (g * jax.nn.sigmoid(g) * u).astype(BF16)
    return jnp.dot(a, wd, preferred_element_type=F32)


def _dense_ffn_body(mod_ref, ng_ref, x_ref, wg_ref, wu_ref, wd_ref, o_ref, xn_ref, acc_ref):
    j = pl.program_id(2)

    @pl.when(j == 0)
    def _():
        xn_ref[...] = _rms_mod(x_ref[...], ng_ref[...], mod_ref[4:5, :], mod_ref[3:4, :]).astype(BF16)
        acc_ref[...] = jnp.zeros_like(acc_ref)

    acc_ref[...] += _swiglu_part(xn_ref[...], wg_ref[...], wu_ref[...], wd_ref[...])

    @pl.when(j == pl.num_programs(2) - 1)
    def _():
        o_ref[...] = x_ref[...] + mod_ref[5:6, :] * acc_ref[...]


def _dense_ffn(x, mod_l, mod_row, ng, w_gate, w_up, w_down, tm, tf):
    b, s, d = x.shape
    tm = min(tm, s)
    f = w_gate.shape[1]
    return pl.pallas_call(
        _dense_ffn_body,
        grid=(b, s // tm, f // tf),
        in_specs=[
            pl.BlockSpec((None, 6, d), lambda bi, i, j: (mod_row(bi), 0, 0)),
            _full((1, d)),
            pl.BlockSpec((None, tm, d), lambda bi, i, j: (bi, i, 0)),
            pl.BlockSpec((d, tf), lambda bi, i, j: (0, j)),
            pl.BlockSpec((d, tf), lambda bi, i, j: (0, j)),
            pl.BlockSpec((tf, d), lambda bi, i, j: (j, 0)),
        ],
        out_specs=pl.BlockSpec((None, tm, d), lambda bi, i, j: (bi, i, 0)),
        out_shape=jax.ShapeDtypeStruct(x.shape, F32),
        scratch_shapes=[pltpu.VMEM((tm, d), BF16), pltpu.VMEM((tm, d), F32)],
        compiler_params=_params("arbitrary", "arbitrary", "arbitrary"),
        name="dense_ffn",
    )(mod_l, ng, x, w_gate, w_up, w_down)


def _rope(y, cos, sin_up, sin_dn):
    w = y.shape[1]
    reps = w // LANES
    tile = lambda t: jnp.concatenate([t] * reps, axis=1)
    up = pltpu.roll(y, w - HEAD_DIM // 4, axis=1)
    dn = pltpu.roll(y, HEAD_DIM // 4, axis=1)
    return y * tile(cos) + up * tile(sin_up) + dn * tile(sin_dn)


def _qkv_body(mod_ref, ng_ref, x_ref, w_ref, cos_ref, sup_ref, sdn_ref, *out_refs, nq, nk):
    hn = _rms_mod(x_ref[...], ng_ref[...], mod_ref[1:2, :], mod_ref[0:1, :])
    y = jnp.dot(hn.astype(BF16), w_ref[...], preferred_element_type=F32)
    cos, sup, sdn = cos_ref[...], sup_ref[...], sdn_ref[...]
    if nq:
        q_ref, k_ref, v_ref = out_refs
        q_ref[...] = (_rope(y[:, :nq], cos, sup, sdn) * HEAD_DIM ** -0.5).astype(BF16)
    else:
        k_ref, v_ref = out_refs
    k_ref[...] = _rope(y[:, nq:nq + nk], cos, sup, sdn).astype(BF16)
    v_ref[...] = y[:, nq + nk:].astype(BF16)


def _qkv(x, mod_l, mod_row, ng, w, tables, nq, nk, nv, tm):
    b, s, d = x.shape
    tm = min(tm, s)
    widths = ([nq] if nq else []) + [nk, nv]
    tab_spec = pl.BlockSpec((tm, LANES), lambda bi, i: (i, 0))
    return pl.pallas_call(
        functools.partial(_qkv_body, nq=nq, nk=nk),
        grid=(b, s // tm),
        in_specs=[
            pl.BlockSpec((None, 6, d), lambda bi, i: (mod_row(bi), 0, 0)),
            _full((1, d)),
            pl.BlockSpec((None, tm, d), lambda bi, i: (bi, i, 0)),
            _full(w.shape), tab_spec, tab_spec, tab_spec,
        ],
        out_specs=[pl.BlockSpec((None, tm, n), lambda bi, i: (bi, i, 0)) for n in widths],
        out_shape=[jax.ShapeDtypeStruct((b, s, n), BF16) for n in widths],
        compiler_params=_params("arbitrary", "arbitrary"),
        name="qkv_rope" if nq else "kv_ctx",
    )(mod_l, ng, x, w, *tables)


def _attn_body(q_ref, kt_ref, kp_ref, kn_ref, vt_ref, vp_ref, vn_ref, kc_ref, vc_ref, sink_ref,
               o_ref, kwin_ref, vwin_ref, vctx_ref, bias_ref, *, seq):
    tq = q_ref.shape[0]
    nq = tq // WINDOW
    ctx = kc_ref.shape[0]
    kw = 3 * WINDOW
    rows = Q_PER_KV * WINDOW
    n_pairs = vt_ref.shape[1] // LANES
    i = pl.program_id(1)
    first_step = (pl.program_id(0) == 0) & (i == 0)

    @pl.when(first_step)
    def _():
        qi = lax.broadcasted_iota(jnp.int32, (WINDOW, kw), 0)
        kr = lax.broadcasted_iota(jnp.int32, (WINDOW, kw), 1)
        band = jnp.abs(qi - kr + WINDOW) <= WINDOW
        for e in range(4):
            ok = band
            if e & 1:
                ok = ok & (kr >= WINDOW)
            if e & 2:
                ok = ok & (kr < 2 * WINDOW)
            bias_ref[e] = jnp.where(ok, 0.0, NEG_INF)

    kwin_ref[0:WINDOW, :] = kp_ref[...]
    kwin_ref[WINDOW:WINDOW + tq, :] = kt_ref[...]
    kwin_ref[WINDOW + tq:, :] = kn_ref[...]
    for p in range(n_pairs):
        src = slice(p * LANES, (p + 1) * LANES)
        dst = slice(2 * p * LANES, (2 * p + 1) * LANES)
        one = slice((2 * p + 1) * LANES, (2 * p + 2) * LANES)
        vwin_ref[0:WINDOW, dst] = vp_ref[:, src]
        vwin_ref[WINDOW:WINDOW + tq, dst] = vt_ref[:, src]
        vwin_ref[WINDOW + tq:, dst] = vn_ref[:, src]
        vwin_ref[:, one] = jnp.ones((tq + 2 * WINDOW, LANES), BF16)
        vctx_ref[0:ctx, dst] = vc_ref[:, src]
        vctx_ref[ctx:, dst] = jnp.zeros((LANES, LANES), BF16)
        sink_row = lax.broadcasted_iota(jnp.int32, (ctx + LANES, LANES), 0) <= ctx
        vctx_ref[:, one] = jnp.where(sink_row, 1.0, 0.0).astype(BF16)
    nt = (((1,), (1,)), ((), ()))
    low = lax.broadcasted_iota(jnp.int32, (rows, LANES), 1) < HEAD_DIM
    n_blocks = seq // WINDOW

    def block(jj, carry):
        r0 = pl.multiple_of(jj * WINDOW, WINDOW)
        blk = i * nq + jj
        edge = jnp.where(blk == 0, 1, 0) + jnp.where(blk == n_blocks - 1, 2, 0)
        bias = bias_ref[edge]
        bias4 = jnp.concatenate([bias] * Q_PER_KV, axis=0)
        for p in range(n_pairs):
            q4 = jnp.concatenate(
                [q_ref[pl.ds(r0, WINDOW), (p * Q_PER_KV + g) * LANES:(p * Q_PER_KV + g + 1) * LANES]
                 for g in range(Q_PER_KV)], axis=0)
            v_loc = vwin_ref[pl.ds(r0, kw), 2 * p * LANES:(2 * p + 2) * LANES]
            v_ctx = vctx_ref[:, 2 * p * LANES:(2 * p + 2) * LANES]
            halves = []
            for half in range(2):
                h = 2 * p + half
                s_loc = lax.dot_general(q4, kwin_ref[pl.ds(r0, kw), h * LANES:(h + 1) * LANES], nt,
                                        preferred_element_type=F32) + bias4
                s_ctx = lax.dot_general(q4, kc_ref[:, h * LANES:(h + 1) * LANES], nt,
                                        preferred_element_type=F32)
                s = jnp.concatenate([s_loc, s_ctx, sink_ref[h]], axis=1)
                m = jnp.max(s, axis=1, keepdims=True)
                pe = jnp.exp(s - m).astype(BF16)
                o = (jnp.dot(pe[:, :kw], v_loc, preferred_element_type=F32)
                     + jnp.dot(pe[:, kw:], v_ctx, preferred_element_type=F32))
                halves.append(o[:, :LANES] / o[:, LANES:])
            o4 = jnp.where(low, halves[0], halves[1]).astype(o_ref.dtype)
            for g in range(Q_PER_KV):
                c0 = (p * Q_PER_KV + g) * LANES
                o_ref[pl.ds(r0, WINDOW), c0:c0 + LANES] = o4[g * WINDOW:(g + 1) * WINDOW, :]
        return carry

    lax.fori_loop(0, nq, block, 0, unroll=2)


def _attention(q, k, v, kc, vc, sink, tq):
    b, s, dq = q.shape
    nk, nv = k.shape[2], v.shape[2]
    ctx = kc.shape[1]
    nq = tq // WINDOW
    nb = s // WINDOW
    n_kv = sink.shape[0] // Q_PER_KV
    lane0 = jnp.arange(LANES) == 0
    sink_rows = jnp.repeat(sink.reshape(n_kv, Q_PER_KV), WINDOW, axis=1)
    sink_blk = jnp.where(lane0[None, None, :], sink_rows[:, :, None], NEG_INF).astype(F32)
    tile = lambda n: pl.BlockSpec((None, tq, n), lambda bi, i: (bi, i, 0))
    prev = lambda n: pl.BlockSpec((None, WINDOW, n), lambda bi, i: (bi, jnp.maximum(i * nq - 1, 0), 0))
    nxt = lambda n: pl.BlockSpec((None, WINDOW, n), lambda bi, i: (bi, jnp.minimum((i + 1) * nq, nb - 1), 0))
    cblk = lambda n: pl.BlockSpec((None, ctx, n), lambda bi, i: (bi, 0, 0))
    return pl.pallas_call(
        functools.partial(_attn_body, seq=s),
        grid=(b, s // tq),
        in_specs=[
            tile(dq), tile(nk), prev(nk), nxt(nk), tile(nv), prev(nv), nxt(nv), cblk(nk), cblk(nv),
            _full(sink_blk.shape),
        ],
        out_specs=tile(dq),
        out_shape=jax.ShapeDtypeStruct(q.shape, BF16),
        scratch_shapes=[pltpu.VMEM((tq + 2 * WINDOW, nk), BF16),
                        pltpu.VMEM((tq + 2 * WINDOW, 2 * nv), BF16),
                        pltpu.VMEM((ctx + LANES, 2 * nv), BF16),
                        pltpu.VMEM((4, WINDOW, 3 * WINDOW), F32)],
        compiler_params=_params("arbitrary", "arbitrary"),
        name="window_attn",
    )(q, k, k, k, v, v, v, kc, vc, sink_blk)


R_E1, R_E2, R_W1, R_W2, R_RANK1, R_RANK2 = range(6)
ROUTE_COLS = 8


def _attn_out_body(mod_ref, ng_ref, h_ref, o_ref, wout_ref, wrt_ref, h1_ref, xn_ref, route_ref, cnt_ref):
    tm = h_ref.shape[0]
    dh = jnp.dot(o_ref[...], wout_ref[...], preferred_element_type=F32)
    h1 = h_ref[...] + mod_ref[2:3, :] * dh
    h1_ref[...] = h1
    xn = _rms_mod(h1, ng_ref[...], mod_ref[4:5, :], mod_ref[3:4, :])
    xn_ref[...] = xn.astype(BF16)
    n_exp = wrt_ref.shape[0]
    idx = lax.broadcasted_iota(jnp.int32, (tm, n_exp), 1)
    logits = jnp.zeros((tm, n_exp), F32)
    for e in range(n_exp):
        col = jnp.sum(xn * wrt_ref[e:e + 1, :], axis=1, keepdims=True)
        logits = jnp.where(idx == e, col, logits)
    m1 = jnp.max(logits, axis=1, keepdims=True)
    i1 = jnp.min(jnp.where(logits == m1, idx, n_exp), axis=1, keepdims=True)
    rest = jnp.where(idx == i1, -jnp.inf, logits)
    m2 = jnp.max(rest, axis=1, keepdims=True)
    i2 = jnp.min(jnp.where(rest == m2, idx, n_exp), axis=1, keepdims=True)
    e2 = jnp.exp(m2 - m1)
    w1 = 1.0 / (1.0 + e2)
    w2 = e2 / (1.0 + e2)
    chosen = jnp.where((idx == i1) | (idx == i2), 1.0, 0.0)
    row = lax.broadcasted_iota(jnp.int32, (tm, tm), 0)
    col = lax.broadcasted_iota(jnp.int32, (tm, tm), 1)
    earlier = jnp.where(col < row, 1.0, 0.0).astype(BF16)
    before = jnp.dot(earlier, chosen.astype(BF16), preferred_element_type=F32)
    rank1 = jnp.sum(jnp.where(idx == i1, before, 0.0), axis=1, keepdims=True)
    rank2 = jnp.sum(jnp.where(idx == i2, before, 0.0), axis=1, keepdims=True)
    rec = jnp.zeros((tm, ROUTE_COLS), F32)
    rcol = lax.broadcasted_iota(jnp.int32, (tm, ROUTE_COLS), 1)
    for c, val in ((R_E1, i1.astype(F32)), (R_E2, i2.astype(F32)), (R_W1, w1), (R_W2, w2),
                   (R_RANK1, rank1), (R_RANK2, rank2)):
        rec = jnp.where(rcol == c, val, rec)
    route_ref[...] = rec
    cnt_ref[...] = jnp.sum(chosen, axis=0, keepdims=True)


def _attn_out(h, o, mod_l, ng, w_out, w_router_t, tm):
    b, s, d = h.shape
    n_exp = w_router_t.shape[0]
    tok = lambda n: pl.BlockSpec((None, tm, n), lambda bi, i: (bi, i, 0))
    return pl.pallas_call(
        _attn_out_body,
        grid=(b, s // tm),
        in_specs=[
            pl.BlockSpec((None, 6, d), lambda bi, i: (bi, 0, 0)),
            _full((1, d)), tok(d), tok(o.shape[2]), _full(w_out.shape), _full(w_router_t.shape),
        ],
        out_specs=[tok(d), tok(d), tok(ROUTE_COLS),
                   pl.BlockSpec((None, None, 1, n_exp), lambda bi, i: (bi, i, 0, 0))],
        out_shape=[jax.ShapeDtypeStruct(h.shape, F32), jax.ShapeDtypeStruct(h.shape, BF16),
                   jax.ShapeDtypeStruct((b, s, ROUTE_COLS), F32),
                   jax.ShapeDtypeStruct((b, s // tm, 1, n_exp), F32)],
        compiler_params=_params("arbitrary", "arbitrary"),
        name="attn_out_router",
    )(mod_l, ng, h, o, w_out, w_router_t)


ROW_ALIGN = 8
GATE_LANES = LANES
G_W1, G_W2, G_E1 = 0, 3, 6


def _chunk_sizes(max_rows):
    sizes = []
    size = ROW_ALIGN
    while size <= max_rows:
        sizes.append(size)
        size *= 2
    return tuple(reversed(sizes))


def _chunked_copies(n, src, src_row, dst, dst_row, sems, e, sizes):
    pairs = []
    for k, size in enumerate(sizes):
        done = n & (-2 * size)
        copy = pltpu.make_async_copy(
            src.at[pl.ds(pl.multiple_of(src_row + done, ROW_ALIGN), size)],
            dst.at[pl.ds(pl.multiple_of(dst_row + done, ROW_ALIGN), size)],
            sems.at[e, k])
        pairs.append(((n & size) != 0, copy))
    return pairs


def _start_all(pairs):
    for cond, copy in pairs:
        pl.when(cond)(copy.start)


def _wait_all(pairs):
    for cond, copy in pairs:
        pl.when(cond)(copy.wait)


def _selection(route, ls_ref, base, n_exp, r_cap):
    tm = route.shape[0]
    e1 = route[:, R_E1:R_E1 + 1].astype(jnp.int32)
    e2 = route[:, R_E2:R_E2 + 1].astype(jnp.int32)
    row1 = route[:, R_RANK1:R_RANK1 + 1].astype(jnp.int32)
    row2 = route[:, R_RANK2:R_RANK2 + 1].astype(jnp.int32)
    for e in range(n_exp):
        start = ls_ref[base + e]
        row1 = row1 + jnp.where(e1 == e, start, 0)
        row2 = row2 + jnp.where(e2 == e, start, 0)
    col = lax.broadcasted_iota(jnp.int32, (tm, r_cap), 1)
    return jnp.where((col == row1) | (col == row2), 1.0, 0.0).astype(BF16)


def _split3(w):
    a = w.astype(BF16).astype(F32)
    r = w - a
    b = r.astype(BF16).astype(F32)
    return a, b, r - b


def _dispatch_body(ls_ref, off_ref, n8_ref, tail_off_ref, tail_n_ref, na_ref, x_ref, route_ref, xs_ref,
                   local_ref, sems, *, n_exp, sizes, tm_moe, min_active):
    t = pl.program_id(0)
    base = t * n_exp
    tm, d = x_ref.shape
    r_cap = local_ref.shape[0]
    route = route_ref[...]
    sel = _selection(route, ls_ref, base, n_exp, r_cap)
    lane = lax.broadcasted_iota(jnp.int32, (tm, GATE_LANES), 1)
    extra = jnp.zeros((tm, GATE_LANES), F32)
    terms = _split3(route[:, R_W1:R_W1 + 1]) + _split3(route[:, R_W2:R_W2 + 1]) + (route[:, R_E1:R_E1 + 1],)
    for k, term in enumerate(terms):
        extra = jnp.where(lane == k, term, extra)
    xa = jnp.concatenate([x_ref[...], extra.astype(BF16)], axis=1)
    tn = (((0,), (0,)), ((), ()))
    rows = lax.dot_general(sel, xa, tn, preferred_element_type=F32)
    ridx = lax.broadcasted_iota(jnp.int32, (r_cap, 1), 0)
    owner = jnp.zeros((r_cap, 1), jnp.int32)
    for e in range(1, n_exp):
        owner = owner + jnp.where(ridx >= ls_ref[base + e], 1, 0)
    g = rows[:, d:]
    gate1 = g[:, G_W1:G_W1 + 1] + g[:, G_W1 + 1:G_W1 + 2] + g[:, G_W1 + 2:G_W1 + 3]
    gate2 = g[:, G_W2:G_W2 + 1] + g[:, G_W2 + 1:G_W2 + 2] + g[:, G_W2 + 2:G_W2 + 3]
    gate = jnp.where(g[:, G_E1:G_E1 + 1].astype(jnp.int32) == owner, gate1, gate2)
    local_ref[:, :d] = rows[:, :d]
    local_ref[:, d:] = jnp.broadcast_to(gate, (r_cap, GATE_LANES))

    pairs = []
    for e in range(n_exp):
        pairs += _chunked_copies(n8_ref[base + e], local_ref, ls_ref[base + e], xs_ref, off_ref[base + e],
                                 sems, e, sizes[0])
    _start_all(pairs)
    _wait_all(pairs)

    @pl.when(t == pl.num_programs(0) - 1)
    def _():
        local_ref[...] = jnp.zeros_like(local_ref)
        tails = []
        for e in range(n_exp):
            tails += _chunked_copies(tail_n_ref[e], local_ref, 0, xs_ref, tail_off_ref[e], sems, e, sizes[1])
        _start_all(tails)
        _wait_all(tails)
        n_tiles = xs_ref.shape[0] // tm_moe
        spare = []
        for k in range(n_tiles - min_active):
            tile = na_ref[0] + k
            copy = pltpu.make_async_copy(
                local_ref.at[pl.ds(0, tm_moe)],
                xs_ref.at[pl.ds(pl.multiple_of(jnp.minimum(tile, n_tiles - 1) * tm_moe, tm_moe), tm_moe)],
                sems.at[k % n_exp, k // n_exp])
            spare.append((tile < n_tiles, copy))
        _start_all(spare)
        _wait_all(spare)


def _dispatch(xn, route, tables, n_rows, tm, tm_moe):
    n_tok, d = xn.shape
    n_exp = tables["tail_n"].shape[0]
    r_cap = 2 * tm + LANES
    sizes = _chunk_sizes(tm), _chunk_sizes(tm_moe - ROW_ALIGN)
    assert tm_moe <= r_cap
    min_active = TOP_K * n_tok // tm_moe
    n_sems = max(len(sizes[0]), len(sizes[1]))
    assert n_rows // tm_moe - min_active <= n_exp * n_sems
    grid_spec = pltpu.PrefetchScalarGridSpec(
        num_scalar_prefetch=6,
        grid=(n_tok // tm,),
        in_specs=[pl.BlockSpec((tm, d), lambda t, *_: (t, 0)),
                  pl.BlockSpec((tm, ROUTE_COLS), lambda t, *_: (t, 0))],
        out_specs=pl.BlockSpec(memory_space=pl.ANY),
        scratch_shapes=[pltpu.VMEM((r_cap, d + GATE_LANES), F32),
                        pltpu.SemaphoreType.DMA((n_exp, n_sems))],
    )
    return pl.pallas_call(
        functools.partial(_dispatch_body, n_exp=n_exp, sizes=sizes, tm_moe=tm_moe, min_active=min_active),
        grid_spec=grid_spec,
        out_shape=jax.ShapeDtypeStruct((n_rows, d + GATE_LANES), F32),
        compiler_params=_params("arbitrary"),
        name="moe_dispatch",
    )(tables["ls"], tables["off"], tables["n8"], tables["tail_off"], tables["tail_n"], tables["n_active"],
      xn, route)


def _moe_body(te_ref, na_ref, x_ref, wg_ref, wu_ref, wd_ref, o_ref, xb_ref, acc_ref):
    i = pl.program_id(0)
    j = pl.program_id(1)
    d = o_ref.shape[1]

    @pl.when(i < na_ref[0])
    def _():
        @pl.when(j == 0)
        def _():
            xb_ref[...] = x_ref[:, :d].astype(BF16)
            acc_ref[...] = jnp.zeros_like(acc_ref)

        acc_ref[...] += _swiglu_part(xb_ref[...], wg_ref[...], wu_ref[...], wd_ref[...])

        @pl.when(j == pl.num_programs(1) - 1)
        def _():
            o_ref[...] = acc_ref[...] * x_ref[:, d:d + 1]

    @pl.when((i >= na_ref[0]) & (j == 0))
    def _():
        o_ref[...] = jnp.zeros_like(o_ref)


def _moe(xs, tile_expert, n_active, w_gate, w_up, w_down, tm, tf):
    n_rows, width = xs.shape
    n_exp, d, f = w_gate.shape
    nj = f // tf
    row = lambda i, j, te, na: (i, 0)
    jj = lambda i, j, na: jnp.where(i < na[0], j, nj - 1)
    grid_spec = pltpu.PrefetchScalarGridSpec(
        num_scalar_prefetch=2,
        grid=(n_rows // tm, nj),
        in_specs=[
            pl.BlockSpec((tm, width), row),
            pl.BlockSpec((None, d, tf), lambda i, j, te, na: (te[i], 0, jj(i, j, na))),
            pl.BlockSpec((None, d, tf), lambda i, j, te, na: (te[i], 0, jj(i, j, na))),
            pl.BlockSpec((None, tf, d), lambda i, j, te, na: (te[i], jj(i, j, na), 0)),
        ],
        out_specs=pl.BlockSpec((tm, d), row),
        scratch_shapes=[pltpu.VMEM((tm, d), BF16), pltpu.VMEM((tm, d), F32)],
    )
    return pl.pallas_call(
        _moe_body,
        grid_spec=grid_spec,
        out_shape=jax.ShapeDtypeStruct((n_rows, d), F32),
        compiler_params=_params("arbitrary", "arbitrary"),
        name="moe_ffn",
    )(tile_expert, n_active, xs, w_gate, w_up, w_down)


def _combine_body(ls_ref, off_ref, n8_ref, route_ref, h_ref, mod_ref, fg_ref, ys_ref, o_ref,
                  local_ref, sems, *, n_exp, sizes):
    t = pl.program_id(0) * pl.num_programs(1) + pl.program_id(1)
    base = t * n_exp
    r_cap = local_ref.shape[0]
    local_ref[...] = jnp.zeros_like(local_ref)
    pairs = []
    for e in range(n_exp):
        pairs += _chunked_copies(n8_ref[base + e], ys_ref, off_ref[base + e], local_ref, ls_ref[base + e],
                                 sems, e, sizes)
    _start_all(pairs)
    sel = _selection(route_ref[...], ls_ref, base, n_exp, r_cap)
    _wait_all(pairs)
    ys = local_ref[...]
    hi = ys.astype(BF16)
    lo = (ys - hi.astype(F32)).astype(BF16)
    y = jnp.dot(sel, hi, preferred_element_type=F32) + jnp.dot(sel, lo, preferred_element_type=F32)
    h2 = h_ref[...] + mod_ref[5:6, :] * y
    ms = jnp.mean(h2 * h2, axis=-1, keepdims=True)
    o_ref[...] = h2 * lax.rsqrt(ms + EPS) * fg_ref[...]


def _combine(ys, route, h1, mod_l, final_g, tables, tm):
    b, s, d = h1.shape
    n_exp = tables["tail_n"].shape[0]
    r_cap = 2 * tm + LANES
    sizes = _chunk_sizes(tm)
    tok = lambda n: pl.BlockSpec((None, tm, n), lambda bi, i, *_: (bi, i, 0))
    grid_spec = pltpu.PrefetchScalarGridSpec(
        num_scalar_prefetch=3,
        grid=(b, s // tm),
        in_specs=[tok(ROUTE_COLS), tok(d),
                  pl.BlockSpec((None, 6, d), lambda bi, i, *_: (bi, 0, 0)),
                  pl.BlockSpec((1, d), lambda bi, i, *_: (0, 0)),
                  pl.BlockSpec(memory_space=pl.ANY)],
        out_specs=tok(d),
        scratch_shapes=[pltpu.VMEM((r_cap, d), F32), pltpu.SemaphoreType.DMA((n_exp, len(sizes)))],
    )
    return pl.pallas_call(
        functools.partial(_combine_body, n_exp=n_exp, sizes=sizes),
        grid_spec=grid_spec,
        out_shape=jax.ShapeDtypeStruct(h1.shape, F32),
        compiler_params=_params("arbitrary", "arbitrary"),
        name="moe_combine",
    )(tables["ls"], tables["off"], tables["n8"], route, h1, mod_l, final_g, ys)


def _routing_tables(cnt, tm_moe, n_tiles_moe):
    n = cnt.astype(jnp.int32)
    n8 = (n + ROW_ALIGN - 1) // ROW_ALIGN * ROW_ALIGN
    ls = jnp.cumsum(n8, axis=1) - n8
    rows = jnp.sum(n8, axis=0)
    rows_pad = (rows + tm_moe - 1) // tm_moe * tm_moe
    ends = jnp.cumsum(rows_pad)
    ebase = ends - rows_pad
    off = ebase[None, :] + jnp.cumsum(n8, axis=0) - n8
    n_active = ends[-1] // tm_moe
    tile_start = jnp.arange(n_tiles_moe, dtype=jnp.int32) * tm_moe
    te = jnp.sum((tile_start[:, None] >= ends[None, :]).astype(jnp.int32), axis=1)
    te = jnp.where(tile_start < ends[-1], te, te[n_active - 1])
    flat = lambda a: a.reshape(-1).astype(jnp.int32)
    return dict(ls=flat(ls), off=flat(off), n8=flat(n8), tail_off=flat(ebase + rows),
                tail_n=flat(rows_pad - rows), tile_expert=te, n_active=flat(n_active))


def _rope_tables(seq):
    pos = jnp.arange(seq, dtype=jnp.int32)
    lane = jnp.arange(LANES, dtype=jnp.int32) % HEAD_DIM
    quarter = HEAD_DIM // 4
    inv = ROPE_BASE ** (-(lane % quarter).astype(F32) / quarter)
    coord = jnp.where((lane < HEAD_DIM // 2)[None, :], (pos // GRID_W)[:, None], (pos % GRID_W)[:, None])
    ang = coord.astype(F32) * inv[None, :]
    second = ((lane % (HEAD_DIM // 2)) >= quarter)[None, :]
    sin = jnp.sin(ang)
    return jnp.cos(ang), jnp.where(second, 0.0, -sin), jnp.where(second, sin, 0.0)


def _head_order(n_q_heads):
    order = []
    for p in range(n_q_heads // (2 * Q_PER_KV)):
        for g in range(Q_PER_KV):
            order += [2 * p * Q_PER_KV + g, (2 * p + 1) * Q_PER_KV + g]
    return order


def kernel(x, c, ctx, c_ctx, ada_w, ada_b, norm_g, final_g, a_w_in, a_v_g, a_v_b, a_w_s, a_b_s, a_w_out,
           b_w_qkv, b_sink, b_w_out, ffn_w_gate, ffn_w_up, ffn_w_down,
           moe_w_router, moe_w_gate, moe_w_up, moe_w_down):
    batch, seq, d = x.shape
    ctx_len = ctx.shape[1]
    depth = ada_w.shape[0]
    assert depth == 2 and seq % 1024 == 0 and ctx_len % CHUNK == 0

    mod_rows = 8
    c_rows = jnp.concatenate([c, c_ctx[None, :], jnp.zeros((mod_rows - batch - 1, d), F32)], axis=0)
    mod = _ada_mod(c_rows, ada_w, ada_b).reshape(depth, mod_rows, 6, d)
    lat_row = lambda bi: bi
    ctx_row = lambda bi: batch

    a_width = a_w_out.shape[1]
    groups = a_w_s.shape[1]
    w_in = a_w_in[0].astype(BF16)
    w_s = a_w_s[0]
    wpair = jnp.concatenate([w_s[0::2], w_s[1::2]], axis=2).astype(BF16)
    bias = jnp.repeat(a_b_s[0].T, a_width // groups, axis=1)
    w_out_a = a_w_out[0].astype(BF16)
    ng0 = norm_g[0, 0][None, :]
    ng1 = norm_g[0, 1][None, :]
    wg0, wu0, wd0 = (w[0].astype(BF16) for w in (ffn_w_gate, ffn_w_up, ffn_w_down))
    vg, vb = a_v_g[0][None, :], a_v_b[0][None, :]

    def layer0(t, row):
        t = _mixer_a(t, mod[0], row, ng0, w_in, vg, vb, wpair, bias, w_out_a, tm=512)
        return _dense_ffn(t, mod[0], row, ng1, wg0, wu0, wd0, tm=1024, tf=512)

    h = layer0(x, lat_row)
    z = layer0(ctx, ctx_row)

    w_qkv = b_w_qkv[0]
    n_heads = b_sink.shape[1]
    nq = n_heads * HEAD_DIM
    n_kv = n_heads // Q_PER_KV
    nkv = n_kv * HEAD_DIM
    order = jnp.array(_head_order(n_heads), dtype=jnp.int32)
    wq = w_qkv[:, :nq].reshape(d, n_heads, HEAD_DIM)[:, order].reshape(d, nq)
    wk = w_qkv[:, nq:nq + nkv].reshape(d, n_kv, HEAD_DIM)
    zero = jnp.zeros_like(wk)
    even = (jnp.arange(n_kv) % 2 == 0)[None, :, None]
    wk_ext = jnp.concatenate([jnp.where(even, wk, zero), jnp.where(even, zero, wk)], axis=2).reshape(d, 2 * nkv)
    wv = w_qkv[:, nq + nkv:]
    w_lat = jnp.concatenate([wq, wk_ext, wv], axis=1).astype(BF16)
    w_ctx = jnp.concatenate([wk_ext, wv], axis=1).astype(BF16)
    w_out_b = b_w_out[0].reshape(n_heads, HEAD_DIM, d)[order].reshape(nq, d).astype(BF16)
    sink = b_sink[0]
    ng0 = norm_g[1, 0][None, :]
    ng1 = norm_g[1, 1][None, :]

    tables = _rope_tables(seq)
    q, k, v = _qkv(h, mod[1], lat_row, ng0, w_lat, tables, nq, 2 * nkv, nkv, tm=512)
    no_rope = (jnp.ones((ctx_len, LANES), F32), jnp.zeros((ctx_len, LANES), F32), jnp.zeros((ctx_len, LANES), F32))
    kc, vc = _qkv(z, mod[1], ctx_row, ng0, w_ctx, no_rope, 0, 2 * nkv, nkv, tm=ctx_len)
    o = _attention(q, k, v, kc, vc, sink, tq=512)
    tm_tok, tm_moe = 512, 1024
    h1, xn, route, cnt = _attn_out(h, o, mod[1], ng1, w_out_b, moe_w_router[0].T, tm=tm_tok)
    n_exp = moe_w_router.shape[2]
    n_tok = batch * seq
    n_tok_tiles = n_tok // tm_tok
    max_rows = TOP_K * n_tok + n_tok_tiles * n_exp * (ROW_ALIGN - 1) + n_exp * (tm_moe - 1)
    n_moe_tiles = -(-max_rows // tm_moe)
    tables = _routing_tables(cnt.reshape(n_tok_tiles, n_exp), tm_moe, n_moe_tiles)
    xs = _dispatch(xn.reshape(n_tok, d), route.reshape(n_tok, ROUTE_COLS), tables, n_moe_tiles * tm_moe,
                   tm_tok, tm_moe)
    wg1, wu1, wd1 = (w[0].astype(BF16) for w in (moe_w_gate, moe_w_up, moe_w_down))
    ys = _moe(xs, tables["tile_expert"], tables["n_active"], wg1, wu1, wd1, tm=tm_moe, tf=512)
    return _combine(ys, route, h1, mod[1], final_g[None, :], tables, tm_tok)
```

```python
import functools

import jax
import jax.numpy as jnp
from jax import lax
from jax.experimental import pallas as pl
from jax.experimental.pallas import tpu as pltpu

F32 = jnp.float32
BF16 = jnp.bfloat16

EPS = 1e-6
NEG_INF = -1e30
LANES = 128
HEAD_DIM = 64
Q_PER_KV = 4
WINDOW = 128
GRID_W = 64
ROPE_BASE = 10000.0
CHUNK = 128
TOP_K = 2
VMEM_LIMIT_BYTES = 56 * 1024 * 1024
HIGHEST = lax.Precision.HIGHEST


def _params(*semantics):
    return pltpu.CompilerParams(dimension_semantics=semantics, vmem_limit_bytes=VMEM_LIMIT_BYTES)


def _rms_mod(x, gain, scale, shift):
    ms = jnp.mean(x * x, axis=-1, keepdims=True)
    return (x * lax.rsqrt(ms + EPS) * gain) * (1.0 + scale) + shift


def _full(shape):
    return pl.BlockSpec(shape, lambda *_: (0,) * len(shape))


def _ada_body(c_ref, w_ref, b_ref, o_ref):
    c = c_ref[...]
    a = c * jax.nn.sigmoid(c)
    o_ref[...] = jnp.dot(a, w_ref[...], precision=HIGHEST, preferred_element_type=F32) + b_ref[...]


def _ada_mod(c_rows, ada_w, ada_b, tn=1536):
    depth, d, n = ada_w.shape
    rows = c_rows.shape[0]
    return pl.pallas_call(
        _ada_body,
        grid=(depth, n // tn),
        in_specs=[
            pl.BlockSpec((rows, d), lambda l, j: (0, 0)),
            pl.BlockSpec((None, d, tn), lambda l, j: (l, 0, j)),
            pl.BlockSpec((None, 1, tn), lambda l, j: (l, 0, j)),
        ],
        out_specs=pl.BlockSpec((None, rows, tn), lambda l, j: (l, 0, j)),
        out_shape=jax.ShapeDtypeStruct((depth, rows, n), F32),
        compiler_params=_params("arbitrary", "arbitrary"),
        name="ada_mod",
    )(c_rows, ada_w, ada_b.reshape(depth, 1, n))


def _mixer_a_body(mod_ref, ng_ref, x_ref, win_ref, vg_ref, vb_ref, wpair_ref, bias_ref, wout_ref,
                  o_ref, vlo_ref, vhi_ref, gated_ref):
    tm, d = x_ref.shape
    a_width = wout_ref.shape[0]
    x = x_ref[...]
    hn = _rms_mod(x, ng_ref[...], mod_ref[1:2, :], mod_ref[0:1, :])
    uv = jnp.dot(hn.astype(BF16), win_ref[...], preferred_element_type=F32)
    uv = jax.nn.gelu(uv, approximate=True)
    v = uv[:, a_width:]
    mu = jnp.mean(v, axis=-1, keepdims=True)
    vc = v - mu
    var = jnp.mean(vc * vc, axis=-1, keepdims=True)
    vn = vc * lax.rsqrt(var + EPS) * vg_ref[...] + vb_ref[...]
    lane = lax.broadcasted_iota(jnp.int32, vn.shape, 1)
    low = (lane & (LANES - 1)) < (LANES // 2)
    vlo_ref[...] = jnp.where(low, vn, 0.0).astype(BF16)
    vhi_ref[...] = jnp.where(low, 0.0, vn).astype(BF16)
    for c in range(tm // CHUNK):
        rows = slice(c * CHUNK, (c + 1) * CHUNK)
        cols = []
        for j in range(a_width // LANES):
            lanes = slice(j * LANES, (j + 1) * LANES)
            rhs = jnp.concatenate([vlo_ref[rows, lanes], vhi_ref[rows, lanes]], axis=0)
            cols.append(jnp.dot(wpair_ref[j], rhs, preferred_element_type=F32))
        s = jnp.concatenate(cols, axis=1) + bias_ref[...]
        gated_ref[rows, :] = (uv[rows, :a_width] * s).astype(BF16)
    dh = jnp.dot(gated_ref[...], wout_ref[...], preferred_element_type=F32)
    o_ref[...] = x + mod_ref[2:3, :] * dh


def _mixer_a(x, mod_l, mod_row, ng, w_in, v_g, v_b, wpair, bias, w_out, tm):
    b, s, d = x.shape
    tm = min(tm, s)
    a_width = w_out.shape[0]
    return pl.pallas_call(
        _mixer_a_body,
        grid=(b, s // tm),
        in_specs=[
            pl.BlockSpec((None, 6, d), lambda bi, i: (mod_row(bi), 0, 0)),
            _full((1, d)),
            pl.BlockSpec((None, tm, d), lambda bi, i: (bi, i, 0)),
            _full(w_in.shape), _full((1, a_width)), _full((1, a_width)),
            _full(wpair.shape), _full(bias.shape), _full(w_out.shape),
        ],
        out_specs=pl.BlockSpec((None, tm, d), lambda bi, i: (bi, i, 0)),
        out_shape=jax.ShapeDtypeStruct(x.shape, F32),
        scratch_shapes=[pltpu.VMEM((tm, a_width), BF16)] * 3,
        compiler_params=_params("arbitrary", "arbitrary"),
        name="mixer_a",
    )(mod_l, ng, x, w_in, v_g, v_b, wpair, bias, w_out)


def _swiglu_part(xb, wg, wu, wd):
    g = jnp.dot(xb, wg, preferred_element_type=F32)
    u = jnp.dot(xb, wu, preferred_element_type=F32)
    a = (g * jax.nn.sigmoid(g) * u).astype(BF16)
    return jnp.dot(a, wd, preferred_element_type=F32)


def _dense_ffn_body(mod_ref, ng_ref, x_ref, wg_ref, wu_ref, wd_ref, o_ref, xn_ref, acc_ref):
    j = pl.program_id(2)

    @pl.when(j == 0)
    def _():
        xn_ref[...] = _rms_mod(x_ref[...], ng_ref[...], mod_ref[4:5, :], mod_ref[3:4, :]).astype(BF16)
        acc_ref[...] = jnp.zeros_like(acc_ref)

    acc_ref[...] += _swiglu_part(xn_ref[...], wg_ref[...], wu_ref[...], wd_ref[...])

    @pl.when(j == pl.num_programs(2) - 1)
    def _():
        o_ref[...] = x_ref[...] + mod_ref[5:6, :] * acc_ref[...]


def _dense_ffn(x, mod_l, mod_row, ng, w_gate, w_up, w_down, tm, tf):
    b, s, d = x.shape
    tm = min(tm, s)
    f = w_gate.shape[1]
    return pl.pallas_call(
        _dense_ffn_body,
        grid=(b, s // tm, f // tf),
        in_specs=[
            pl.BlockSpec((None, 6, d), lambda bi, i, j: (mod_row(bi), 0, 0)),
            _full((1, d)),
            pl.BlockSpec((None, tm, d), lambda bi, i, j: (bi, i, 0)),
            pl.BlockSpec((d, tf), lambda bi, i, j: (0, j)),
            pl.BlockSpec((d, tf), lambda bi, i, j: (0, j)),
            pl.BlockSpec((tf, d), lambda bi, i, j: (j, 0)),
        ],
        out_specs=pl.BlockSpec((None, tm, d), lambda bi, i, j: (bi, i, 0)),
        out_shape=jax.ShapeDtypeStruct(x.shape, F32),
        scratch_shapes=[pltpu.VMEM((tm, d), BF16), pltpu.VMEM((tm, d), F32)],
        compiler_params=_params("arbitrary", "arbitrary", "arbitrary"),
        name="dense_ffn",
    )(mod_l, ng, x, w_gate, w_up, w_down)


def _rope(y, cos, sin_up, sin_dn):
    w = y.shape[1]
    reps = w // LANES
    tile = lambda t: jnp.concatenate([t] * reps, axis=1)
    up = pltpu.roll(y, w - HEAD_DIM // 4, axis=1)
    dn = pltpu.roll(y, HEAD_DIM // 4, axis=1)
    return y * tile(cos) + up * tile(sin_up) + dn * tile(sin_dn)


def _qkv_body(mod_ref, ng_ref, x_ref, w_ref, cos_ref, sup_ref, sdn_ref, *out_refs, nq, nk):
    hn = _rms_mod(x_ref[...], ng_ref[...], mod_ref[1:2, :], mod_ref[0:1, :])
    y = jnp.dot(hn.astype(BF16), w_ref[...], preferred_element_type=F32)
    cos, sup, sdn = cos_ref[...], sup_ref[...], sdn_ref[...]
    if nq:
        q_ref, k_ref, v_ref = out_refs
        q_ref[...] = (_rope(y[:, :nq], cos, sup, sdn) * HEAD_DIM ** -0.5).astype(BF16)
    else:
        k_ref, v_ref = out_refs
    k_ref[...] = _rope(y[:, nq:nq + nk], cos, sup, sdn).astype(BF16)
    v_ref[...] = y[:, nq + nk:].astype(BF16)


def _qkv(x, mod_l, mod_row, ng, w, tables, nq, nk, nv, tm):
    b, s, d = x.shape
    tm = min(tm, s)
    widths = ([nq] if nq else []) + [nk, nv]
    tab_spec = pl.BlockSpec((tm, LANES), lambda bi, i: (i, 0))
    return pl.pallas_call(
        functools.partial(_qkv_body, nq=nq, nk=nk),
        grid=(b, s // tm),
        in_specs=[
            pl.BlockSpec((None, 6, d), lambda bi, i: (mod_row(bi), 0, 0)),
            _full((1, d)),
            pl.BlockSpec((None, tm, d), lambda bi, i: (bi, i, 0)),
            _full(w.shape), tab_spec, tab_spec, tab_spec,
        ],
        out_specs=[pl.BlockSpec((None, tm, n), lambda bi, i: (bi, i, 0)) for n in widths],
        out_shape=[jax.ShapeDtypeStruct((b, s, n), BF16) for n in widths],
        compiler_params=_params("arbitrary", "arbitrary"),
        name="qkv_rope" if nq else "kv_ctx",
    )(mod_l, ng, x, w, *tables)


def _attn_body(q_ref, kt_ref, kp_ref, kn_ref, vt_ref, vp_ref, vn_ref, kc_ref, vc_ref, sink_ref,
               o_ref, kwin_ref, vwin_ref, vctx_ref, bias_ref, *, seq):
    tq = q_ref.shape[0]
    nq = tq // WINDOW
    ctx = kc_ref.shape[0]
    kw = 3 * WINDOW
    rows = Q_PER_KV * WINDOW
    n_pairs = vt_ref.shape[1] // LANES
    i = pl.program_id(1)
    first_step = (pl.program_id(0) == 0) & (i == 0)

    @pl.when(first_step)
    def _():
        qi = lax.broadcasted_iota(jnp.int32, (WINDOW, kw), 0)
        kr = lax.broadcasted_iota(jnp.int32, (WINDOW, kw), 1)
        band = jnp.abs(qi - kr + WINDOW) <= WINDOW
        for e in range(4):
            ok = band
            if e & 1:
                ok = ok & (kr >= WINDOW)
            if e & 2:
                ok = ok & (kr < 2 * WINDOW)
            bias_ref[e] = jnp.where(ok, 0.0, NEG_INF)

    kwin_ref[0:WINDOW, :] = kp_ref[...]
    kwin_ref[WINDOW:WINDOW + tq, :] = kt_ref[...]
    kwin_ref[WINDOW + tq:, :] = kn_ref[...]
    for p in range(n_pairs):
        src = slice(p * LANES, (p + 1) * LANES)
        dst = slice(2 * p * LANES, (2 * p + 1) * LANES)
        one = slice((2 * p + 1) * LANES, (2 * p + 2) * LANES)
        vwin_ref[0:WINDOW, dst] = vp_ref[:, src]
        vwin_ref[WINDOW:WINDOW + tq, dst] = vt_ref[:, src]
        vwin_ref[WINDOW + tq:, dst] = vn_ref[:, src]
        vwin_ref[:, one] = jnp.ones((tq + 2 * WINDOW, LANES), BF16)
        vctx_ref[0:ctx, dst] = vc_ref[:, src]
        vctx_ref[ctx:, dst] = jnp.zeros((LANES, LANES), BF16)
        sink_row = lax.broadcasted_iota(jnp.int32, (ctx + LANES, LANES), 0) <= ctx
        vctx_ref[:, one] = jnp.where(sink_row, 1.0, 0.0).astype(BF16)
    nt = (((1,), (1,)), ((), ()))
    low = lax.broadcasted_iota(jnp.int32, (rows, LANES), 1) < HEAD_DIM
    n_blocks = seq // WINDOW

    def block(jj, carry):
        r0 = pl.multiple_of(jj * WINDOW, WINDOW)
        blk = i * nq + jj
        edge = jnp.where(blk == 0, 1, 0) + jnp.where(blk == n_blocks - 1, 2, 0)
        bias = bias_ref[edge]
        bias4 = jnp.concatenate([bias] * Q_PER_KV, axis=0)
        for p in range(n_pairs):
            q4 = jnp.concatenate(
                [q_ref[pl.ds(r0, WINDOW), (p * Q_PER_KV + g) * LANES:(p * Q_PER_KV + g + 1) * LANES]
                 for g in range(Q_PER_KV)], axis=0)
            v_loc = vwin_ref[pl.ds(r0, kw), 2 * p * LANES:(2 * p + 2) * LANES]
            v_ctx = vctx_ref[:, 2 * p * LANES:(2 * p + 2) * LANES]
            halves = []
            for half in range(2):
                h = 2 * p + half
                s_loc = lax.dot_general(q4, kwin_ref[pl.ds(r0, kw), h * LANES:(h + 1) * LANES], nt,
                                        preferred_element_type=F32) + bias4
                s_ctx = lax.dot_general(q4, kc_ref[:, h * LANES:(h + 1) * LANES], nt,
                                        preferred_element_type=F32)
                s = jnp.concatenate([s_loc, s_ctx, sink_ref[h]], axis=1)
                m = jnp.max(s, axis=1, keepdims=True)
                pe = jnp.exp(s - m).astype(BF16)
                o = (jnp.dot(pe[:, :kw], v_loc, preferred_element_type=F32)
                     + jnp.dot(pe[:, kw:], v_ctx, preferred_element_type=F32))
                halves.append(o[:, :LANES] / o[:, LANES:])
            o4 = jnp.where(low, halves[0], halves[1]).astype(o_ref.dtype)
            for g in range(Q_PER_KV):
                c0 = (p * Q_PER_KV + g) * LANES
                o_ref[pl.ds(r0, WINDOW), c0:c0 + LANES] = o4[g * WINDOW:(g + 1) * WINDOW, :]
        return carry

    lax.fori_loop(0, nq, block, 0, unroll=2)


def _attention(q, k, v, kc, vc, sink, tq):
    b, s, dq = q.shape
    nk, nv = k.shape[2], v.shape[2]
    ctx = kc.shape[1]
    nq = tq // WINDOW
    nb = s // WINDOW
    n_kv = sink.shape[0] // Q_PER_KV
    lane0 = jnp.arange(LANES) == 0
    sink_rows = jnp.repeat(sink.reshape(n_kv, Q_PER_KV), WINDOW, axis=1)
    sink_blk = jnp.where(lane0[None, None, :], sink_rows[:, :, None], NEG_INF).astype(F32)
    tile = lambda n: pl.BlockSpec((None, tq, n), lambda bi, i: (bi, i, 0))
    prev = lambda n: pl.BlockSpec((None, WINDOW, n), lambda bi, i: (bi, jnp.maximum(i * nq - 1, 0), 0))
    nxt = lambda n: pl.BlockSpec((None, WINDOW, n), lambda bi, i: (bi, jnp.minimum((i + 1) * nq, nb - 1), 0))
    cblk = lambda n: pl.BlockSpec((None, ctx, n), lambda bi, i: (bi, 0, 0))
    return pl.pallas_call(
        functools.partial(_attn_body, seq=s),
        grid=(b, s // tq),
        in_specs=[
            tile(dq), tile(nk), prev(nk), nxt(nk), tile(nv), prev(nv), nxt(nv), cblk(nk), cblk(nv),
            _full(sink_blk.shape),
        ],
        out_specs=tile(dq),
        out_shape=jax.ShapeDtypeStruct(q.shape, BF16),
        scratch_shapes=[pltpu.VMEM((tq + 2 * WINDOW, nk), BF16),
                        pltpu.VMEM((tq + 2 * WINDOW, 2 * nv), BF16),
                        pltpu.VMEM((ctx + LANES, 2 * nv), BF16),
                        pltpu.VMEM((4, WINDOW, 3 * WINDOW), F32)],
        compiler_params=_params("arbitrary", "arbitrary"),
        name="window_attn",
    )(q, k, k, k, v, v, v, kc, vc, sink_blk)


R_E1, R_E2, R_W1, R_W2, R_RANK1, R_RANK2 = range(6)
ROUTE_COLS = 8
T_E1, T_E2, T_RANK1_HI, T_RANK1_LO, T_RANK2_HI, T_RANK2_LO = range(6)
RANK_RADIX = 16


def _attn_out_body(mod_ref, ng_ref, h_ref, o_ref, wout_ref, wrt_ref, h1_ref, xn_ref, route_ref, cnt_ref,
                   route_t_ref):
    tm = h_ref.shape[0]
    dh = jnp.dot(o_ref[...], wout_ref[...], preferred_element_type=F32)
    h1 = h_ref[...] + mod_ref[2:3, :] * dh
    h1_ref[...] = h1
    xn = _rms_mod(h1, ng_ref[...], mod_ref[4:5, :], mod_ref[3:4, :])
    xn_ref[...] = xn.astype(BF16)
    n_exp = wrt_ref.shape[0]
    idx = lax.broadcasted_iota(jnp.int32, (tm, n_exp), 1)
    logits = jnp.zeros((tm, n_exp), F32)
    for e in range(n_exp):
        col = jnp.sum(xn * wrt_ref[e:e + 1, :], axis=1, keepdims=True)
        logits = jnp.where(idx == e, col, logits)
    m1 = jnp.max(logits, axis=1, keepdims=True)
    i1 = jnp.min(jnp.where(logits == m1, idx, n_exp), axis=1, keepdims=True)
    rest = jnp.where(idx == i1, -jnp.inf, logits)
    m2 = jnp.max(rest, axis=1, keepdims=True)
    i2 = jnp.min(jnp.where(rest == m2, idx, n_exp), axis=1, keepdims=True)
    e2 = jnp.exp(m2 - m1)
    w1 = 1.0 / (1.0 + e2)
    w2 = e2 / (1.0 + e2)
    chosen = jnp.where((idx == i1) | (idx == i2), 1.0, 0.0)
    row = lax.broadcasted_iota(jnp.int32, (tm, tm), 0)
    col = lax.broadcasted_iota(jnp.int32, (tm, tm), 1)
    earlier = jnp.where(col < row, 1.0, 0.0).astype(BF16)
    before = jnp.dot(earlier, chosen.astype(BF16), preferred_element_type=F32)
    rank1 = jnp.sum(jnp.where(idx == i1, before, 0.0), axis=1, keepdims=True)
    rank2 = jnp.sum(jnp.where(idx == i2, before, 0.0), axis=1, keepdims=True)
    rec = jnp.zeros((tm, ROUTE_COLS), F32)
    rcol = lax.broadcasted_iota(jnp.int32, (tm, ROUTE_COLS), 1)
    for c, val in ((R_E1, i1.astype(F32)), (R_E2, i2.astype(F32)), (R_W1, w1), (R_W2, w2),
                   (R_RANK1, rank1), (R_RANK2, rank2)):
        rec = jnp.where(rcol == c, val, rec)
    route_ref[...] = rec
    cnt_ref[...] = jnp.sum(chosen, axis=0, keepdims=True)
    ints = jnp.zeros((tm, ROUTE_COLS), F32)
    for c, val in ((T_E1, i1.astype(F32)), (T_E2, i2.astype(F32)),
                   (T_RANK1_HI, jnp.floor(rank1 * (1.0 / RANK_RADIX))),
                   (T_RANK2_HI, jnp.floor(rank2 * (1.0 / RANK_RADIX)))):
        ints = jnp.where(rcol == c, val, ints)
    ints = jnp.where(rcol == T_RANK1_LO, rank1 - RANK_RADIX * jnp.floor(rank1 * (1.0 / RANK_RADIX)), ints)
    ints = jnp.where(rcol == T_RANK2_LO, rank2 - RANK_RADIX * jnp.floor(rank2 * (1.0 / RANK_RADIX)), ints)
    eye = jnp.where(lax.broadcasted_iota(jnp.int32, (ROUTE_COLS, ROUTE_COLS), 0)
                    == lax.broadcasted_iota(jnp.int32, (ROUTE_COLS, ROUTE_COLS), 1), 1.0, 0.0)
    nt = (((1,), (1,)), ((), ()))
    route_t_ref[...] = lax.dot_general(eye.astype(BF16), ints.astype(BF16), nt, preferred_element_type=F32)


def _attn_out(h, o, mod_l, ng, w_out, w_router_t, tm):
    b, s, d = h.shape
    n_exp = w_router_t.shape[0]
    tok = lambda n: pl.BlockSpec((None, tm, n), lambda bi, i: (bi, i, 0))
    return pl.pallas_call(
        _attn_out_body,
        grid=(b, s // tm),
        in_specs=[
            pl.BlockSpec((None, 6, d), lambda bi, i: (bi, 0, 0)),
            _full((1, d)), tok(d), tok(o.shape[2]), _full(w_out.shape), _full(w_router_t.shape),
        ],
        out_specs=[tok(d), tok(d), tok(ROUTE_COLS),
                   pl.BlockSpec((None, None, 1, n_exp), lambda bi, i: (bi, i, 0, 0)),
                   pl.BlockSpec((None, None, ROUTE_COLS, tm), lambda bi, i: (bi, i, 0, 0))],
        out_shape=[jax.ShapeDtypeStruct(h.shape, F32), jax.ShapeDtypeStruct(h.shape, BF16),
                   jax.ShapeDtypeStruct((b, s, ROUTE_COLS), F32),
                   jax.ShapeDtypeStruct((b, s // tm, 1, n_exp), F32),
                   jax.ShapeDtypeStruct((b, s // tm, ROUTE_COLS, tm), F32)],
        compiler_params=_params("arbitrary", "arbitrary"),
        name="attn_out_router",
    )(mod_l, ng, h, o, w_out, w_router_t)


ROW_ALIGN = 8
GATE_LANES = LANES
G_W1, G_W2, G_E1 = 0, 3, 6


def _chunk_sizes(max_rows):
    sizes = []
    size = ROW_ALIGN
    while size <= max_rows:
        sizes.append(size)
        size *= 2
    return tuple(reversed(sizes))


def _chunked_copies(n, src, src_row, dst, dst_row, sems, e, sizes):
    pairs = []
    for k, size in enumerate(sizes):
        done = n & (-2 * size)
        copy = pltpu.make_async_copy(
            src.at[pl.ds(pl.multiple_of(src_row + done, ROW_ALIGN), size)],
            dst.at[pl.ds(pl.multiple_of(dst_row + done, ROW_ALIGN), size)],
            sems.at[e, k])
        pairs.append(((n & size) != 0, copy))
    return pairs


def _start_all(pairs):
    for cond, copy in pairs:
        pl.when(cond)(copy.start)


def _wait_all(pairs):
    for cond, copy in pairs:
        pl.when(cond)(copy.wait)


def _selection(route, ls_ref, base, n_exp, r_cap):
    tm = route.shape[0]
    e1 = route[:, R_E1:R_E1 + 1].astype(jnp.int32)
    e2 = route[:, R_E2:R_E2 + 1].astype(jnp.int32)
    row1 = route[:, R_RANK1:R_RANK1 + 1].astype(jnp.int32)
    row2 = route[:, R_RANK2:R_RANK2 + 1].astype(jnp.int32)
    for e in range(n_exp):
        start = ls_ref[base + e]
        row1 = row1 + jnp.where(e1 == e, start, 0)
        row2 = row2 + jnp.where(e2 == e, start, 0)
    col = lax.broadcasted_iota(jnp.int32, (tm, r_cap), 1)
    return jnp.where((col == row1) | (col == row2), 1.0, 0.0).astype(BF16)


def _split3(w):
    a = w.astype(BF16).astype(F32)
    r = w - a
    b = r.astype(BF16).astype(F32)
    return a, b, r - b


def _selection_t(route_t, ls_ref, base, n_exp, r_cap):
    tm = route_t.shape[1]
    e1 = route_t[T_E1:T_E1 + 1, :].astype(jnp.int32)
    e2 = route_t[T_E2:T_E2 + 1, :].astype(jnp.int32)
    row1 = (route_t[T_RANK1_HI:T_RANK1_HI + 1, :] * RANK_RADIX + route_t[T_RANK1_LO:T_RANK1_LO + 1, :]).astype(jnp.int32)
    row2 = (route_t[T_RANK2_HI:T_RANK2_HI + 1, :] * RANK_RADIX + route_t[T_RANK2_LO:T_RANK2_LO + 1, :]).astype(jnp.int32)
    for e in range(n_exp):
        start = ls_ref[base + e]
        row1 = row1 + jnp.where(e1 == e, start, 0)
        row2 = row2 + jnp.where(e2 == e, start, 0)
    ridx = lax.broadcasted_iota(jnp.int32, (r_cap, tm), 0)
    return jnp.where((ridx == row1) | (ridx == row2), 1.0, 0.0).astype(BF16)


def _dispatch_body(ls_ref, off_ref, n8_ref, tail_off_ref, tail_n_ref, na_ref, x_ref, route_ref, route_t_ref,
                   xs_ref, local_ref, sems, *, n_exp, sizes, tm_moe, min_active):
    t = pl.program_id(0)
    base = t * n_exp
    tm, d = x_ref.shape
    r_cap = local_ref.shape[0]
    route = route_ref[...]
    sel = _selection_t(route_t_ref[...], ls_ref, base, n_exp, r_cap)
    lane = lax.broadcasted_iota(jnp.int32, (tm, GATE_LANES), 1)
    extra = jnp.zeros((tm, GATE_LANES), F32)
    terms = _split3(route[:, R_W1:R_W1 + 1]) + _split3(route[:, R_W2:R_W2 + 1]) + (route[:, R_E1:R_E1 + 1],)
    for k, term in enumerate(terms):
        extra = jnp.where(lane == k, term, extra)
    xa = jnp.concatenate([x_ref[...], extra.astype(BF16)], axis=1)
    rows = jnp.dot(sel, xa, preferred_element_type=F32)
    ridx = lax.broadcasted_iota(jnp.int32, (r_cap, 1), 0)
    owner = jnp.zeros((r_cap, 1), jnp.int32)
    for e in range(1, n_exp):
        owner = owner + jnp.where(ridx >= ls_ref[base + e], 1, 0)
    g = rows[:, d:]
    gate1 = g[:, G_W1:G_W1 + 1] + g[:, G_W1 + 1:G_W1 + 2] + g[:, G_W1 + 2:G_W1 + 3]
    gate2 = g[:, G_W2:G_W2 + 1] + g[:, G_W2 + 1:G_W2 + 2] + g[:, G_W2 + 2:G_W2 + 3]
    gate = jnp.where(g[:, G_E1:G_E1 + 1].astype(jnp.int32) == owner, gate1, gate2)
    local_ref[:, :d] = rows[:, :d]
    local_ref[:, d:] = jnp.broadcast_to(gate, (r_cap, GATE_LANES))

    pairs = []
    for e in range(n_exp):
        pairs += _chunked_copies(n8_ref[base + e], local_ref, ls_ref[base + e], xs_ref, off_ref[base + e],
                                 sems, e, sizes[0])
    _start_all(pairs)
    _wait_all(pairs)

    @pl.when(t == pl.num_programs(0) - 1)
    def _():
        local_ref[...] = jnp.zeros_like(local_ref)
        tails = []
        for e in range(n_exp):
            tails += _chunked_copies(tail_n_ref[e], local_ref, 0, xs_ref, tail_off_ref[e], sems, e, sizes[1])
        _start_all(tails)
        _wait_all(tails)
        n_tiles = xs_ref.shape[0] // tm_moe
        spare = []
        for k in range(n_tiles - min_active):
            tile = na_ref[0] + k
            copy = pltpu.make_async_copy(
                local_ref.at[pl.ds(0, tm_moe)],
                xs_ref.at[pl.ds(pl.multiple_of(jnp.minimum(tile, n_tiles - 1) * tm_moe, tm_moe), tm_moe)],
                sems.at[k % n_exp, k // n_exp])
            spare.append((tile < n_tiles, copy))
        _start_all(spare)
        _wait_all(spare)


def _dispatch(xn, route, route_t, tables, n_rows, tm, tm_moe):
    n_tok, d = xn.shape
    n_exp = tables["tail_n"].shape[0]
    r_cap = 2 * tm + LANES
    sizes = _chunk_sizes(tm), _chunk_sizes(tm_moe - ROW_ALIGN)
    assert tm_moe <= r_cap
    min_active = TOP_K * n_tok // tm_moe
    n_sems = max(len(sizes[0]), len(sizes[1]))
    assert n_rows // tm_moe - min_active <= n_exp * n_sems
    grid_spec = pltpu.PrefetchScalarGridSpec(
        num_scalar_prefetch=6,
        grid=(n_tok // tm,),
        in_specs=[pl.BlockSpec((tm, d), lambda t, *_: (t, 0)),
                  pl.BlockSpec((tm, ROUTE_COLS), lambda t, *_: (t, 0)),
                  pl.BlockSpec((None, ROUTE_COLS, tm), lambda t, *_: (t, 0, 0))],
        out_specs=pl.BlockSpec(memory_space=pl.ANY),
        scratch_shapes=[pltpu.VMEM((r_cap, d + GATE_LANES), F32),
                        pltpu.SemaphoreType.DMA((n_exp, n_sems))],
    )
    return pl.pallas_call(
        functools.partial(_dispatch_body, n_exp=n_exp, sizes=sizes, tm_moe=tm_moe, min_active=min_active),
        grid_spec=grid_spec,
        out_shape=jax.ShapeDtypeStruct((n_rows, d + GATE_LANES), F32),
        compiler_params=_params("arbitrary"),
        name="moe_dispatch",
    )(tables["ls"], tables["off"], tables["n8"], tables["tail_off"], tables["tail_n"], tables["n_active"],
      xn, route, route_t)


def _moe_body(te_ref, na_ref, x_ref, wg_ref, wu_ref, wd_ref, o_ref, xb_ref, acc_ref):
    i = pl.program_id(0)
    j = pl.program_id(1)
    d = o_ref.shape[1]

    @pl.when(i < na_ref[0])
    def _():
        @pl.when(j == 0)
        def _():
            xb_ref[...] = x_ref[:, :d].astype(BF16)
            acc_ref[...] = jnp.zeros_like(acc_ref)

        acc_ref[...] += _swiglu_part(xb_ref[...], wg_ref[...], wu_ref[...], wd_ref[...])

        @pl.when(j == pl.num_programs(1) - 1)
        def _():
            o_ref[...] = acc_ref[...] * x_ref[:, d:d + 1]

    @pl.when((i >= na_ref[0]) & (j == 0))
    def _():
        o_ref[...] = jnp.zeros_like(o_ref)


def _moe(xs, tile_expert, n_active, w_gate, w_up, w_down, tm, tf):
    n_rows, width = xs.shape
    n_exp, d, f = w_gate.shape
    nj = f // tf
    row = lambda i, j, te, na: (i, 0)
    jj = lambda i, j, na: jnp.where(i < na[0], j, nj - 1)
    grid_spec = pltpu.PrefetchScalarGridSpec(
        num_scalar_prefetch=2,
        grid=(n_rows // tm, nj),
        in_specs=[
            pl.BlockSpec((tm, width), row),
            pl.BlockSpec((None, d, tf), lambda i, j, te, na: (te[i], 0, jj(i, j, na))),
            pl.BlockSpec((None, d, tf), lambda i, j, te, na: (te[i], 0, jj(i, j, na))),
            pl.BlockSpec((None, tf, d), lambda i, j, te, na: (te[i], jj(i, j, na), 0)),
        ],
        out_specs=pl.BlockSpec((tm, d), row),
        scratch_shapes=[pltpu.VMEM((tm, d), BF16), pltpu.VMEM((tm, d), F32)],
    )
    return pl.pallas_call(
        _moe_body,
        grid_spec=grid_spec,
        out_shape=jax.ShapeDtypeStruct((n_rows, d), F32),
        compiler_params=_params("arbitrary", "arbitrary"),
        name="moe_ffn",
    )(tile_expert, n_active, xs, w_gate, w_up, w_down)


def _combine_body(ls_ref, off_ref, n8_ref, route_ref, h_ref, mod_ref, fg_ref, ys_ref, o_ref,
                  local_ref, sems, *, n_exp, sizes):
    t = pl.program_id(0) * pl.num_programs(1) + pl.program_id(1)
    base = t * n_exp
    r_cap = local_ref.shape[0]
    local_ref[...] = jnp.zeros_like(local_ref)
    pairs = []
    for e in range(n_exp):
        pairs += _chunked_copies(n8_ref[base + e], ys_ref, off_ref[base + e], local_ref, ls_ref[base + e],
                                 sems, e, sizes)
    _start_all(pairs)
    sel = _selection(route_ref[...], ls_ref, base, n_exp, r_cap)
    _wait_all(pairs)
    y = jnp.dot(sel, local_ref[...].astype(BF16), preferred_element_type=F32)
    h2 = h_ref[...] + mod_ref[5:6, :] * y
    ms = jnp.mean(h2 * h2, axis=-1, keepdims=True)
    o_ref[...] = h2 * lax.rsqrt(ms + EPS) * fg_ref[...]


def _combine(ys, route, h1, mod_l, final_g, tables, tm):
    b, s, d = h1.shape
    n_exp = tables["tail_n"].shape[0]
    r_cap = 2 * tm + LANES
    sizes = _chunk_sizes(tm)
    tok = lambda n: pl.BlockSpec((None, tm, n), lambda bi, i, *_: (bi, i, 0))
    grid_spec = pltpu.PrefetchScalarGridSpec(
        num_scalar_prefetch=3,
        grid=(b, s // tm),
        in_specs=[tok(ROUTE_COLS), tok(d),
                  pl.BlockSpec((None, 6, d), lambda bi, i, *_: (bi, 0, 0)),
                  pl.BlockSpec((1, d), lambda bi, i, *_: (0, 0)),
                  pl.BlockSpec(memory_space=pl.ANY)],
        out_specs=tok(d),
        scratch_shapes=[pltpu.VMEM((r_cap, d), F32), pltpu.SemaphoreType.DMA((n_exp, len(sizes)))],
    )
    return pl.pallas_call(
        functools.partial(_combine_body, n_exp=n_exp, sizes=sizes),
        grid_spec=grid_spec,
        out_shape=jax.ShapeDtypeStruct(h1.shape, F32),
        compiler_params=_params("arbitrary", "arbitrary"),
        name="moe_combine",
    )(tables["ls"], tables["off"], tables["n8"], route, h1, mod_l, final_g, ys)


def _routing_tables(cnt, tm_moe, n_tiles_moe):
    n = cnt.astype(jnp.int32)
    n8 = (n + ROW_ALIGN - 1) // ROW_ALIGN * ROW_ALIGN
    ls = jnp.cumsum(n8, axis=1) - n8
    rows = jnp.sum(n8, axis=0)
    rows_pad = (rows + tm_moe - 1) // tm_moe * tm_moe
    ends = jnp.cumsum(rows_pad)
    ebase = ends - rows_pad
    off = ebase[None, :] + jnp.cumsum(n8, axis=0) - n8
    n_active = ends[-1] // tm_moe
    tile_start = jnp.arange(n_tiles_moe, dtype=jnp.int32) * tm_moe
    te = jnp.sum((tile_start[:, None] >= ends[None, :]).astype(jnp.int32), axis=1)
    te = jnp.where(tile_start < ends[-1], te, te[n_active - 1])
    flat = lambda a: a.reshape(-1).astype(jnp.int32)
    return dict(ls=flat(ls), off=flat(off), n8=flat(n8), tail_off=flat(ebase + rows),
                tail_n=flat(rows_pad - rows), tile_expert=te, n_active=flat(n_active))


def _rope_tables(seq):
    pos = jnp.arange(seq, dtype=jnp.int32)
    lane = jnp.arange(LANES, dtype=jnp.int32) % HEAD_DIM
    quarter = HEAD_DIM // 4
    inv = ROPE_BASE ** (-(lane % quarter).astype(F32) / quarter)
    coord = jnp.where((lane < HEAD_DIM // 2)[None, :], (pos // GRID_W)[:, None], (pos % GRID_W)[:, None])
    ang = coord.astype(F32) * inv[None, :]
    second = ((lane % (HEAD_DIM // 2)) >= quarter)[None, :]
    sin = jnp.sin(ang)
    return jnp.cos(ang), jnp.where(second, 0.0, -sin), jnp.where(second, sin, 0.0)


def _head_order(n_q_heads):
    order = []
    for p in range(n_q_heads // (2 * Q_PER_KV)):
        for g in range(Q_PER_KV):
            order += [2 * p * Q_PER_KV + g, (2 * p + 1) * Q_PER_KV + g]
    return order


def kernel(x, c, ctx, c_ctx, ada_w, ada_b, norm_g, final_g, a_w_in, a_v_g, a_v_b, a_w_s, a_b_s, a_w_out,
           b_w_qkv, b_sink, b_w_out, ffn_w_gate, ffn_w_up, ffn_w_down,
           moe_w_router, moe_w_gate, moe_w_up, moe_w_down):
    batch, seq, d = x.shape
    ctx_len = ctx.shape[1]
    depth = ada_w.shape[0]
    assert depth == 2 and seq % 1024 == 0 and ctx_len % CHUNK == 0

    mod_rows = 8
    c_rows = jnp.concatenate([c, c_ctx[None, :], jnp.zeros((mod_rows - batch - 1, d), F32)], axis=0)
    mod = _ada_mod(c_rows, ada_w, ada_b).reshape(depth, mod_rows, 6, d)
    lat_row = lambda bi: bi
    ctx_row = lambda bi: batch

    a_width = a_w_out.shape[1]
    groups = a_w_s.shape[1]
    w_in = a_w_in[0].astype(BF16)
    w_s = a_w_s[0]
    wpair = jnp.concatenate([w_s[0::2], w_s[1::2]], axis=2).astype(BF16)
    bias = jnp.repeat(a_b_s[0].T, a_width // groups, axis=1)
    w_out_a = a_w_out[0].astype(BF16)
    ng0 = norm_g[0, 0][None, :]
    ng1 = norm_g[0, 1][None, :]
    wg0, wu0, wd0 = (w[0].astype(BF16) for w in (ffn_w_gate, ffn_w_up, ffn_w_down))
    vg, vb = a_v_g[0][None, :], a_v_b[0][None, :]

    def layer0(t, row):
        t = _mixer_a(t, mod[0], row, ng0, w_in, vg, vb, wpair, bias, w_out_a, tm=512)
        return _dense_ffn(t, mod[0], row, ng1, wg0, wu0, wd0, tm=512, tf=1792)

    h = layer0(x, lat_row)
    z = layer0(ctx, ctx_row)

    w_qkv = b_w_qkv[0]
    n_heads = b_sink.shape[1]
    nq = n_heads * HEAD_DIM
    n_kv = n_heads // Q_PER_KV
    nkv = n_kv * HEAD_DIM
    order = jnp.array(_head_order(n_heads), dtype=jnp.int32)
    wq = w_qkv[:, :nq].reshape(d, n_heads, HEAD_DIM)[:, order].reshape(d, nq)
    wk = w_qkv[:, nq:nq + nkv].reshape(d, n_kv, HEAD_DIM)
    zero = jnp.zeros_like(wk)
    even = (jnp.arange(n_kv) % 2 == 0)[None, :, None]
    wk_ext = jnp.concatenate([jnp.where(even, wk, zero), jnp.where(even, zero, wk)], axis=2).reshape(d, 2 * nkv)
    wv = w_qkv[:, nq + nkv:]
    w_lat = jnp.concatenate([wq, wk_ext, wv], axis=1).astype(BF16)
    w_ctx = jnp.concatenate([wk_ext, wv], axis=1).astype(BF16)
    w_out_b = b_w_out[0].reshape(n_heads, HEAD_DIM, d)[order].reshape(nq, d).astype(BF16)
    sink = b_sink[0]
    ng0 = norm_g[1, 0][None, :]
    ng1 = norm_g[1, 1][None, :]

    tables = _rope_tables(seq)
    q, k, v = _qkv(h, mod[1], lat_row, ng0, w_lat, tables, nq, 2 * nkv, nkv, tm=512)
    no_rope = (jnp.ones((ctx_len, LANES), F32), jnp.zeros((ctx_len, LANES), F32), jnp.zeros((ctx_len, LANES), F32))
    kc, vc = _qkv(z, mod[1], ctx_row, ng0, w_ctx, no_rope, 0, 2 * nkv, nkv, tm=ctx_len)
    o = _attention(q, k, v, kc, vc, sink, tq=512)
    tm_tok, tm_moe = 512, 1024
    h1, xn, route, cnt, route_t = _attn_out(h, o, mod[1], ng1, w_out_b, moe_w_router[0].T, tm=tm_tok)
    n_exp = moe_w_router.shape[2]
    n_tok = batch * seq
    n_tok_tiles = n_tok // tm_tok
    max_rows = TOP_K * n_tok + n_tok_tiles * n_exp * (ROW_ALIGN - 1) + n_exp * (tm_moe - 1)
    n_moe_tiles = -(-max_rows // tm_moe)
    tables = _routing_tables(cnt.reshape(n_tok_tiles, n_exp), tm_moe, n_moe_tiles)
    xs = _dispatch(xn.reshape(n_tok, d), route.reshape(n_tok, ROUTE_COLS),
                   route_t.reshape(n_tok_tiles, ROUTE_COLS, tm_tok), tables, n_moe_tiles * tm_moe, tm_tok, tm_moe)
    wg1, wu1, wd1 = (w[0].astype(BF16) for w in (moe_w_gate, moe_w_up, moe_w_down))
    ys = _moe(xs, tables["tile_expert"], tables["n_active"], wg1, wu1, wd1, tm=tm_moe, tf=512)
    return _combine(ys, route, h1, mod[1], final_g[None, :], tables, tm_tok)
```

```python
import functools

import jax
import jax.numpy as jnp
from jax import lax
from jax.experimental import pallas as pl
from jax.experimental.pallas import tpu as pltpu

F32 = jnp.float32
BF16 = jnp.bfloat16

EPS = 1e-6
NEG_INF = -1e30
LANES = 128
HEAD_DIM = 64
Q_PER_KV = 4
WINDOW = 128
GRID_W = 64
ROPE_BASE = 10000.0
CHUNK = 128
TOP_K = 2
VMEM_LIMIT_BYTES = 56 * 1024 * 1024
HIGHEST = lax.Precision.HIGHEST


def _params(*semantics):
    return pltpu.CompilerParams(dimension_semantics=semantics, vmem_limit_bytes=VMEM_LIMIT_BYTES)


def _rms_mod(x, gain, scale, shift):
    ms = jnp.mean(x * x, axis=-1, keepdims=True)
    return (x * lax.rsqrt(ms + EPS) * gain) * (1.0 + scale) + shift


def _full(shape):
    return pl.BlockSpec(shape, lambda *_: (0,) * len(shape))


def _ada_body(c_ref, w_ref, b_ref, o_ref):
    c = c_ref[...]
    a = c * jax.nn.sigmoid(c)
    o_ref[...] = jnp.dot(a, w_ref[...], precision=HIGHEST, preferred_element_type=F32) + b_ref[...]


def _ada_mod(c_rows, ada_w, ada_b, tn=1536):
    depth, d, n = ada_w.shape
    rows = c_rows.shape[0]
    return pl.pallas_call(
        _ada_body,
        grid=(depth, n // tn),
        in_specs=[
            pl.BlockSpec((rows, d), lambda l, j: (0, 0)),
            pl.BlockSpec((None, d, tn), lambda l, j: (l, 0, j)),
            pl.BlockSpec((None, 1, tn), lambda l, j: (l, 0, j)),
        ],
        out_specs=pl.BlockSpec((None, rows, tn), lambda l, j: (l, 0, j)),
        out_shape=jax.ShapeDtypeStruct((depth, rows, n), F32),
        compiler_params=_params("arbitrary", "arbitrary"),
        name="ada_mod",
    )(c_rows, ada_w, ada_b.reshape(depth, 1, n))


def _mixer_a_body(mod_ref, ng_ref, x_ref, win_ref, vg_ref, vb_ref, wpair_ref, bias_ref, wout_ref,
                  o_ref, vlo_ref, vhi_ref, gated_ref):
    tm, d = x_ref.shape
    a_width = wout_ref.shape[0]
    x = x_ref[...]
    hn = _rms_mod(x, ng_ref[...], mod_ref[1:2, :], mod_ref[0:1, :])
    uv = jnp.dot(hn.astype(BF16), win_ref[...], preferred_element_type=F32)
    uv = jax.nn.gelu(uv, approximate=True)
    v = uv[:, a_width:]
    mu = jnp.mean(v, axis=-1, keepdims=True)
    vc = v - mu
    var = jnp.mean(vc * vc, axis=-1, keepdims=True)
    vn = vc * lax.rsqrt(var + EPS) * vg_ref[...] + vb_ref[...]
    lane = lax.broadcasted_iota(jnp.int32, vn.shape, 1)
    low = (lane & (LANES - 1)) < (LANES // 2)
    vlo_ref[...] = jnp.where(low, vn, 0.0).astype(BF16)
    vhi_ref[...] = jnp.where(low, 0.0, vn).astype(BF16)
    for c in range(tm // CHUNK):
        rows = slice(c * CHUNK, (c + 1) * CHUNK)
        cols = []
        for j in range(a_width // LANES):
            lanes = slice(j * LANES, (j + 1) * LANES)
            rhs = jnp.concatenate([vlo_ref[rows, lanes], vhi_ref[rows, lanes]], axis=0)
            cols.append(jnp.dot(wpair_ref[j], rhs, preferred_element_type=F32))
        s = jnp.concatenate(cols, axis=1) + bias_ref[...]
        gated_ref[rows, :] = (uv[rows, :a_width] * s).astype(BF16)
    dh = jnp.dot(gated_ref[...], wout_ref[...], preferred_element_type=F32)
    o_ref[...] = x + mod_ref[2:3, :] * dh


def _mixer_a(x, mod_l, mod_row, ng, w_in, v_g, v_b, wpair, bias, w_out, tm):
    b, s, d = x.shape
    tm = min(tm, s)
    a_width = w_out.shape[0]
    return pl.pallas_call(
        _mixer_a_body,
        grid=(b, s // tm),
        in_specs=[
            pl.BlockSpec((None, 6, d), lambda bi, i: (mod_row(bi), 0, 0)),
            _full((1, d)),
            pl.BlockSpec((None, tm, d), lambda bi, i: (bi, i, 0)),
            _full(w_in.shape), _full((1, a_width)), _full((1, a_width)),
            _full(wpair.shape), _full(bias.shape), _full(w_out.shape),
        ],
        out_specs=pl.BlockSpec((None, tm, d), lambda bi, i: (bi, i, 0)),
        out_shape=jax.ShapeDtypeStruct(x.shape, F32),
        scratch_shapes=[pltpu.VMEM((tm, a_width), BF16)] * 3,
        compiler_params=_params("arbitrary", "arbitrary"),
        name="mixer_a",
    )(mod_l, ng, x, w_in, v_g, v_b, wpair, bias, w_out)


def _swiglu_part(xb, wg, wu, wd):
    g = jnp.dot(xb, wg, preferred_element_type=F32)
    u = jnp.dot(xb, wu, preferred_element_type=F32)
    a = (g * jax.nn.sigmoid(g) * u).astype(BF16)
    return jnp.dot(a, wd, preferred_element_type=F32)


def _dense_ffn_body(mod_ref, ng_ref, x_ref, wg_ref, wu_ref, wd_ref, o_ref, xn_ref, acc_ref):
    j = pl.program_id(2)

    @pl.when(j == 0)
    def _():
        xn_ref[...] = _rms_mod(x_ref[...], ng_ref[...], mod_ref[4:5, :], mod_ref[3:4, :]).astype(BF16)
        acc_ref[...] = jnp.zeros_like(acc_ref)

    acc_ref[...] += _swiglu_part(xn_ref[...], wg_ref[...], wu_ref[...], wd_ref[...])

    @pl.when(j == pl.num_programs(2) - 1)
    def _():
        o_ref[...] = x_ref[...] + mod_ref[5:6, :] * acc_ref[...]


def _dense_ffn(x, mod_l, mod_row, ng, w_gate, w_up, w_down, tm, tf):
    b, s, d = x.shape
    tm = min(tm, s)
    f = w_gate.shape[1]
    return pl.pallas_call(
        _dense_ffn_body,
        grid=(b, s // tm, f // tf),
        in_specs=[
            pl.BlockSpec((None, 6, d), lambda bi, i, j: (mod_row(bi), 0, 0)),
            _full((1, d)),
            pl.BlockSpec((None, tm, d), lambda bi, i, j: (bi, i, 0)),
            pl.BlockSpec((d, tf), lambda bi, i, j: (0, j)),
            pl.BlockSpec((d, tf), lambda bi, i, j: (0, j)),
            pl.BlockSpec((tf, d), lambda bi, i, j: (j, 0)),
        ],
        out_specs=pl.BlockSpec((None, tm, d), lambda bi, i, j: (bi, i, 0)),
        out_shape=jax.ShapeDtypeStruct(x.shape, F32),
        scratch_shapes=[pltpu.VMEM((tm, d), BF16), pltpu.VMEM((tm, d), F32)],
        compiler_params=_params("arbitrary", "arbitrary", "arbitrary"),
        name="dense_ffn",
    )(mod_l, ng, x, w_gate, w_up, w_down)


def _rope(y, cos, sin_up, sin_dn):
    w = y.shape[1]
    reps = w // LANES
    tile = lambda t: jnp.concatenate([t] * reps, axis=1)
    up = pltpu.roll(y, w - HEAD_DIM // 4, axis=1)
    dn = pltpu.roll(y, HEAD_DIM // 4, axis=1)
    return y * tile(cos) + up * tile(sin_up) + dn * tile(sin_dn)


def _qkv_body(mod_ref, ng_ref, x_ref, w_ref, cos_ref, sup_ref, sdn_ref, *out_refs, nq, nk):
    hn = _rms_mod(x_ref[...], ng_ref[...], mod_ref[1:2, :], mod_ref[0:1, :])
    y = jnp.dot(hn.astype(BF16), w_ref[...], preferred_element_type=F32)
    cos, sup, sdn = cos_ref[...], sup_ref[...], sdn_ref[...]
    if nq:
        q_ref, k_ref, v_ref = out_refs
        q_ref[...] = (_rope(y[:, :nq], cos, sup, sdn) * HEAD_DIM ** -0.5).astype(BF16)
    else:
        k_ref, v_ref = out_refs
    k_ref[...] = _rope(y[:, nq:nq + nk], cos, sup, sdn).astype(BF16)
    v_ref[...] = y[:, nq + nk:].astype(BF16)


def _qkv(x, mod_l, mod_row, ng, w, tables, nq, nk, nv, tm):
    b, s, d = x.shape
    tm = min(tm, s)
    widths = ([nq] if nq else []) + [nk, nv]
    tab_spec = pl.BlockSpec((tm, LANES), lambda bi, i: (i, 0))
    return pl.pallas_call(
        functools.partial(_qkv_body, nq=nq, nk=nk),
        grid=(b, s // tm),
        in_specs=[
            pl.BlockSpec((None, 6, d), lambda bi, i: (mod_row(bi), 0, 0)),
            _full((1, d)),
            pl.BlockSpec((None, tm, d), lambda bi, i: (bi, i, 0)),
            _full(w.shape), tab_spec, tab_spec, tab_spec,
        ],
        out_specs=[pl.BlockSpec((None, tm, n), lambda bi, i: (bi, i, 0)) for n in widths],
        out_shape=[jax.ShapeDtypeStruct((b, s, n), BF16) for n in widths],
        compiler_params=_params("arbitrary", "arbitrary"),
        name="qkv_rope" if nq else "kv_ctx",
    )(mod_l, ng, x, w, *tables)


def _attn_body(q_ref, kt_ref, kp_ref, kn_ref, vt_ref, vp_ref, vn_ref, kc_ref, vc_ref, sink_ref,
               o_ref, kwin_ref, vwin_ref, vctx_ref, bias_ref, *, seq):
    tq = q_ref.shape[0]
    nq = tq // WINDOW
    ctx = kc_ref.shape[0]
    kw = 3 * WINDOW
    rows = Q_PER_KV * WINDOW
    n_pairs = vt_ref.shape[1] // LANES
    i = pl.program_id(1)
    first_step = (pl.program_id(0) == 0) & (i == 0)

    @pl.when(first_step)
    def _():
        qi = lax.broadcasted_iota(jnp.int32, (WINDOW, kw), 0)
        kr = lax.broadcasted_iota(jnp.int32, (WINDOW, kw), 1)
        band = jnp.abs(qi - kr + WINDOW) <= WINDOW
        for e in range(4):
            ok = band
            if e & 1:
                ok = ok & (kr >= WINDOW)
            if e & 2:
                ok = ok & (kr < 2 * WINDOW)
            bias_ref[e] = jnp.where(ok, 0.0, NEG_INF)

    kwin_ref[0:WINDOW, :] = kp_ref[...]
    kwin_ref[WINDOW:WINDOW + tq, :] = kt_ref[...]
    kwin_ref[WINDOW + tq:, :] = kn_ref[...]
    for p in range(n_pairs):
        src = slice(p * LANES, (p + 1) * LANES)
        dst = slice(2 * p * LANES, (2 * p + 1) * LANES)
        one = slice((2 * p + 1) * LANES, (2 * p + 2) * LANES)
        vwin_ref[0:WINDOW, dst] = vp_ref[:, src]
        vwin_ref[WINDOW:WINDOW + tq, dst] = vt_ref[:, src]
        vwin_ref[WINDOW + tq:, dst] = vn_ref[:, src]
        vwin_ref[:, one] = jnp.ones((tq + 2 * WINDOW, LANES), BF16)
        vctx_ref[0:ctx, dst] = vc_ref[:, src]
        vctx_ref[ctx:, dst] = jnp.zeros((LANES, LANES), BF16)
        sink_row = lax.broadcasted_iota(jnp.int32, (ctx + LANES, LANES), 0) <= ctx
        vctx_ref[:, one] = jnp.where(sink_row, 1.0, 0.0).astype(BF16)
    nt = (((1,), (1,)), ((), ()))
    low = lax.broadcasted_iota(jnp.int32, (rows, LANES), 1) < HEAD_DIM
    n_blocks = seq // WINDOW

    def block(jj, carry):
        r0 = pl.multiple_of(jj * WINDOW, WINDOW)
        blk = i * nq + jj
        edge = jnp.where(blk == 0, 1, 0) + jnp.where(blk == n_blocks - 1, 2, 0)
        bias = bias_ref[edge]
        bias4 = jnp.concatenate([bias] * Q_PER_KV, axis=0)
        for p in range(n_pairs):
            q4 = jnp.concatenate(
                [q_ref[pl.ds(r0, WINDOW), (p * Q_PER_KV + g) * LANES:(p * Q_PER_KV + g + 1) * LANES]
                 for g in range(Q_PER_KV)], axis=0)
            v_loc = vwin_ref[pl.ds(r0, kw), 2 * p * LANES:(2 * p + 2) * LANES]
            v_ctx = vctx_ref[:, 2 * p * LANES:(2 * p + 2) * LANES]
            halves = []
            for half in range(2):
                h = 2 * p + half
                s_loc = lax.dot_general(q4, kwin_ref[pl.ds(r0, kw), h * LANES:(h + 1) * LANES], nt,
                                        preferred_element_type=F32) + bias4
                s_ctx = lax.dot_general(q4, kc_ref[:, h * LANES:(h + 1) * LANES], nt,
                                        preferred_element_type=F32)
                s = jnp.concatenate([s_loc, s_ctx, sink_ref[h]], axis=1)
                m = jnp.max(s, axis=1, keepdims=True)
                pe = jnp.exp(s - m).astype(BF16)
                o = (jnp.dot(pe[:, :kw], v_loc, preferred_element_type=F32)
                     + jnp.dot(pe[:, kw:], v_ctx, preferred_element_type=F32))
                halves.append(o[:, :LANES] / o[:, LANES:])
            o4 = jnp.where(low, halves[0], halves[1]).astype(o_ref.dtype)
            for g in range(Q_PER_KV):
                c0 = (p * Q_PER_KV + g) * LANES
                o_ref[pl.ds(r0, WINDOW), c0:c0 + LANES] = o4[g * WINDOW:(g + 1) * WINDOW, :]
        return carry

    lax.fori_loop(0, nq, block, 0, unroll=2)


def _attention(q, k, v, kc, vc, sink, tq):
    b, s, dq = q.shape
    nk, nv = k.shape[2], v.shape[2]
    ctx = kc.shape[1]
    nq = tq // WINDOW
    nb = s // WINDOW
    n_kv = sink.shape[0] // Q_PER_KV
    lane0 = jnp.arange(LANES) == 0
    sink_rows = jnp.repeat(sink.reshape(n_kv, Q_PER_KV), WINDOW, axis=1)
    sink_blk = jnp.where(lane0[None, None, :], sink_rows[:, :, None], NEG_INF).astype(F32)
    tile = lambda n: pl.BlockSpec((None, tq, n), lambda bi, i: (bi, i, 0))
    prev = lambda n: pl.BlockSpec((None, WINDOW, n), lambda bi, i: (bi, jnp.maximum(i * nq - 1, 0), 0))
    nxt = lambda n: pl.BlockSpec((None, WINDOW, n), lambda bi, i: (bi, jnp.minimum((i + 1) * nq, nb - 1), 0))
    cblk = lambda n: pl.BlockSpec((None, ctx, n), lambda bi, i: (bi, 0, 0))
    return pl.pallas_call(
        functools.partial(_attn_body, seq=s),
        grid=(b, s // tq),
        in_specs=[
            tile(dq), tile(nk), prev(nk), nxt(nk), tile(nv), prev(nv), nxt(nv), cblk(nk), cblk(nv),
            _full(sink_blk.shape),
        ],
        out_specs=tile(dq),
        out_shape=jax.ShapeDtypeStruct(q.shape, BF16),
        scratch_shapes=[pltpu.VMEM((tq + 2 * WINDOW, nk), BF16),
                        pltpu.VMEM((tq + 2 * WINDOW, 2 * nv), BF16),
                        pltpu.VMEM((ctx + LANES, 2 * nv), BF16),
                        pltpu.VMEM((4, WINDOW, 3 * WINDOW), F32)],
        compiler_params=_params("arbitrary", "arbitrary"),
        name="window_attn",
    )(q, k, k, k, v, v, v, kc, vc, sink_blk)


R_E1, R_E2, R_W1, R_W2, R_RANK1, R_RANK2 = range(6)
ROUTE_COLS = 8
T_E1, T_E2, T_RANK1_HI, T_RANK1_LO, T_RANK2_HI, T_RANK2_LO = range(6)
RANK_RADIX = 16
ROUTE_SUB = 512


def _attn_out_body(mod_ref, ng_ref, h_ref, o_ref, wout_ref, wrt_ref, h1_ref, xn_ref, route_ref, cnt_ref,
                   route_t_ref):
    tm = h_ref.shape[0]
    n_exp = wrt_ref.shape[0]
    sub = min(ROUTE_SUB, tm)
    idx = lax.broadcasted_iota(jnp.int32, (sub, n_exp), 1)
    rcol = lax.broadcasted_iota(jnp.int32, (sub, ROUTE_COLS), 1)
    row = lax.broadcasted_iota(jnp.int32, (sub, sub), 0)
    col = lax.broadcasted_iota(jnp.int32, (sub, sub), 1)
    earlier = jnp.where(col < row, 1.0, 0.0).astype(BF16)
    eye = jnp.where(lax.broadcasted_iota(jnp.int32, (ROUTE_COLS, ROUTE_COLS), 0)
                    == lax.broadcasted_iota(jnp.int32, (ROUTE_COLS, ROUTE_COLS), 1), 1.0, 0.0).astype(BF16)
    nt = (((1,), (1,)), ((), ()))
    seen = jnp.zeros((1, n_exp), F32)
    for sb in range(tm // sub):
        rows = slice(sb * sub, (sb + 1) * sub)
        dh = jnp.dot(o_ref[rows, :], wout_ref[...], preferred_element_type=F32)
        h1 = h_ref[rows, :] + mod_ref[2:3, :] * dh
        h1_ref[rows, :] = h1
        xn = _rms_mod(h1, ng_ref[...], mod_ref[4:5, :], mod_ref[3:4, :])
        xn_ref[rows, :] = xn.astype(BF16)
        logits = jnp.zeros((sub, n_exp), F32)
        for e in range(n_exp):
            logit = jnp.sum(xn * wrt_ref[e:e + 1, :], axis=1, keepdims=True)
            logits = jnp.where(idx == e, logit, logits)
        m1 = jnp.max(logits, axis=1, keepdims=True)
        i1 = jnp.min(jnp.where(logits == m1, idx, n_exp), axis=1, keepdims=True)
        rest = jnp.where(idx == i1, -jnp.inf, logits)
        m2 = jnp.max(rest, axis=1, keepdims=True)
        i2 = jnp.min(jnp.where(rest == m2, idx, n_exp), axis=1, keepdims=True)
        e2 = jnp.exp(m2 - m1)
        w1 = 1.0 / (1.0 + e2)
        w2 = e2 / (1.0 + e2)
        chosen = jnp.where((idx == i1) | (idx == i2), 1.0, 0.0)
        before = jnp.dot(earlier, chosen.astype(BF16), preferred_element_type=F32) + seen
        seen = seen + jnp.sum(chosen, axis=0, keepdims=True)
        rank1 = jnp.sum(jnp.where(idx == i1, before, 0.0), axis=1, keepdims=True)
        rank2 = jnp.sum(jnp.where(idx == i2, before, 0.0), axis=1, keepdims=True)
        rec = jnp.zeros((sub, ROUTE_COLS), F32)
        for c, val in ((R_E1, i1.astype(F32)), (R_E2, i2.astype(F32)), (R_W1, w1), (R_W2, w2),
                       (R_RANK1, rank1), (R_RANK2, rank2)):
            rec = jnp.where(rcol == c, val, rec)
        route_ref[rows, :] = rec
        hi1 = jnp.floor(rank1 * (1.0 / RANK_RADIX))
        hi2 = jnp.floor(rank2 * (1.0 / RANK_RADIX))
        ints = jnp.zeros((sub, ROUTE_COLS), F32)
        for c, val in ((T_E1, i1.astype(F32)), (T_E2, i2.astype(F32)), (T_RANK1_HI, hi1), (T_RANK2_HI, hi2),
                       (T_RANK1_LO, rank1 - RANK_RADIX * hi1), (T_RANK2_LO, rank2 - RANK_RADIX * hi2)):
            ints = jnp.where(rcol == c, val, ints)
        route_t_ref[:, rows] = lax.dot_general(eye, ints.astype(BF16), nt, preferred_element_type=F32)
    cnt_ref[...] = seen


def _attn_out(h, o, mod_l, ng, w_out, w_router_t, tm):
    b, s, d = h.shape
    n_exp = w_router_t.shape[0]
    tok = lambda n: pl.BlockSpec((None, tm, n), lambda bi, i: (bi, i, 0))
    return pl.pallas_call(
        _attn_out_body,
        grid=(b, s // tm),
        in_specs=[
            pl.BlockSpec((None, 6, d), lambda bi, i: (bi, 0, 0)),
            _full((1, d)), tok(d), tok(o.shape[2]), _full(w_out.shape), _full(w_router_t.shape),
        ],
        out_specs=[tok(d), tok(d), tok(ROUTE_COLS),
                   pl.BlockSpec((None, None, 1, n_exp), lambda bi, i: (bi, i, 0, 0)),
                   pl.BlockSpec((None, None, ROUTE_COLS, tm), lambda bi, i: (bi, i, 0, 0))],
        out_shape=[jax.ShapeDtypeStruct(h.shape, F32), jax.ShapeDtypeStruct(h.shape, BF16),
                   jax.ShapeDtypeStruct((b, s, ROUTE_COLS), F32),
                   jax.ShapeDtypeStruct((b, s // tm, 1, n_exp), F32),
                   jax.ShapeDtypeStruct((b, s // tm, ROUTE_COLS, tm), F32)],
        compiler_params=_params("arbitrary", "arbitrary"),
        name="attn_out_router",
    )(mod_l, ng, h, o, w_out, w_router_t)


ROW_ALIGN = 8
GATE_LANES = LANES
G_W1, G_W2, G_E1 = 0, 3, 6


def _chunk_sizes(max_rows):
    sizes = []
    size = ROW_ALIGN
    while size <= max_rows:
        sizes.append(size)
        size *= 2
    return tuple(reversed(sizes))


def _chunked_copies(n, src, src_row, dst, dst_row, sems, sem_index, sizes):
    pairs = []
    for k, size in enumerate(sizes):
        done = n & (-2 * size)
        copy = pltpu.make_async_copy(
            src.at[pl.ds(pl.multiple_of(src_row + done, ROW_ALIGN), size)],
            dst.at[pl.ds(pl.multiple_of(dst_row + done, ROW_ALIGN), size)],
            sems.at[(*sem_index, k)])
        pairs.append(((n & size) != 0, copy))
    return pairs


def _start_all(pairs):
    for cond, copy in pairs:
        pl.when(cond)(copy.start)


def _wait_all(pairs):
    for cond, copy in pairs:
        pl.when(cond)(copy.wait)


def _selection(route, ls_ref, base, n_exp, r_cap):
    tm = route.shape[0]
    e1 = route[:, R_E1:R_E1 + 1].astype(jnp.int32)
    e2 = route[:, R_E2:R_E2 + 1].astype(jnp.int32)
    row1 = route[:, R_RANK1:R_RANK1 + 1].astype(jnp.int32)
    row2 = route[:, R_RANK2:R_RANK2 + 1].astype(jnp.int32)
    for e in range(n_exp):
        start = ls_ref[base + e]
        row1 = row1 + jnp.where(e1 == e, start, 0)
        row2 = row2 + jnp.where(e2 == e, start, 0)
    col = lax.broadcasted_iota(jnp.int32, (tm, r_cap), 1)
    return jnp.where((col == row1) | (col == row2), 1.0, 0.0).astype(BF16)


def _split3(w):
    a = w.astype(BF16).astype(F32)
    r = w - a
    b = r.astype(BF16).astype(F32)
    return a, b, r - b


def _selection_t(route_t, ls_ref, base, n_exp, r_cap):
    tm = route_t.shape[1]
    e1 = route_t[T_E1:T_E1 + 1, :].astype(jnp.int32)
    e2 = route_t[T_E2:T_E2 + 1, :].astype(jnp.int32)
    row1 = (route_t[T_RANK1_HI:T_RANK1_HI + 1, :] * RANK_RADIX + route_t[T_RANK1_LO:T_RANK1_LO + 1, :]).astype(jnp.int32)
    row2 = (route_t[T_RANK2_HI:T_RANK2_HI + 1, :] * RANK_RADIX + route_t[T_RANK2_LO:T_RANK2_LO + 1, :]).astype(jnp.int32)
    for e in range(n_exp):
        start = ls_ref[base + e]
        row1 = row1 + jnp.where(e1 == e, start, 0)
        row2 = row2 + jnp.where(e2 == e, start, 0)
    ridx = lax.broadcasted_iota(jnp.int32, (r_cap, tm), 0)
    return jnp.where((ridx == row1) | (ridx == row2), 1.0, 0.0).astype(BF16)


def _dispatch_body(ls_ref, off_ref, n8_ref, tail_off_ref, tail_n_ref, na_ref, x_ref, route_ref, route_t_ref,
                   xs_ref, local_ref, sems, *, n_exp, sizes, tm_moe, min_active):
    t = pl.program_id(0)
    base = t * n_exp
    tm, d = x_ref.shape
    r_cap = local_ref.shape[1]
    route = route_ref[...]
    sel = _selection_t(route_t_ref[...], ls_ref, base, n_exp, r_cap)
    lane = lax.broadcasted_iota(jnp.int32, (tm, GATE_LANES), 1)
    extra = jnp.zeros((tm, GATE_LANES), F32)
    terms = _split3(route[:, R_W1:R_W1 + 1]) + _split3(route[:, R_W2:R_W2 + 1]) + (route[:, R_E1:R_E1 + 1],)
    for k, term in enumerate(terms):
        extra = jnp.where(lane == k, term, extra)
    xa = jnp.concatenate([x_ref[...], extra.astype(BF16)], axis=1)
    rows = jnp.dot(sel, xa, preferred_element_type=F32)
    ridx = lax.broadcasted_iota(jnp.int32, (r_cap, 1), 0)
    owner = jnp.zeros((r_cap, 1), jnp.int32)
    for e in range(1, n_exp):
        owner = owner + jnp.where(ridx >= ls_ref[base + e], 1, 0)
    g = rows[:, d:]
    gate1 = g[:, G_W1:G_W1 + 1] + g[:, G_W1 + 1:G_W1 + 2] + g[:, G_W1 + 2:G_W1 + 3]
    gate2 = g[:, G_W2:G_W2 + 1] + g[:, G_W2 + 1:G_W2 + 2] + g[:, G_W2 + 2:G_W2 + 3]
    gate = jnp.where(g[:, G_E1:G_E1 + 1].astype(jnp.int32) == owner, gate1, gate2)
    slot = lax.rem(t, 2)

    def run_copies(tile, slot_):
        b0 = tile * n_exp
        pairs = []
        for e in range(n_exp):
            pairs += _chunked_copies(n8_ref[b0 + e], local_ref.at[slot_], ls_ref[b0 + e], xs_ref, off_ref[b0 + e],
                                     sems, (slot_, e), sizes[0])
        return pairs

    @pl.when(t >= 2)
    def _():
        _wait_all(run_copies(jnp.maximum(t - 2, 0), slot))

    local_ref[slot, :, :d] = rows[:, :d]
    local_ref[slot, :, d:] = jnp.broadcast_to(gate, (r_cap, GATE_LANES))
    _start_all(run_copies(t, slot))

    @pl.when(t == pl.num_programs(0) - 1)
    def _():
        @pl.when(t >= 1)
        def _():
            _wait_all(run_copies(jnp.maximum(t - 1, 0), 1 - slot))

        _wait_all(run_copies(t, slot))
        zeros_ref = local_ref.at[0]
        zeros_ref[...] = jnp.zeros_like(zeros_ref)
        tails = []
        for e in range(n_exp):
            tails += _chunked_copies(tail_n_ref[e], zeros_ref, 0, xs_ref, tail_off_ref[e], sems, (0, e), sizes[1])
        _start_all(tails)
        _wait_all(tails)
        n_tiles = xs_ref.shape[0] // tm_moe
        spare = []
        for k in range(n_tiles - min_active):
            tile = na_ref[0] + k
            copy = pltpu.make_async_copy(
                zeros_ref.at[pl.ds(0, tm_moe)],
                xs_ref.at[pl.ds(pl.multiple_of(jnp.minimum(tile, n_tiles - 1) * tm_moe, tm_moe), tm_moe)],
                sems.at[0, k % n_exp, k // n_exp])
            spare.append((tile < n_tiles, copy))
        _start_all(spare)
        _wait_all(spare)


def _dispatch(xn, route, route_t, tables, n_rows, tm, tm_moe):
    n_tok, d = xn.shape
    n_exp = tables["tail_n"].shape[0]
    r_cap = 2 * tm + LANES
    sizes = _chunk_sizes(tm), _chunk_sizes(tm_moe - ROW_ALIGN)
    assert tm_moe <= r_cap
    min_active = TOP_K * n_tok // tm_moe
    n_sems = max(len(sizes[0]), len(sizes[1]))
    assert n_rows // tm_moe - min_active <= n_exp * n_sems
    grid_spec = pltpu.PrefetchScalarGridSpec(
        num_scalar_prefetch=6,
        grid=(n_tok // tm,),
        in_specs=[pl.BlockSpec((tm, d), lambda t, *_: (t, 0)),
                  pl.BlockSpec((tm, ROUTE_COLS), lambda t, *_: (t, 0)),
                  pl.BlockSpec((None, ROUTE_COLS, tm), lambda t, *_: (t, 0, 0))],
        out_specs=pl.BlockSpec(memory_space=pl.ANY),
        scratch_shapes=[pltpu.VMEM((2, r_cap, d + GATE_LANES), F32),
                        pltpu.SemaphoreType.DMA((2, n_exp, n_sems))],
    )
    return pl.pallas_call(
        functools.partial(_dispatch_body, n_exp=n_exp, sizes=sizes, tm_moe=tm_moe, min_active=min_active),
        grid_spec=grid_spec,
        out_shape=jax.ShapeDtypeStruct((n_rows, d + GATE_LANES), F32),
        compiler_params=_params("arbitrary"),
        name="moe_dispatch",
    )(tables["ls"], tables["off"], tables["n8"], tables["tail_off"], tables["tail_n"], tables["n_active"],
      xn, route, route_t)


def _moe_body(te_ref, na_ref, x_ref, wg_ref, wu_ref, wd_ref, o_ref, xb_ref, acc_ref):
    i = pl.program_id(0)
    j = pl.program_id(1)
    d = o_ref.shape[1]

    @pl.when(i < na_ref[0])
    def _():
        @pl.when(j == 0)
        def _():
            xb_ref[...] = x_ref[:, :d].astype(BF16)
            acc_ref[...] = jnp.zeros_like(acc_ref)

        acc_ref[...] += _swiglu_part(xb_ref[...], wg_ref[...], wu_ref[...], wd_ref[...])

        @pl.when(j == pl.num_programs(1) - 1)
        def _():
            o_ref[...] = acc_ref[...] * x_ref[:, d:d + 1]

    @pl.when((i >= na_ref[0]) & (j == 0))
    def _():
        o_ref[...] = jnp.zeros_like(o_ref)


def _moe(xs, tile_expert, n_active, w_gate, w_up, w_down, tm, tf):
    n_rows, width = xs.shape
    n_exp, d, f = w_gate.shape
    nj = f // tf
    row = lambda i, j, te, na: (i, 0)
    jj = lambda i, j, na: jnp.where(i < na[0], j, nj - 1)
    grid_spec = pltpu.PrefetchScalarGridSpec(
        num_scalar_prefetch=2,
        grid=(n_rows // tm, nj),
        in_specs=[
            pl.BlockSpec((tm, width), row),
            pl.BlockSpec((None, d, tf), lambda i, j, te, na: (te[i], 0, jj(i, j, na))),
            pl.BlockSpec((None, d, tf), lambda i, j, te, na: (te[i], 0, jj(i, j, na))),
            pl.BlockSpec((None, tf, d), lambda i, j, te, na: (te[i], jj(i, j, na), 0)),
        ],
        out_specs=pl.BlockSpec((tm, d), row),
        scratch_shapes=[pltpu.VMEM((tm, d), BF16), pltpu.VMEM((tm, d), F32)],
    )
    return pl.pallas_call(
        _moe_body,
        grid_spec=grid_spec,
        out_shape=jax.ShapeDtypeStruct((n_rows, d), F32),
        compiler_params=_params("arbitrary", "arbitrary"),
        name="moe_ffn",
    )(tile_expert, n_active, xs, w_gate, w_up, w_down)


def _combine_body(ls_ref, off_ref, n8_ref, route_ref, h_ref, mod_ref, fg_ref, ys_ref, o_ref,
                  local_ref, sems, *, n_exp, sizes):
    n_steps = pl.num_programs(0) * pl.num_programs(1)
    t = pl.program_id(0) * pl.num_programs(1) + pl.program_id(1)
    base = t * n_exp
    r_cap = local_ref.shape[1]
    slot = lax.rem(t, 2)

    def run_copies(tile, slot_):
        b0 = tile * n_exp
        pairs = []
        for e in range(n_exp):
            pairs += _chunked_copies(n8_ref[b0 + e], ys_ref, off_ref[b0 + e], local_ref.at[slot_], ls_ref[b0 + e],
                                     sems, (slot_, e), sizes)
        return pairs

    def fetch(tile, slot_):
        local_ref[slot_] = jnp.zeros(local_ref.shape[1:], local_ref.dtype)
        _start_all(run_copies(tile, slot_))

    @pl.when(t == 0)
    def _():
        fetch(t, slot)

    @pl.when(t + 1 < n_steps)
    def _():
        fetch(jnp.minimum(t + 1, n_steps - 1), 1 - slot)

    sel = _selection(route_ref[...], ls_ref, base, n_exp, r_cap)
    _wait_all(run_copies(t, slot))
    y = jnp.dot(sel, local_ref[slot].astype(BF16), preferred_element_type=F32)
    h2 = h_ref[...] + mod_ref[5:6, :] * y
    ms = jnp.mean(h2 * h2, axis=-1, keepdims=True)
    o_ref[...] = h2 * lax.rsqrt(ms + EPS) * fg_ref[...]


def _combine(ys, route, h1, mod_l, final_g, tables, tm):
    b, s, d = h1.shape
    n_exp = tables["tail_n"].shape[0]
    r_cap = 2 * tm + LANES
    sizes = _chunk_sizes(tm)
    tok = lambda n: pl.BlockSpec((None, tm, n), lambda bi, i, *_: (bi, i, 0))
    grid_spec = pltpu.PrefetchScalarGridSpec(
        num_scalar_prefetch=3,
        grid=(b, s // tm),
        in_specs=[tok(ROUTE_COLS), tok(d),
                  pl.BlockSpec((None, 6, d), lambda bi, i, *_: (bi, 0, 0)),
                  pl.BlockSpec((1, d), lambda bi, i, *_: (0, 0)),
                  pl.BlockSpec(memory_space=pl.ANY)],
        out_specs=tok(d),
        scratch_shapes=[pltpu.VMEM((2, r_cap, d), F32), pltpu.SemaphoreType.DMA((2, n_exp, len(sizes)))],
    )
    return pl.pallas_call(
        functools.partial(_combine_body, n_exp=n_exp, sizes=sizes),
        grid_spec=grid_spec,
        out_shape=jax.ShapeDtypeStruct(h1.shape, F32),
        compiler_params=_params("arbitrary", "arbitrary"),
        name="moe_combine",
    )(tables["ls"], tables["off"], tables["n8"], route, h1, mod_l, final_g, ys)


def _routing_tables(cnt, tm_moe, n_tiles_moe):
    n = cnt.astype(jnp.int32)
    n8 = (n + ROW_ALIGN - 1) // ROW_ALIGN * ROW_ALIGN
    ls = jnp.cumsum(n8, axis=1) - n8
    rows = jnp.sum(n8, axis=0)
    rows_pad = (rows + tm_moe - 1) // tm_moe * tm_moe
    ends = jnp.cumsum(rows_pad)
    ebase = ends - rows_pad
    off = ebase[None, :] + jnp.cumsum(n8, axis=0) - n8
    n_active = ends[-1] // tm_moe
    tile_start = jnp.arange(n_tiles_moe, dtype=jnp.int32) * tm_moe
    te = jnp.sum((tile_start[:, None] >= ends[None, :]).astype(jnp.int32), axis=1)
    te = jnp.where(tile_start < ends[-1], te, te[n_active - 1])
    flat = lambda a: a.reshape(-1).astype(jnp.int32)
    return dict(ls=flat(ls), off=flat(off), n8=flat(n8), tail_off=flat(ebase + rows),
                tail_n=flat(rows_pad - rows), tile_expert=te, n_active=flat(n_active))


def _rope_tables(seq):
    pos = jnp.arange(seq, dtype=jnp.int32)
    lane = jnp.arange(LANES, dtype=jnp.int32) % HEAD_DIM
    quarter = HEAD_DIM // 4
    inv = ROPE_BASE ** (-(lane % quarter).astype(F32) / quarter)
    coord = jnp.where((lane < HEAD_DIM // 2)[None, :], (pos // GRID_W)[:, None], (pos % GRID_W)[:, None])
    ang = coord.astype(F32) * inv[None, :]
    second = ((lane % (HEAD_DIM // 2)) >= quarter)[None, :]
    sin = jnp.sin(ang)
    return jnp.cos(ang), jnp.where(second, 0.0, -sin), jnp.where(second, sin, 0.0)


def _head_order(n_q_heads):
    order = []
    for p in range(n_q_heads // (2 * Q_PER_KV)):
        for g in range(Q_PER_KV):
            order += [2 * p * Q_PER_KV + g, (2 * p + 1) * Q_PER_KV + g]
    return order


def kernel(x, c, ctx, c_ctx, ada_w, ada_b, norm_g, final_g, a_w_in, a_v_g, a_v_b, a_w_s, a_b_s, a_w_out,
           b_w_qkv, b_sink, b_w_out, ffn_w_gate, ffn_w_up, ffn_w_down,
           moe_w_router, moe_w_gate, moe_w_up, moe_w_down):
    batch, seq, d = x.shape
    ctx_len = ctx.shape[1]
    depth = ada_w.shape[0]
    assert depth == 2 and seq % 1024 == 0 and ctx_len % CHUNK == 0

    mod_rows = 8
    c_rows = jnp.concatenate([c, c_ctx[None, :], jnp.zeros((mod_rows - batch - 1, d), F32)], axis=0)
    mod = _ada_mod(c_rows, ada_w, ada_b).reshape(depth, mod_rows, 6, d)
    lat_row = lambda bi: bi
    ctx_row = lambda bi: batch

    a_width = a_w_out.shape[1]
    groups = a_w_s.shape[1]
    w_in = a_w_in[0].astype(BF16)
    w_s = a_w_s[0]
    wpair = jnp.concatenate([w_s[0::2], w_s[1::2]], axis=2).astype(BF16)
    bias = jnp.repeat(a_b_s[0].T, a_width // groups, axis=1)
    w_out_a = a_w_out[0].astype(BF16)
    ng0 = norm_g[0, 0][None, :]
    ng1 = norm_g[0, 1][None, :]
    wg0, wu0, wd0 = (w[0].astype(BF16) for w in (ffn_w_gate, ffn_w_up, ffn_w_down))
    vg, vb = a_v_g[0][None, :], a_v_b[0][None, :]

    def layer0(t, row):
        t = _mixer_a(t, mod[0], row, ng0, w_in, vg, vb, wpair, bias, w_out_a, tm=512)
        return _dense_ffn(t, mod[0], row, ng1, wg0, wu0, wd0, tm=512, tf=1792)

    h = layer0(x, lat_row)
    z = layer0(ctx, ctx_row)

    w_qkv = b_w_qkv[0]
    n_heads = b_sink.shape[1]
    nq = n_heads * HEAD_DIM
    n_kv = n_heads // Q_PER_KV
    nkv = n_kv * HEAD_DIM
    order = jnp.array(_head_order(n_heads), dtype=jnp.int32)
    wq = w_qkv[:, :nq].reshape(d, n_heads, HEAD_DIM)[:, order].reshape(d, nq)
    wk = w_qkv[:, nq:nq + nkv].reshape(d, n_kv, HEAD_DIM)
    zero = jnp.zeros_like(wk)
    even = (jnp.arange(n_kv) % 2 == 0)[None, :, None]
    wk_ext = jnp.concatenate([jnp.where(even, wk, zero), jnp.where(even, zero, wk)], axis=2).reshape(d, 2 * nkv)
    wv = w_qkv[:, nq + nkv:]
    w_lat = jnp.concatenate([wq, wk_ext, wv], axis=1).astype(BF16)
    w_ctx = jnp.concatenate([wk_ext, wv], axis=1).astype(BF16)
    w_out_b = b_w_out[0].reshape(n_heads, HEAD_DIM, d)[order].reshape(nq, d).astype(BF16)
    sink = b_sink[0]
    ng0 = norm_g[1, 0][None, :]
    ng1 = norm_g[1, 1][None, :]

    tables = _rope_tables(seq)
    q, k, v = _qkv(h, mod[1], lat_row, ng0, w_lat, tables, nq, 2 * nkv, nkv, tm=512)
    no_rope = (jnp.ones((ctx_len, LANES), F32), jnp.zeros((ctx_len, LANES), F32), jnp.zeros((ctx_len, LANES), F32))
    kc, vc = _qkv(z, mod[1], ctx_row, ng0, w_ctx, no_rope, 0, 2 * nkv, nkv, tm=ctx_len)
    o = _attention(q, k, v, kc, vc, sink, tq=512)
    tm_tok, tm_moe = 512, 1024
    h1, xn, route, cnt, route_t = _attn_out(h, o, mod[1], ng1, w_out_b, moe_w_router[0].T, tm=tm_tok)
    n_exp = moe_w_router.shape[2]
    n_tok = batch * seq
    n_tok_tiles = n_tok // tm_tok
    max_rows = TOP_K * n_tok + n_tok_tiles * n_exp * (ROW_ALIGN - 1) + n_exp * (tm_moe - 1)
    n_moe_tiles = -(-max_rows // tm_moe)
    tables = _routing_tables(cnt.reshape(n_tok_tiles, n_exp), tm_moe, n_moe_tiles)
    xs = _dispatch(xn.reshape(n_tok, d), route.reshape(n_tok, ROUTE_COLS),
                   route_t.reshape(n_tok_tiles, ROUTE_COLS, tm_tok), tables, n_moe_tiles * tm_moe, tm_tok, tm_moe)
    wg1, wu1, wd1 = (w[0].astype(BF16) for w in (moe_w_gate, moe_w_up, moe_w_down))
    ys = _moe(xs, tables["tile_expert"], tables["n_active"], wg1, wu1, wd1, tm=tm_moe, tf=512)
    return _combine(ys, route, h1, mod[1], final_g[None, :], tables, tm_tok)
```

```python
import functools

import jax
import jax.numpy as jnp
from jax import lax
from jax.experimental import pallas as pl
from jax.experimental.pallas import tpu as pltpu

F32 = jnp.float32
BF16 = jnp.bfloat16

EPS = 1e-6
NEG_INF = -1e30
LANES = 128
HEAD_DIM = 64
Q_PER_KV = 4
WINDOW = 128
GRID_W = 64
ROPE_BASE = 10000.0
CHUNK = 128
TOP_K = 2
VMEM_LIMIT_BYTES = 56 * 1024 * 1024
HIGHEST = lax.Precision.HIGHEST


def _params(*semantics):
    return pltpu.CompilerParams(dimension_semantics=semantics, vmem_limit_bytes=VMEM_LIMIT_BYTES)


def _rms_mod(x, gain, scale, shift):
    ms = jnp.mean(x * x, axis=-1, keepdims=True)
    return (x * lax.rsqrt(ms + EPS) * gain) * (1.0 + scale) + shift


def _full(shape):
    return pl.BlockSpec(shape, lambda *_: (0,) * len(shape))


def _ada_body(c_ref, w_ref, b_ref, o_ref):
    c = c_ref[...]
    a = c * jax.nn.sigmoid(c)
    o_ref[...] = jnp.dot(a, w_ref[...], precision=HIGHEST, preferred_element_type=F32) + b_ref[...]


def _ada_mod(c_rows, ada_w, ada_b, tn=1536):
    depth, d, n = ada_w.shape
    rows = c_rows.shape[0]
    return pl.pallas_call(
        _ada_body,
        grid=(depth, n // tn),
        in_specs=[
            pl.BlockSpec((rows, d), lambda l, j: (0, 0)),
            pl.BlockSpec((None, d, tn), lambda l, j: (l, 0, j)),
            pl.BlockSpec((None, 1, tn), lambda l, j: (l, 0, j)),
        ],
        out_specs=pl.BlockSpec((None, rows, tn), lambda l, j: (l, 0, j)),
        out_shape=jax.ShapeDtypeStruct((depth, rows, n), F32),
        compiler_params=_params("arbitrary", "arbitrary"),
        name="ada_mod",
    )(c_rows, ada_w, ada_b.reshape(depth, 1, n))


def _mixer_a_body(mod_ref, ng_ref, x_ref, win_ref, vg_ref, vb_ref, wpair_ref, bias_ref, wout_ref,
                  o_ref, vlo_ref, vhi_ref, gated_ref):
    tm, d = x_ref.shape
    a_width = wout_ref.shape[0]
    x = x_ref[...]
    hn = _rms_mod(x, ng_ref[...], mod_ref[1:2, :], mod_ref[0:1, :])
    uv = jnp.dot(hn.astype(BF16), win_ref[...], preferred_element_type=F32)
    uv = jax.nn.gelu(uv, approximate=True)
    v = uv[:, a_width:]
    mu = jnp.mean(v, axis=-1, keepdims=True)
    vc = v - mu
    var = jnp.mean(vc * vc, axis=-1, keepdims=True)
    vn = vc * lax.rsqrt(var + EPS) * vg_ref[...] + vb_ref[...]
    lane = lax.broadcasted_iota(jnp.int32, vn.shape, 1)
    low = (lane & (LANES - 1)) < (LANES // 2)
    vlo_ref[...] = jnp.where(low, vn, 0.0).astype(BF16)
    vhi_ref[...] = jnp.where(low, 0.0, vn).astype(BF16)
    n_chunks = tm // CHUNK
    mixed = []
    for j in range(a_width // LANES):
        lanes = slice(j * LANES, (j + 1) * LANES)
        rhs = jnp.concatenate(
            [jnp.concatenate([vlo_ref[c * CHUNK:(c + 1) * CHUNK, lanes], vhi_ref[c * CHUNK:(c + 1) * CHUNK, lanes]],
                             axis=0) for c in range(n_chunks)], axis=1)
        mixed.append(jnp.dot(wpair_ref[j], rhs, preferred_element_type=F32))
    for c in range(n_chunks):
        rows = slice(c * CHUNK, (c + 1) * CHUNK)
        s = jnp.concatenate([m[:, c * LANES:(c + 1) * LANES] for m in mixed], axis=1) + bias_ref[...]
        gated_ref[rows, :] = (uv[rows, :a_width] * s).astype(BF16)
    dh = jnp.dot(gated_ref[...], wout_ref[...], preferred_element_type=F32)
    o_ref[...] = x + mod_ref[2:3, :] * dh


def _mixer_a(x, mod_l, mod_row, ng, w_in, v_g, v_b, wpair, bias, w_out, tm):
    b, s, d = x.shape
    tm = min(tm, s)
    a_width = w_out.shape[0]
    return pl.pallas_call(
        _mixer_a_body,
        grid=(b, s // tm),
        in_specs=[
            pl.BlockSpec((None, 6, d), lambda bi, i: (mod_row(bi), 0, 0)),
            _full((1, d)),
            pl.BlockSpec((None, tm, d), lambda bi, i: (bi, i, 0)),
            _full(w_in.shape), _full((1, a_width)), _full((1, a_width)),
            _full(wpair.shape), _full(bias.shape), _full(w_out.shape),
        ],
        out_specs=pl.BlockSpec((None, tm, d), lambda bi, i: (bi, i, 0)),
        out_shape=jax.ShapeDtypeStruct(x.shape, F32),
        scratch_shapes=[pltpu.VMEM((tm, a_width), BF16)] * 3,
        compiler_params=_params("arbitrary", "arbitrary"),
        name="mixer_a",
    )(mod_l, ng, x, w_in, v_g, v_b, wpair, bias, w_out)


def _swiglu_part(xb, wg, wu, wd):
    g = jnp.dot(xb, wg, preferred_element_type=F32)
    u = jnp.dot(xb, wu, preferred_element_type=F32)
    a = (g * jax.nn.sigmoid(g) * u).astype(BF16)
    return jnp.dot(a, wd, preferred_element_type=F32)


def _dense_ffn_body(mod_ref, ng_ref, x_ref, wg_ref, wu_ref, wd_ref, o_ref, xn_ref, acc_ref):
    j = pl.program_id(2)

    @pl.when(j == 0)
    def _():
        xn_ref[...] = _rms_mod(x_ref[...], ng_ref[...], mod_ref[4:5, :], mod_ref[3:4, :]).astype(BF16)
        acc_ref[...] = jnp.zeros_like(acc_ref)

    acc_ref[...] += _swiglu_part(xn_ref[...], wg_ref[...], wu_ref[...], wd_ref[...])

    @pl.when(j == pl.num_programs(2) - 1)
    def _():
        o_ref[...] = x_ref[...] + mod_ref[5:6, :] * acc_ref[...]


def _dense_ffn(x, mod_l, mod_row, ng, w_gate, w_up, w_down, tm, tf):
    b, s, d = x.shape
    tm = min(tm, s)
    f = w_gate.shape[1]
    return pl.pallas_call(
        _dense_ffn_body,
        grid=(b, s // tm, f // tf),
        in_specs=[
            pl.BlockSpec((None, 6, d), lambda bi, i, j: (mod_row(bi), 0, 0)),
            _full((1, d)),
            pl.BlockSpec((None, tm, d), lambda bi, i, j: (bi, i, 0)),
            pl.BlockSpec((d, tf), lambda bi, i, j: (0, j)),
            pl.BlockSpec((d, tf), lambda bi, i, j: (0, j)),
            pl.BlockSpec((tf, d), lambda bi, i, j: (j, 0)),
        ],
        out_specs=pl.BlockSpec((None, tm, d), lambda bi, i, j: (bi, i, 0)),
        out_shape=jax.ShapeDtypeStruct(x.shape, F32),
        scratch_shapes=[pltpu.VMEM((tm, d), BF16), pltpu.VMEM((tm, d), F32)],
        compiler_params=_params("arbitrary", "arbitrary", "arbitrary"),
        name="dense_ffn",
    )(mod_l, ng, x, w_gate, w_up, w_down)


def _rope(y, cos, sin_up, sin_dn):
    w = y.shape[1]
    reps = w // LANES
    tile = lambda t: jnp.concatenate([t] * reps, axis=1)
    up = pltpu.roll(y, w - HEAD_DIM // 4, axis=1)
    dn = pltpu.roll(y, HEAD_DIM // 4, axis=1)
    return y * tile(cos) + up * tile(sin_up) + dn * tile(sin_dn)


def _qkv_body(mod_ref, ng_ref, x_ref, w_ref, cos_ref, sup_ref, sdn_ref, *out_refs, nq, nk):
    hn = _rms_mod(x_ref[...], ng_ref[...], mod_ref[1:2, :], mod_ref[0:1, :])
    y = jnp.dot(hn.astype(BF16), w_ref[...], preferred_element_type=F32)
    cos, sup, sdn = cos_ref[...], sup_ref[...], sdn_ref[...]
    if nq:
        q_ref, k_ref, v_ref = out_refs
        q_ref[...] = (_rope(y[:, :nq], cos, sup, sdn) * HEAD_DIM ** -0.5).astype(BF16)
    else:
        k_ref, v_ref = out_refs
    k_ref[...] = _rope(y[:, nq:nq + nk], cos, sup, sdn).astype(BF16)
    v_ref[...] = y[:, nq + nk:].astype(BF16)


def _qkv(x, mod_l, mod_row, ng, w, tables, nq, nk, nv, tm):
    b, s, d = x.shape
    tm = min(tm, s)
    widths = ([nq] if nq else []) + [nk, nv]
    tab_spec = pl.BlockSpec((tm, LANES), lambda bi, i: (i, 0))
    return pl.pallas_call(
        functools.partial(_qkv_body, nq=nq, nk=nk),
        grid=(b, s // tm),
        in_specs=[
            pl.BlockSpec((None, 6, d), lambda bi, i: (mod_row(bi), 0, 0)),
            _full((1, d)),
            pl.BlockSpec((None, tm, d), lambda bi, i: (bi, i, 0)),
            _full(w.shape), tab_spec, tab_spec, tab_spec,
        ],
        out_specs=[pl.BlockSpec((None, tm, n), lambda bi, i: (bi, i, 0)) for n in widths],
        out_shape=[jax.ShapeDtypeStruct((b, s, n), BF16) for n in widths],
        compiler_params=_params("arbitrary", "arbitrary"),
        name="qkv_rope" if nq else "kv_ctx",
    )(mod_l, ng, x, w, *tables)


def _attn_body(q_ref, kt_ref, kp_ref, kn_ref, vt_ref, vp_ref, vn_ref, kc_ref, vc_ref, sink_ref,
               o_ref, kwin_ref, vwin_ref, vctx_ref, bias_ref, *, seq):
    tq = q_ref.shape[0]
    nq = tq // WINDOW
    ctx = kc_ref.shape[0]
    kw = 3 * WINDOW
    rows = Q_PER_KV * WINDOW
    n_pairs = vt_ref.shape[1] // LANES
    i = pl.program_id(1)
    first_step = (pl.program_id(0) == 0) & (i == 0)

    @pl.when(first_step)
    def _():
        qi = lax.broadcasted_iota(jnp.int32, (WINDOW, kw), 0)
        kr = lax.broadcasted_iota(jnp.int32, (WINDOW, kw), 1)
        band = jnp.abs(qi - kr + WINDOW) <= WINDOW
        for e in range(4):
            ok = band
            if e & 1:
                ok = ok & (kr >= WINDOW)
            if e & 2:
                ok = ok & (kr < 2 * WINDOW)
            bias_ref[e] = jnp.where(ok, 0.0, NEG_INF)

    kwin_ref[0:WINDOW, :] = kp_ref[...]
    kwin_ref[WINDOW:WINDOW + tq, :] = kt_ref[...]
    kwin_ref[WINDOW + tq:, :] = kn_ref[...]
    for p in range(n_pairs):
        src = slice(p * LANES, (p + 1) * LANES)
        dst = slice(2 * p * LANES, (2 * p + 1) * LANES)
        one = slice((2 * p + 1) * LANES, (2 * p + 2) * LANES)
        vwin_ref[0:WINDOW, dst] = vp_ref[:, src]
        vwin_ref[WINDOW:WINDOW + tq, dst] = vt_ref[:, src]
        vwin_ref[WINDOW + tq:, dst] = vn_ref[:, src]
        vwin_ref[:, one] = jnp.ones((tq + 2 * WINDOW, LANES), BF16)
        vctx_ref[0:ctx, dst] = vc_ref[:, src]
        vctx_ref[ctx:, dst] = jnp.zeros((LANES, LANES), BF16)
        sink_row = lax.broadcasted_iota(jnp.int32, (ctx + LANES, LANES), 0) <= ctx
        vctx_ref[:, one] = jnp.where(sink_row, 1.0, 0.0).astype(BF16)
    nt = (((1,), (1,)), ((), ()))
    low = lax.broadcasted_iota(jnp.int32, (rows, LANES), 1) < HEAD_DIM
    n_blocks = seq // WINDOW

    def block(jj, carry):
        r0 = pl.multiple_of(jj * WINDOW, WINDOW)
        blk = i * nq + jj
        edge = jnp.where(blk == 0, 1, 0) + jnp.where(blk == n_blocks - 1, 2, 0)
        bias = bias_ref[edge]
        bias4 = jnp.concatenate([bias] * Q_PER_KV, axis=0)
        for p in range(n_pairs):
            q4 = jnp.concatenate(
                [q_ref[pl.ds(r0, WINDOW), (p * Q_PER_KV + g) * LANES:(p * Q_PER_KV + g + 1) * LANES]
                 for g in range(Q_PER_KV)], axis=0)
            v_loc = vwin_ref[pl.ds(r0, kw), 2 * p * LANES:(2 * p + 2) * LANES]
            v_ctx = vctx_ref[:, 2 * p * LANES:(2 * p + 2) * LANES]
            halves = []
            for half in range(2):
                h = 2 * p + half
                s_loc = lax.dot_general(q4, kwin_ref[pl.ds(r0, kw), h * LANES:(h + 1) * LANES], nt,
                                        preferred_element_type=F32) + bias4
                s_ctx = lax.dot_general(q4, kc_ref[:, h * LANES:(h + 1) * LANES], nt,
                                        preferred_element_type=F32)
                s = jnp.concatenate([s_loc, s_ctx, sink_ref[h]], axis=1)
                m = jnp.max(s, axis=1, keepdims=True)
                pe = jnp.exp(s - m).astype(BF16)
                o = (jnp.dot(pe[:, :kw], v_loc, preferred_element_type=F32)
                     + jnp.dot(pe[:, kw:], v_ctx, preferred_element_type=F32))
                halves.append(o[:, :LANES] / o[:, LANES:])
            o4 = jnp.where(low, halves[0], halves[1]).astype(o_ref.dtype)
            for g in range(Q_PER_KV):
                c0 = (p * Q_PER_KV + g) * LANES
                o_ref[pl.ds(r0, WINDOW), c0:c0 + LANES] = o4[g * WINDOW:(g + 1) * WINDOW, :]
        return carry

    lax.fori_loop(0, nq, block, 0, unroll=2)


def _attention(q, k, v, kc, vc, sink, tq):
    b, s, dq = q.shape
    nk, nv = k.shape[2], v.shape[2]
    ctx = kc.shape[1]
    nq = tq // WINDOW
    nb = s // WINDOW
    n_kv = sink.shape[0] // Q_PER_KV
    lane0 = jnp.arange(LANES) == 0
    sink_rows = jnp.repeat(sink.reshape(n_kv, Q_PER_KV), WINDOW, axis=1)
    sink_blk = jnp.where(lane0[None, None, :], sink_rows[:, :, None], NEG_INF).astype(F32)
    tile = lambda n: pl.BlockSpec((None, tq, n), lambda bi, i: (bi, i, 0))
    prev = lambda n: pl.BlockSpec((None, WINDOW, n), lambda bi, i: (bi, jnp.maximum(i * nq - 1, 0), 0))
    nxt = lambda n: pl.BlockSpec((None, WINDOW, n), lambda bi, i: (bi, jnp.minimum((i + 1) * nq, nb - 1), 0))
    cblk = lambda n: pl.BlockSpec((None, ctx, n), lambda bi, i: (bi, 0, 0))
    return pl.pallas_call(
        functools.partial(_attn_body, seq=s),
        grid=(b, s // tq),
        in_specs=[
            tile(dq), tile(nk), prev(nk), nxt(nk), tile(nv), prev(nv), nxt(nv), cblk(nk), cblk(nv),
            _full(sink_blk.shape),
        ],
        out_specs=tile(dq),
        out_shape=jax.ShapeDtypeStruct(q.shape, BF16),
        scratch_shapes=[pltpu.VMEM((tq + 2 * WINDOW, nk), BF16),
                        pltpu.VMEM((tq + 2 * WINDOW, 2 * nv), BF16),
                        pltpu.VMEM((ctx + LANES, 2 * nv), BF16),
                        pltpu.VMEM((4, WINDOW, 3 * WINDOW), F32)],
        compiler_params=_params("arbitrary", "arbitrary"),
        name="window_attn",
    )(q, k, k, k, v, v, v, kc, vc, sink_blk)


R_E1, R_E2, R_W1, R_W2, R_RANK1, R_RANK2 = range(6)
ROUTE_COLS = 8
T_E1, T_E2, T_RANK1_HI, T_RANK1_LO, T_RANK2_HI, T_RANK2_LO = range(6)
RANK_RADIX = 16
ROUTE_SUB = 512


def _attn_out_body(mod_ref, ng_ref, h_ref, o_ref, wout_ref, wrt_ref, h1_ref, xn_ref, route_ref, cnt_ref,
                   route_t_ref):
    tm = h_ref.shape[0]
    n_exp = wrt_ref.shape[0]
    sub = min(ROUTE_SUB, tm)
    idx = lax.broadcasted_iota(jnp.int32, (sub, n_exp), 1)
    rcol = lax.broadcasted_iota(jnp.int32, (sub, ROUTE_COLS), 1)
    row = lax.broadcasted_iota(jnp.int32, (sub, sub), 0)
    col = lax.broadcasted_iota(jnp.int32, (sub, sub), 1)
    earlier = jnp.where(col < row, 1.0, 0.0).astype(BF16)
    eye = jnp.where(lax.broadcasted_iota(jnp.int32, (ROUTE_COLS, ROUTE_COLS), 0)
                    == lax.broadcasted_iota(jnp.int32, (ROUTE_COLS, ROUTE_COLS), 1), 1.0, 0.0).astype(BF16)
    nt = (((1,), (1,)), ((), ()))
    seen = jnp.zeros((1, n_exp), F32)
    for sb in range(tm // sub):
        rows = slice(sb * sub, (sb + 1) * sub)
        dh = jnp.dot(o_ref[rows, :], wout_ref[...], preferred_element_type=F32)
        h1 = h_ref[rows, :] + mod_ref[2:3, :] * dh
        h1_ref[rows, :] = h1
        xn = _rms_mod(h1, ng_ref[...], mod_ref[4:5, :], mod_ref[3:4, :])
        xn_ref[rows, :] = xn.astype(BF16)
        logits = jnp.zeros((sub, n_exp), F32)
        for e in range(n_exp):
            logit = jnp.sum(xn * wrt_ref[e:e + 1, :], axis=1, keepdims=True)
            logits = jnp.where(idx == e, logit, logits)
        m1 = jnp.max(logits, axis=1, keepdims=True)
        i1 = jnp.min(jnp.where(logits == m1, idx, n_exp), axis=1, keepdims=True)
        rest = jnp.where(idx == i1, -jnp.inf, logits)
        m2 = jnp.max(rest, axis=1, keepdims=True)
        i2 = jnp.min(jnp.where(rest == m2, idx, n_exp), axis=1, keepdims=True)
        e2 = jnp.exp(m2 - m1)
        w1 = 1.0 / (1.0 + e2)
        w2 = e2 / (1.0 + e2)
        chosen = jnp.where((idx == i1) | (idx == i2), 1.0, 0.0)
        before = jnp.dot(earlier, chosen.astype(BF16), preferred_element_type=F32) + seen
        seen = seen + jnp.sum(chosen, axis=0, keepdims=True)
        rank1 = jnp.sum(jnp.where(idx == i1, before, 0.0), axis=1, keepdims=True)
        rank2 = jnp.sum(jnp.where(idx == i2, before, 0.0), axis=1, keepdims=True)
        rec = jnp.zeros((sub, ROUTE_COLS), F32)
        for c, val in ((R_E1, i1.astype(F32)), (R_E2, i2.astype(F32)), (R_W1, w1), (R_W2, w2),
                       (R_RANK1, rank1), (R_RANK2, rank2)):
            rec = jnp.where(rcol == c, val, rec)
        route_ref[rows, :] = rec
        hi1 = jnp.floor(rank1 * (1.0 / RANK_RADIX))
        hi2 = jnp.floor(rank2 * (1.0 / RANK_RADIX))
        ints = jnp.zeros((sub, ROUTE_COLS), F32)
        for c, val in ((T_E1, i1.astype(F32)), (T_E2, i2.astype(F32)), (T_RANK1_HI, hi1), (T_RANK2_HI, hi2),
                       (T_RANK1_LO, rank1 - RANK_RADIX * hi1), (T_RANK2_LO, rank2 - RANK_RADIX * hi2)):
            ints = jnp.where(rcol == c, val, ints)
        route_t_ref[:, rows] = lax.dot_general(eye, ints.astype(BF16), nt, preferred_element_type=F32)
    cnt_ref[...] = seen


def _attn_out(h, o, mod_l, ng, w_out, w_router_t, tm):
    b, s, d = h.shape
    n_exp = w_router_t.shape[0]
    tok = lambda n: pl.BlockSpec((None, tm, n), lambda bi, i: (bi, i, 0))
    return pl.pallas_call(
        _attn_out_body,
        grid=(b, s // tm),
        in_specs=[
            pl.BlockSpec((None, 6, d), lambda bi, i: (bi, 0, 0)),
            _full((1, d)), tok(d), tok(o.shape[2]), _full(w_out.shape), _full(w_router_t.shape),
        ],
        out_specs=[tok(d), tok(d), tok(ROUTE_COLS),
                   pl.BlockSpec((None, None, 1, n_exp), lambda bi, i: (bi, i, 0, 0)),
                   pl.BlockSpec((None, None, ROUTE_COLS, tm), lambda bi, i: (bi, i, 0, 0))],
        out_shape=[jax.ShapeDtypeStruct(h.shape, F32), jax.ShapeDtypeStruct(h.shape, BF16),
                   jax.ShapeDtypeStruct((b, s, ROUTE_COLS), F32),
                   jax.ShapeDtypeStruct((b, s // tm, 1, n_exp), F32),
                   jax.ShapeDtypeStruct((b, s // tm, ROUTE_COLS, tm), F32)],
        compiler_params=_params("arbitrary", "arbitrary"),
        name="attn_out_router",
    )(mod_l, ng, h, o, w_out, w_router_t)


ROW_ALIGN = 8
GATE_LANES = LANES
G_W1, G_W2, G_E1 = 0, 3, 6


def _chunk_sizes(max_rows):
    sizes = []
    size = ROW_ALIGN
    while size <= max_rows:
        sizes.append(size)
        size *= 2
    return tuple(reversed(sizes))


def _chunked_copies(n, src, src_row, dst, dst_row, sems, sem_index, sizes):
    pairs = []
    for k, size in enumerate(sizes):
        done = n & (-2 * size)
        copy = pltpu.make_async_copy(
            src.at[pl.ds(pl.multiple_of(src_row + done, ROW_ALIGN), size)],
            dst.at[pl.ds(pl.multiple_of(dst_row + done, ROW_ALIGN), size)],
            sems.at[(*sem_index, k)])
        pairs.append(((n & size) != 0, copy))
    return pairs


def _start_all(pairs):
    for cond, copy in pairs:
        pl.when(cond)(copy.start)


def _wait_all(pairs):
    for cond, copy in pairs:
        pl.when(cond)(copy.wait)


def _selection(route, ls_ref, base, n_exp, r_cap):
    tm = route.shape[0]
    e1 = route[:, R_E1:R_E1 + 1].astype(jnp.int32)
    e2 = route[:, R_E2:R_E2 + 1].astype(jnp.int32)
    row1 = route[:, R_RANK1:R_RANK1 + 1].astype(jnp.int32)
    row2 = route[:, R_RANK2:R_RANK2 + 1].astype(jnp.int32)
    for e in range(n_exp):
        start = ls_ref[base + e]
        row1 = row1 + jnp.where(e1 == e, start, 0)
        row2 = row2 + jnp.where(e2 == e, start, 0)
    col = lax.broadcasted_iota(jnp.int32, (tm, r_cap), 1)
    return jnp.where((col == row1) | (col == row2), 1.0, 0.0).astype(BF16)


def _split3(w):
    a = w.astype(BF16).astype(F32)
    r = w - a
    b = r.astype(BF16).astype(F32)
    return a, b, r - b


def _selection_t(route_t, ls_ref, base, n_exp, r_cap):
    tm = route_t.shape[1]
    e1 = route_t[T_E1:T_E1 + 1, :].astype(jnp.int32)
    e2 = route_t[T_E2:T_E2 + 1, :].astype(jnp.int32)
    row1 = (route_t[T_RANK1_HI:T_RANK1_HI + 1, :] * RANK_RADIX + route_t[T_RANK1_LO:T_RANK1_LO + 1, :]).astype(jnp.int32)
    row2 = (route_t[T_RANK2_HI:T_RANK2_HI + 1, :] * RANK_RADIX + route_t[T_RANK2_LO:T_RANK2_LO + 1, :]).astype(jnp.int32)
    for e in range(n_exp):
        start = ls_ref[base + e]
        row1 = row1 + jnp.where(e1 == e, start, 0)
        row2 = row2 + jnp.where(e2 == e, start, 0)
    ridx = lax.broadcasted_iota(jnp.int32, (r_cap, tm), 0)
    return jnp.where((ridx == row1) | (ridx == row2), 1.0, 0.0).astype(BF16)


def _dispatch_body(ls_ref, off_ref, n8_ref, tail_off_ref, tail_n_ref, na_ref, x_ref, route_ref, route_t_ref,
                   xs_ref, local_ref, sems, *, n_exp, sizes, tm_moe, min_active):
    t = pl.program_id(0)
    base = t * n_exp
    tm, d = x_ref.shape
    r_cap = local_ref.shape[1]
    route = route_ref[...]
    sel = _selection_t(route_t_ref[...], ls_ref, base, n_exp, r_cap)
    lane = lax.broadcasted_iota(jnp.int32, (tm, GATE_LANES), 1)
    extra = jnp.zeros((tm, GATE_LANES), F32)
    terms = _split3(route[:, R_W1:R_W1 + 1]) + _split3(route[:, R_W2:R_W2 + 1]) + (route[:, R_E1:R_E1 + 1],)
    for k, term in enumerate(terms):
        extra = jnp.where(lane == k, term, extra)
    xa = jnp.concatenate([x_ref[...], extra.astype(BF16)], axis=1)
    rows = jnp.dot(sel, xa, preferred_element_type=F32)
    ridx = lax.broadcasted_iota(jnp.int32, (r_cap, 1), 0)
    owner = jnp.zeros((r_cap, 1), jnp.int32)
    for e in range(1, n_exp):
        owner = owner + jnp.where(ridx >= ls_ref[base + e], 1, 0)
    g = rows[:, d:]
    gate1 = g[:, G_W1:G_W1 + 1] + g[:, G_W1 + 1:G_W1 + 2] + g[:, G_W1 + 2:G_W1 + 3]
    gate2 = g[:, G_W2:G_W2 + 1] + g[:, G_W2 + 1:G_W2 + 2] + g[:, G_W2 + 2:G_W2 + 3]
    gate = jnp.where(g[:, G_E1:G_E1 + 1].astype(jnp.int32) == owner, gate1, gate2)
    slot = lax.rem(t, 2)

    def run_copies(tile, slot_):
        b0 = tile * n_exp
        pairs = []
        for e in range(n_exp):
            pairs += _chunked_copies(n8_ref[b0 + e], local_ref.at[slot_], ls_ref[b0 + e], xs_ref, off_ref[b0 + e],
                                     sems, (slot_, e), sizes[0])
        return pairs

    @pl.when(t >= 2)
    def _():
        _wait_all(run_copies(jnp.maximum(t - 2, 0), slot))

    local_ref[slot, :, :d] = rows[:, :d]
    local_ref[slot, :, d:] = jnp.broadcast_to(gate, (r_cap, GATE_LANES))
    _start_all(run_copies(t, slot))

    @pl.when(t == pl.num_programs(0) - 1)
    def _():
        @pl.when(t >= 1)
        def _():
            _wait_all(run_copies(jnp.maximum(t - 1, 0), 1 - slot))

        _wait_all(run_copies(t, slot))
        zeros_ref = local_ref.at[0]
        zeros_ref[...] = jnp.zeros_like(zeros_ref)
        tails = []
        for e in range(n_exp):
            tails += _chunked_copies(tail_n_ref[e], zeros_ref, 0, xs_ref, tail_off_ref[e], sems, (0, e), sizes[1])
        _start_all(tails)
        _wait_all(tails)
        n_tiles = xs_ref.shape[0] // tm_moe
        spare = []
        for k in range(n_tiles - min_active):
            tile = na_ref[0] + k
            copy = pltpu.make_async_copy(
                zeros_ref.at[pl.ds(0, tm_moe)],
                xs_ref.at[pl.ds(pl.multiple_of(jnp.minimum(tile, n_tiles - 1) * tm_moe, tm_moe), tm_moe)],
                sems.at[0, k % n_exp, k // n_exp])
            spare.append((tile < n_tiles, copy))
        _start_all(spare)
        _wait_all(spare)


def _dispatch(xn, route, route_t, tables, n_rows, tm, tm_moe):
    n_tok, d = xn.shape
    n_exp = tables["tail_n"].shape[0]
    r_cap = 2 * tm + LANES
    sizes = _chunk_sizes(tm), _chunk_sizes(tm_moe - ROW_ALIGN)
    assert tm_moe <= r_cap
    min_active = TOP_K * n_tok // tm_moe
    n_sems = max(len(sizes[0]), len(sizes[1]))
    assert n_rows // tm_moe - min_active <= n_exp * n_sems
    grid_spec = pltpu.PrefetchScalarGridSpec(
        num_scalar_prefetch=6,
        grid=(n_tok // tm,),
        in_specs=[pl.BlockSpec((tm, d), lambda t, *_: (t, 0)),
                  pl.BlockSpec((tm, ROUTE_COLS), lambda t, *_: (t, 0)),
                  pl.BlockSpec((None, ROUTE_COLS, tm), lambda t, *_: (t, 0, 0))],
        out_specs=pl.BlockSpec(memory_space=pl.ANY),
        scratch_shapes=[pltpu.VMEM((2, r_cap, d + GATE_LANES), F32),
                        pltpu.SemaphoreType.DMA((2, n_exp, n_sems))],
    )
    return pl.pallas_call(
        functools.partial(_dispatch_body, n_exp=n_exp, sizes=sizes, tm_moe=tm_moe, min_active=min_active),
        grid_spec=grid_spec,
        out_shape=jax.ShapeDtypeStruct((n_rows, d + GATE_LANES), F32),
        compiler_params=_params("arbitrary"),
        name="moe_dispatch",
    )(tables["ls"], tables["off"], tables["n8"], tables["tail_off"], tables["tail_n"], tables["n_active"],
      xn, route, route_t)


def _moe_body(te_ref, na_ref, x_ref, wg_ref, wu_ref, wd_ref, o_ref, xb_ref, acc_ref):
    i = pl.program_id(0)
    j = pl.program_id(1)
    d = o_ref.shape[1]

    @pl.when(i < na_ref[0])
    def _():
        @pl.when(j == 0)
        def _():
            xb_ref[...] = x_ref[:, :d].astype(BF16)
            acc_ref[...] = jnp.zeros_like(acc_ref)

        acc_ref[...] += _swiglu_part(xb_ref[...], wg_ref[...], wu_ref[...], wd_ref[...])

        @pl.when(j == pl.num_programs(1) - 1)
        def _():
            o_ref[...] = acc_ref[...] * x_ref[:, d:d + 1]

    @pl.when((i >= na_ref[0]) & (j == 0))
    def _():
        o_ref[...] = jnp.zeros_like(o_ref)


def _moe(xs, tile_expert, n_active, w_gate, w_up, w_down, tm, tf):
    n_rows, width = xs.shape
    n_exp, d, f = w_gate.shape
    nj = f // tf
    row = lambda i, j, te, na: (i, 0)
    jj = lambda i, j, na: jnp.where(i < na[0], j, nj - 1)
    grid_spec = pltpu.PrefetchScalarGridSpec(
        num_scalar_prefetch=2,
        grid=(n_rows // tm, nj),
        in_specs=[
            pl.BlockSpec((tm, width), row),
            pl.BlockSpec((None, d, tf), lambda i, j, te, na: (te[i], 0, jj(i, j, na))),
            pl.BlockSpec((None, d, tf), lambda i, j, te, na: (te[i], 0, jj(i, j, na))),
            pl.BlockSpec((None, tf, d), lambda i, j, te, na: (te[i], jj(i, j, na), 0)),
        ],
        out_specs=pl.BlockSpec((tm, d), row),
        scratch_shapes=[pltpu.VMEM((tm, d), BF16), pltpu.VMEM((tm, d), F32)],
    )
    return pl.pallas_call(
        _moe_body,
        grid_spec=grid_spec,
        out_shape=jax.ShapeDtypeStruct((n_rows, d), F32),
        compiler_params=_params("arbitrary", "arbitrary"),
        name="moe_ffn",
    )(tile_expert, n_active, xs, w_gate, w_up, w_down)


def _combine_body(ls_ref, off_ref, n8_ref, route_ref, h_ref, mod_ref, fg_ref, ys_ref, o_ref,
                  local_ref, sems, *, n_exp, sizes):
    n_steps = pl.num_programs(0) * pl.num_programs(1)
    t = pl.program_id(0) * pl.num_programs(1) + pl.program_id(1)
    base = t * n_exp
    r_cap = local_ref.shape[1]
    slot = lax.rem(t, 2)

    def run_copies(tile, slot_):
        b0 = tile * n_exp
        pairs = []
        for e in range(n_exp):
            pairs += _chunked_copies(n8_ref[b0 + e], ys_ref, off_ref[b0 + e], local_ref.at[slot_], ls_ref[b0 + e],
                                     sems, (slot_, e), sizes)
        return pairs

    def fetch(tile, slot_):
        local_ref[slot_] = jnp.zeros(local_ref.shape[1:], local_ref.dtype)
        _start_all(run_copies(tile, slot_))

    @pl.when(t == 0)
    def _():
        fetch(t, slot)

    @pl.when(t + 1 < n_steps)
    def _():
        fetch(jnp.minimum(t + 1, n_steps - 1), 1 - slot)

    sel = _selection(route_ref[...], ls_ref, base, n_exp, r_cap)
    _wait_all(run_copies(t, slot))
    y = jnp.dot(sel, local_ref[slot].astype(BF16), preferred_element_type=F32)
    h2 = h_ref[...] + mod_ref[5:6, :] * y
    ms = jnp.mean(h2 * h2, axis=-1, keepdims=True)
    o_ref[...] = h2 * lax.rsqrt(ms + EPS) * fg_ref[...]


def _combine(ys, route, h1, mod_l, final_g, tables, tm):
    b, s, d = h1.shape
    n_exp = tables["tail_n"].shape[0]
    r_cap = 2 * tm + LANES
    sizes = _chunk_sizes(tm)
    tok = lambda n: pl.BlockSpec((None, tm, n), lambda bi, i, *_: (bi, i, 0))
    grid_spec = pltpu.PrefetchScalarGridSpec(
        num_scalar_prefetch=3,
        grid=(b, s // tm),
        in_specs=[tok(ROUTE_COLS), tok(d),
                  pl.BlockSpec((None, 6, d), lambda bi, i, *_: (bi, 0, 0)),
                  pl.BlockSpec((1, d), lambda bi, i, *_: (0, 0)),
                  pl.BlockSpec(memory_space=pl.ANY)],
        out_specs=tok(d),
        scratch_shapes=[pltpu.VMEM((2, r_cap, d), F32), pltpu.SemaphoreType.DMA((2, n_exp, len(sizes)))],
    )
    return pl.pallas_call(
        functools.partial(_combine_body, n_exp=n_exp, sizes=sizes),
        grid_spec=grid_spec,
        out_shape=jax.ShapeDtypeStruct(h1.shape, F32),
        compiler_params=_params("arbitrary", "arbitrary"),
        name="moe_combine",
    )(tables["ls"], tables["off"], tables["n8"], route, h1, mod_l, final_g, ys)


def _routing_tables(cnt, tm_moe, n_tiles_moe):
    n = cnt.astype(jnp.int32)
    n8 = (n + ROW_ALIGN - 1) // ROW_ALIGN * ROW_ALIGN
    ls = jnp.cumsum(n8, axis=1) - n8
    rows = jnp.sum(n8, axis=0)
    rows_pad = (rows + tm_moe - 1) // tm_moe * tm_moe
    ends = jnp.cumsum(rows_pad)
    ebase = ends - rows_pad
    off = ebase[None, :] + jnp.cumsum(n8, axis=0) - n8
    n_active = ends[-1] // tm_moe
    tile_start = jnp.arange(n_tiles_moe, dtype=jnp.int32) * tm_moe
    te = jnp.sum((tile_start[:, None] >= ends[None, :]).astype(jnp.int32), axis=1)
    te = jnp.where(tile_start < ends[-1], te, te[n_active - 1])
    flat = lambda a: a.reshape(-1).astype(jnp.int32)
    return dict(ls=flat(ls), off=flat(off), n8=flat(n8), tail_off=flat(ebase + rows),
                tail_n=flat(rows_pad - rows), tile_expert=te, n_active=flat(n_active))


def _rope_tables(seq):
    pos = jnp.arange(seq, dtype=jnp.int32)
    lane = jnp.arange(LANES, dtype=jnp.int32) % HEAD_DIM
    quarter = HEAD_DIM // 4
    inv = ROPE_BASE ** (-(lane % quarter).astype(F32) / quarter)
    coord = jnp.where((lane < HEAD_DIM // 2)[None, :], (pos // GRID_W)[:, None], (pos % GRID_W)[:, None])
    ang = coord.astype(F32) * inv[None, :]
    second = ((lane % (HEAD_DIM // 2)) >= quarter)[None, :]
    sin = jnp.sin(ang)
    return jnp.cos(ang), jnp.where(second, 0.0, -sin), jnp.where(second, sin, 0.0)


def _head_order(n_q_heads):
    order = []
    for p in range(n_q_heads // (2 * Q_PER_KV)):
        for g in range(Q_PER_KV):
            order += [2 * p * Q_PER_KV + g, (2 * p + 1) * Q_PER_KV + g]
    return order


def kernel(x, c, ctx, c_ctx, ada_w, ada_b, norm_g, final_g, a_w_in, a_v_g, a_v_b, a_w_s, a_b_s, a_w_out,
           b_w_qkv, b_sink, b_w_out, ffn_w_gate, ffn_w_up, ffn_w_down,
           moe_w_router, moe_w_gate, moe_w_up, moe_w_down):
    batch, seq, d = x.shape
    ctx_len = ctx.shape[1]
    depth = ada_w.shape[0]
    assert depth == 2 and seq % 1024 == 0 and ctx_len % CHUNK == 0

    mod_rows = 8
    c_rows = jnp.concatenate([c, c_ctx[None, :], jnp.zeros((mod_rows - batch - 1, d), F32)], axis=0)
    mod = _ada_mod(c_rows, ada_w, ada_b).reshape(depth, mod_rows, 6, d)
    lat_row = lambda bi: bi
    ctx_row = lambda bi: batch

    a_width = a_w_out.shape[1]
    groups = a_w_s.shape[1]
    w_in = a_w_in[0].astype(BF16)
    w_s = a_w_s[0]
    wpair = jnp.concatenate([w_s[0::2], w_s[1::2]], axis=2).astype(BF16)
    bias = jnp.repeat(a_b_s[0].T, a_width // groups, axis=1)
    w_out_a = a_w_out[0].astype(BF16)
    ng0 = norm_g[0, 0][None, :]
    ng1 = norm_g[0, 1][None, :]
    wg0, wu0, wd0 = (w[0].astype(BF16) for w in (ffn_w_gate, ffn_w_up, ffn_w_down))
    vg, vb = a_v_g[0][None, :], a_v_b[0][None, :]

    def layer0(t, row):
        t = _mixer_a(t, mod[0], row, ng0, w_in, vg, vb, wpair, bias, w_out_a, tm=512)
        return _dense_ffn(t, mod[0], row, ng1, wg0, wu0, wd0, tm=512, tf=1792)

    h = layer0(x, lat_row)
    z = layer0(ctx.reshape(1, batch * ctx_len, d), ctx_row)

    w_qkv = b_w_qkv[0]
    n_heads = b_sink.shape[1]
    nq = n_heads * HEAD_DIM
    n_kv = n_heads // Q_PER_KV
    nkv = n_kv * HEAD_DIM
    order = jnp.array(_head_order(n_heads), dtype=jnp.int32)
    wq = w_qkv[:, :nq].reshape(d, n_heads, HEAD_DIM)[:, order].reshape(d, nq)
    wk = w_qkv[:, nq:nq + nkv].reshape(d, n_kv, HEAD_DIM)
    zero = jnp.zeros_like(wk)
    even = (jnp.arange(n_kv) % 2 == 0)[None, :, None]
    wk_ext = jnp.concatenate([jnp.where(even, wk, zero), jnp.where(even, zero, wk)], axis=2).reshape(d, 2 * nkv)
    wv = w_qkv[:, nq + nkv:]
    w_lat = jnp.concatenate([wq, wk_ext, wv], axis=1).astype(BF16)
    w_ctx = jnp.concatenate([wk_ext, wv], axis=1).astype(BF16)
    w_out_b = b_w_out[0].reshape(n_heads, HEAD_DIM, d)[order].reshape(nq, d).astype(BF16)
    sink = b_sink[0]
    ng0 = norm_g[1, 0][None, :]
    ng1 = norm_g[1, 1][None, :]

    tables = _rope_tables(seq)
    q, k, v = _qkv(h, mod[1], lat_row, ng0, w_lat, tables, nq, 2 * nkv, nkv, tm=512)
    n_ctx = batch * ctx_len
    no_rope = (jnp.ones((n_ctx, LANES), F32), jnp.zeros((n_ctx, LANES), F32), jnp.zeros((n_ctx, LANES), F32))
    kc, vc = _qkv(z, mod[1], ctx_row, ng0, w_ctx, no_rope, 0, 2 * nkv, nkv, tm=512)
    kc = kc.reshape(batch, ctx_len, 2 * nkv)
    vc = vc.reshape(batch, ctx_len, nkv)
    o = _attention(q, k, v, kc, vc, sink, tq=512)
    tm_tok, tm_moe = 512, 512
    h1, xn, route, cnt, route_t = _attn_out(h, o, mod[1], ng1, w_out_b, moe_w_router[0].T, tm=tm_tok)
    n_exp = moe_w_router.shape[2]
    n_tok = batch * seq
    n_tok_tiles = n_tok // tm_tok
    max_rows = TOP_K * n_tok + n_tok_tiles * n_exp * (ROW_ALIGN - 1) + n_exp * (tm_moe - 1)
    n_moe_tiles = -(-max_rows // tm_moe)
    tables = _routing_tables(cnt.reshape(n_tok_tiles, n_exp), tm_moe, n_moe_tiles)
    xs = _dispatch(xn.reshape(n_tok, d), route.reshape(n_tok, ROUTE_COLS),
                   route_t.reshape(n_tok_tiles, ROUTE_COLS, tm_tok), tables, n_moe_tiles * tm_moe, tm_tok, tm_moe)
    wg1, wu1, wd1 = (w[0].astype(BF16) for w in (moe_w_gate, moe_w_up, moe_w_down))
    ys = _moe(xs, tables["tile_expert"], tables["n_active"], wg1, wu1, wd1, tm=tm_moe, tf=1792)
    return _combine(ys, route, h1, mod[1], final_g[None, :], tables, tm_tok)
```

```python
import functools

import jax
import jax.numpy as jnp
from jax import lax
from jax.experimental import pallas as pl
from jax.experimental.pallas import tpu as pltpu

F32 = jnp.float32
BF16 = jnp.bfloat16

EPS = 1e-6
NEG_INF = -1e30
LANES = 128
HEAD_DIM = 64
Q_PER_KV = 4
WINDOW = 128
GRID_W = 64
ROPE_BASE = 10000.0
CHUNK = 128
TOP_K = 2
VMEM_LIMIT_BYTES = 56 * 1024 * 1024
HIGHEST = lax.Precision.HIGHEST


def _params(*semantics):
    return pltpu.CompilerParams(dimension_semantics=semantics, vmem_limit_bytes=VMEM_LIMIT_BYTES)


def _rms_mod(x, gain, scale, shift):
    ms = jnp.mean(x * x, axis=-1, keepdims=True)
    return (x * lax.rsqrt(ms + EPS) * gain) * (1.0 + scale) + shift


def _full(shape):
    return pl.BlockSpec(shape, lambda *_: (0,) * len(shape))


def _ada_body(c_ref, w_ref, b_ref, o_ref):
    c = c_ref[...]
    a = c * jax.nn.sigmoid(c)
    o_ref[...] = jnp.dot(a, w_ref[...], precision=HIGHEST, preferred_element_type=F32) + b_ref[...]


def _ada_mod(c_rows, ada_w, ada_b, tn=1536):
    depth, d, n = ada_w.shape
    rows = c_rows.shape[0]
    return pl.pallas_call(
        _ada_body,
        grid=(depth, n // tn),
        in_specs=[
            pl.BlockSpec((rows, d), lambda l, j: (0, 0)),
            pl.BlockSpec((None, d, tn), lambda l, j: (l, 0, j)),
            pl.BlockSpec((None, 1, tn), lambda l, j: (l, 0, j)),
        ],
        out_specs=pl.BlockSpec((None, rows, tn), lambda l, j: (l, 0, j)),
        out_shape=jax.ShapeDtypeStruct((depth, rows, n), F32),
        compiler_params=_params("arbitrary", "arbitrary"),
        name="ada_mod",
    )(c_rows, ada_w, ada_b.reshape(depth, 1, n))


def _mixer_a_body(mod_ref, ng_ref, x_ref, win_ref, vg_ref, vb_ref, wpair_ref, bias_ref, wout_ref,
                  o_ref, vlo_ref, vhi_ref, gated_ref):
    tm, d = x_ref.shape
    a_width = wout_ref.shape[0]
    x = x_ref[...]
    hn = _rms_mod(x, ng_ref[...], mod_ref[1:2, :], mod_ref[0:1, :])
    uv = jnp.dot(hn.astype(BF16), win_ref[...], preferred_element_type=F32)
    uv = jax.nn.gelu(uv, approximate=True)
    v = uv[:, a_width:]
    mu = jnp.mean(v, axis=-1, keepdims=True)
    vc = v - mu
    var = jnp.mean(vc * vc, axis=-1, keepdims=True)
    vn = vc * lax.rsqrt(var + EPS) * vg_ref[...] + vb_ref[...]
    lane = lax.broadcasted_iota(jnp.int32, vn.shape, 1)
    low = (lane & (LANES - 1)) < (LANES // 2)
    vlo_ref[...] = jnp.where(low, vn, 0.0).astype(BF16)
    vhi_ref[...] = jnp.where(low, 0.0, vn).astype(BF16)
    n_chunks = tm // CHUNK
    mixed = []
    for j in range(a_width // LANES):
        lanes = slice(j * LANES, (j + 1) * LANES)
        rhs = jnp.concatenate(
            [jnp.concatenate([vlo_ref[c * CHUNK:(c + 1) * CHUNK, lanes], vhi_ref[c * CHUNK:(c + 1) * CHUNK, lanes]],
                             axis=0) for c in range(n_chunks)], axis=1)
        mixed.append(jnp.dot(wpair_ref[j], rhs, preferred_element_type=F32))
    for c in range(n_chunks):
        rows = slice(c * CHUNK, (c + 1) * CHUNK)
        s = jnp.concatenate([m[:, c * LANES:(c + 1) * LANES] for m in mixed], axis=1) + bias_ref[...]
        gated_ref[rows, :] = (uv[rows, :a_width] * s).astype(BF16)
    dh = jnp.dot(gated_ref[...], wout_ref[...], preferred_element_type=F32)
    o_ref[...] = x + mod_ref[2:3, :] * dh


def _mixer_a(x, mod_l, mod_row, ng, w_in, v_g, v_b, wpair, bias, w_out, tm):
    b, s, d = x.shape
    tm = min(tm, s)
    a_width = w_out.shape[0]
    return pl.pallas_call(
        _mixer_a_body,
        grid=(b, s // tm),
        in_specs=[
            pl.BlockSpec((None, 6, d), lambda bi, i: (mod_row(bi), 0, 0)),
            _full((1, d)),
            pl.BlockSpec((None, tm, d), lambda bi, i: (bi, i, 0)),
            _full(w_in.shape), _full((1, a_width)), _full((1, a_width)),
            _full(wpair.shape), _full(bias.shape), _full(w_out.shape),
        ],
        out_specs=pl.BlockSpec((None, tm, d), lambda bi, i: (bi, i, 0)),
        out_shape=jax.ShapeDtypeStruct(x.shape, F32),
        scratch_shapes=[pltpu.VMEM((tm, a_width), BF16)] * 3,
        compiler_params=_params("arbitrary", "arbitrary"),
        name="mixer_a",
    )(mod_l, ng, x, w_in, v_g, v_b, wpair, bias, w_out)


def _swiglu_part(xb, wg, wu, wd):
    g = jnp.dot(xb, wg, preferred_element_type=F32)
    u = jnp.dot(xb, wu, preferred_element_type=F32)
    a = (g * jax.nn.sigmoid(g) * u).astype(BF16)
    return jnp.dot(a, wd, preferred_element_type=F32)


def _dense_ffn_body(mod_ref, ng_ref, x_ref, wg_ref, wu_ref, wd_ref, o_ref, xn_ref, acc_ref):
    j = pl.program_id(2)

    @pl.when(j == 0)
    def _():
        xn_ref[...] = _rms_mod(x_ref[...], ng_ref[...], mod_ref[4:5, :], mod_ref[3:4, :]).astype(BF16)
        acc_ref[...] = jnp.zeros_like(acc_ref)

    acc_ref[...] += _swiglu_part(xn_ref[...], wg_ref[...], wu_ref[...], wd_ref[...])

    @pl.when(j == pl.num_programs(2) - 1)
    def _():
        o_ref[...] = x_ref[...] + mod_ref[5:6, :] * acc_ref[...]


def _dense_ffn(x, mod_l, mod_row, ng, w_gate, w_up, w_down, tm, tf):
    b, s, d = x.shape
    tm = min(tm, s)
    f = w_gate.shape[1]
    nj = f // tf
    fblk = lambda bi, i, j: jnp.where((bi * (s // tm) + i) % 2 == 1, nj - 1 - j, j)
    return pl.pallas_call(
        _dense_ffn_body,
        grid=(b, s // tm, f // tf),
        in_specs=[
            pl.BlockSpec((None, 6, d), lambda bi, i, j: (mod_row(bi), 0, 0)),
            _full((1, d)),
            pl.BlockSpec((None, tm, d), lambda bi, i, j: (bi, i, 0)),
            pl.BlockSpec((d, tf), lambda bi, i, j: (0, fblk(bi, i, j))),
            pl.BlockSpec((d, tf), lambda bi, i, j: (0, fblk(bi, i, j))),
            pl.BlockSpec((tf, d), lambda bi, i, j: (fblk(bi, i, j), 0)),
        ],
        out_specs=pl.BlockSpec((None, tm, d), lambda bi, i, j: (bi, i, 0)),
        out_shape=jax.ShapeDtypeStruct(x.shape, F32),
        scratch_shapes=[pltpu.VMEM((tm, d), BF16), pltpu.VMEM((tm, d), F32)],
        compiler_params=_params("arbitrary", "arbitrary", "arbitrary"),
        name="dense_ffn",
    )(mod_l, ng, x, w_gate, w_up, w_down)


def _rope(y, cos, sin_up, sin_dn):
    w = y.shape[1]
    reps = w // LANES
    tile = lambda t: jnp.concatenate([t] * reps, axis=1)
    up = pltpu.roll(y, w - HEAD_DIM // 4, axis=1)
    dn = pltpu.roll(y, HEAD_DIM // 4, axis=1)
    return y * tile(cos) + up * tile(sin_up) + dn * tile(sin_dn)


def _qkv_body(mod_ref, ng_ref, x_ref, w_ref, cos_ref, sup_ref, sdn_ref, *out_refs, nq, nk):
    hn = _rms_mod(x_ref[...], ng_ref[...], mod_ref[1:2, :], mod_ref[0:1, :])
    y = jnp.dot(hn.astype(BF16), w_ref[...], preferred_element_type=F32)
    cos, sup, sdn = cos_ref[...], sup_ref[...], sdn_ref[...]
    if nq:
        q_ref, k_ref, v_ref = out_refs
        q_ref[...] = (_rope(y[:, :nq], cos, sup, sdn) * HEAD_DIM ** -0.5).astype(BF16)
    else:
        k_ref, v_ref = out_refs
    k_ref[...] = _rope(y[:, nq:nq + nk], cos, sup, sdn).astype(BF16)
    v_ref[...] = y[:, nq + nk:].astype(BF16)


def _qkv(x, mod_l, mod_row, ng, w, tables, nq, nk, nv, tm):
    b, s, d = x.shape
    tm = min(tm, s)
    widths = ([nq] if nq else []) + [nk, nv]
    tab_spec = pl.BlockSpec((tm, LANES), lambda bi, i: (i, 0))
    return pl.pallas_call(
        functools.partial(_qkv_body, nq=nq, nk=nk),
        grid=(b, s // tm),
        in_specs=[
            pl.BlockSpec((None, 6, d), lambda bi, i: (mod_row(bi), 0, 0)),
            _full((1, d)),
            pl.BlockSpec((None, tm, d), lambda bi, i: (bi, i, 0)),
            _full(w.shape), tab_spec, tab_spec, tab_spec,
        ],
        out_specs=[pl.BlockSpec((None, tm, n), lambda bi, i: (bi, i, 0)) for n in widths],
        out_shape=[jax.ShapeDtypeStruct((b, s, n), BF16) for n in widths],
        compiler_params=_params("arbitrary", "arbitrary"),
        name="qkv_rope" if nq else "kv_ctx",
    )(mod_l, ng, x, w, *tables)


def _attn_body(q_ref, kt_ref, kp_ref, kn_ref, vt_ref, vp_ref, vn_ref, kc_ref, vc_ref, sink_ref,
               o_ref, kwin_ref, vwin_ref, vctx_ref, bias_ref, *, seq):
    tq = q_ref.shape[0]
    nq = tq // WINDOW
    ctx = kc_ref.shape[0]
    kw = 3 * WINDOW
    rows = Q_PER_KV * WINDOW
    n_pairs = vt_ref.shape[1] // LANES
    i = pl.program_id(1)
    first_step = (pl.program_id(0) == 0) & (i == 0)

    @pl.when(first_step)
    def _():
        qi = lax.broadcasted_iota(jnp.int32, (WINDOW, kw), 0)
        kr = lax.broadcasted_iota(jnp.int32, (WINDOW, kw), 1)
        band = jnp.abs(qi - kr + WINDOW) <= WINDOW
        for e in range(4):
            ok = band
            if e & 1:
                ok = ok & (kr >= WINDOW)
            if e & 2:
                ok = ok & (kr < 2 * WINDOW)
            bias_ref[e] = jnp.where(ok, 0.0, NEG_INF)

    kwin_ref[0:WINDOW, :] = kp_ref[...]
    kwin_ref[WINDOW:WINDOW + tq, :] = kt_ref[...]
    kwin_ref[WINDOW + tq:, :] = kn_ref[...]
    for p in range(n_pairs):
        src = slice(p * LANES, (p + 1) * LANES)
        dst = slice(2 * p * LANES, (2 * p + 1) * LANES)
        one = slice((2 * p + 1) * LANES, (2 * p + 2) * LANES)
        vwin_ref[0:WINDOW, dst] = vp_ref[:, src]
        vwin_ref[WINDOW:WINDOW + tq, dst] = vt_ref[:, src]
        vwin_ref[WINDOW + tq:, dst] = vn_ref[:, src]
        vwin_ref[:, one] = jnp.ones((tq + 2 * WINDOW, LANES), BF16)
        vctx_ref[0:ctx, dst] = vc_ref[:, src]
        vctx_ref[ctx:, dst] = jnp.zeros((LANES, LANES), BF16)
        sink_row = lax.broadcasted_iota(jnp.int32, (ctx + LANES, LANES), 0) <= ctx
        vctx_ref[:, one] = jnp.where(sink_row, 1.0, 0.0).astype(BF16)
    nt = (((1,), (1,)), ((), ()))
    low = lax.broadcasted_iota(jnp.int32, (rows, LANES), 1) < HEAD_DIM
    n_blocks = seq // WINDOW

    def block(jj, carry):
        r0 = pl.multiple_of(jj * WINDOW, WINDOW)
        blk = i * nq + jj
        edge = jnp.where(blk == 0, 1, 0) + jnp.where(blk == n_blocks - 1, 2, 0)
        bias = bias_ref[edge]
        bias4 = jnp.concatenate([bias] * Q_PER_KV, axis=0)
        for p in range(n_pairs):
            q4 = jnp.concatenate(
                [q_ref[pl.ds(r0, WINDOW), (p * Q_PER_KV + g) * LANES:(p * Q_PER_KV + g + 1) * LANES]
                 for g in range(Q_PER_KV)], axis=0)
            v_loc = vwin_ref[pl.ds(r0, kw), 2 * p * LANES:(2 * p + 2) * LANES]
            v_ctx = vctx_ref[:, 2 * p * LANES:(2 * p + 2) * LANES]
            halves = []
            for half in range(2):
                h = 2 * p + half
                s_loc = lax.dot_general(q4, kwin_ref[pl.ds(r0, kw), h * LANES:(h + 1) * LANES], nt,
                                        preferred_element_type=F32) + bias4
                s_ctx = lax.dot_general(q4, kc_ref[:, h * LANES:(h + 1) * LANES], nt,
                                        preferred_element_type=F32)
                s = jnp.concatenate([s_loc, s_ctx, sink_ref[h]], axis=1)
                m = jnp.max(s, axis=1, keepdims=True)
                pe = jnp.exp(s - m).astype(BF16)
                o = (jnp.dot(pe[:, :kw], v_loc, preferred_element_type=F32)
                     + jnp.dot(pe[:, kw:], v_ctx, preferred_element_type=F32))
                halves.append(o[:, :LANES] / o[:, LANES:])
            o4 = jnp.where(low, halves[0], halves[1]).astype(o_ref.dtype)
            for g in range(Q_PER_KV):
                c0 = (p * Q_PER_KV + g) * LANES
                o_ref[pl.ds(r0, WINDOW), c0:c0 + LANES] = o4[g * WINDOW:(g + 1) * WINDOW, :]
        return carry

    lax.fori_loop(0, nq, block, 0, unroll=2)


def _attention(q, k, v, kc, vc, sink, tq):
    b, s, dq = q.shape
    nk, nv = k.shape[2], v.shape[2]
    ctx = kc.shape[1]
    nq = tq // WINDOW
    nb = s // WINDOW
    n_kv = sink.shape[0] // Q_PER_KV
    lane0 = jnp.arange(LANES) == 0
    sink_rows = jnp.repeat(sink.reshape(n_kv, Q_PER_KV), WINDOW, axis=1)
    sink_blk = jnp.where(lane0[None, None, :], sink_rows[:, :, None], NEG_INF).astype(F32)
    tile = lambda n: pl.BlockSpec((None, tq, n), lambda bi, i: (bi, i, 0))
    prev = lambda n: pl.BlockSpec((None, WINDOW, n), lambda bi, i: (bi, jnp.maximum(i * nq - 1, 0), 0))
    nxt = lambda n: pl.BlockSpec((None, WINDOW, n), lambda bi, i: (bi, jnp.minimum((i + 1) * nq, nb - 1), 0))
    cblk = lambda n: pl.BlockSpec((None, ctx, n), lambda bi, i: (bi, 0, 0))
    return pl.pallas_call(
        functools.partial(_attn_body, seq=s),
        grid=(b, s // tq),
        in_specs=[
            tile(dq), tile(nk), prev(nk), nxt(nk), tile(nv), prev(nv), nxt(nv), cblk(nk), cblk(nv),
            _full(sink_blk.shape),
        ],
        out_specs=tile(dq),
        out_shape=jax.ShapeDtypeStruct(q.shape, BF16),
        scratch_shapes=[pltpu.VMEM((tq + 2 * WINDOW, nk), BF16),
                        pltpu.VMEM((tq + 2 * WINDOW, 2 * nv), BF16),
                        pltpu.VMEM((ctx + LANES, 2 * nv), BF16),
                        pltpu.VMEM((4, WINDOW, 3 * WINDOW), F32)],
        compiler_params=_params("arbitrary", "arbitrary"),
        name="window_attn",
    )(q, k, k, k, v, v, v, kc, vc, sink_blk)


R_E1, R_E2, R_W1, R_W2, R_RANK1, R_RANK2 = range(6)
ROUTE_COLS = 8
T_E1, T_E2, T_RANK1_HI, T_RANK1_LO, T_RANK2_HI, T_RANK2_LO = range(6)
RANK_RADIX = 16
ROUTE_SUB = 512


def _attn_out_body(mod_ref, ng_ref, h_ref, o_ref, wout_ref, wrt_ref, h1_ref, xn_ref, route_ref, cnt_ref,
                   route_t_ref):
    tm = h_ref.shape[0]
    n_exp = wrt_ref.shape[0]
    sub = min(ROUTE_SUB, tm)
    idx = lax.broadcasted_iota(jnp.int32, (sub, n_exp), 1)
    rcol = lax.broadcasted_iota(jnp.int32, (sub, ROUTE_COLS), 1)
    row = lax.broadcasted_iota(jnp.int32, (sub, sub), 0)
    col = lax.broadcasted_iota(jnp.int32, (sub, sub), 1)
    earlier = jnp.where(col < row, 1.0, 0.0).astype(BF16)
    eye = jnp.where(lax.broadcasted_iota(jnp.int32, (ROUTE_COLS, ROUTE_COLS), 0)
                    == lax.broadcasted_iota(jnp.int32, (ROUTE_COLS, ROUTE_COLS), 1), 1.0, 0.0).astype(BF16)
    nt = (((1,), (1,)), ((), ()))
    seen = jnp.zeros((1, n_exp), F32)
    for sb in range(tm // sub):
        rows = slice(sb * sub, (sb + 1) * sub)
        dh = jnp.dot(o_ref[rows, :], wout_ref[...], preferred_element_type=F32)
        h1 = h_ref[rows, :] + mod_ref[2:3, :] * dh
        h1_ref[rows, :] = h1
        xn = _rms_mod(h1, ng_ref[...], mod_ref[4:5, :], mod_ref[3:4, :])
        xn_ref[rows, :] = xn.astype(BF16)
        logits = jnp.zeros((sub, n_exp), F32)
        for e in range(n_exp):
            logit = jnp.sum(xn * wrt_ref[e:e + 1, :], axis=1, keepdims=True)
            logits = jnp.where(idx == e, logit, logits)
        m1 = jnp.max(logits, axis=1, keepdims=True)
        i1 = jnp.min(jnp.where(logits == m1, idx, n_exp), axis=1, keepdims=True)
        rest = jnp.where(idx == i1, -jnp.inf, logits)
        m2 = jnp.max(rest, axis=1, keepdims=True)
        i2 = jnp.min(jnp.where(rest == m2, idx, n_exp), axis=1, keepdims=True)
        e2 = jnp.exp(m2 - m1)
        w1 = 1.0 / (1.0 + e2)
        w2 = e2 / (1.0 + e2)
        chosen = jnp.where((idx == i1) | (idx == i2), 1.0, 0.0)
        before = jnp.dot(earlier, chosen.astype(BF16), preferred_element_type=F32) + seen
        seen = seen + jnp.sum(chosen, axis=0, keepdims=True)
        rank1 = jnp.sum(jnp.where(idx == i1, before, 0.0), axis=1, keepdims=True)
        rank2 = jnp.sum(jnp.where(idx == i2, before, 0.0), axis=1, keepdims=True)
        rec = jnp.zeros((sub, ROUTE_COLS), F32)
        for c, val in ((R_E1, i1.astype(F32)), (R_E2, i2.astype(F32)), (R_W1, w1), (R_W2, w2),
                       (R_RANK1, rank1), (R_RANK2, rank2)):
            rec = jnp.where(rcol == c, val, rec)
        route_ref[rows, :] = rec
        hi1 = jnp.floor(rank1 * (1.0 / RANK_RADIX))
        hi2 = jnp.floor(rank2 * (1.0 / RANK_RADIX))
        ints = jnp.zeros((sub, ROUTE_COLS), F32)
        for c, val in ((T_E1, i1.astype(F32)), (T_E2, i2.astype(F32)), (T_RANK1_HI, hi1), (T_RANK2_HI, hi2),
                       (T_RANK1_LO, rank1 - RANK_RADIX * hi1), (T_RANK2_LO, rank2 - RANK_RADIX * hi2)):
            ints = jnp.where(rcol == c, val, ints)
        route_t_ref[:, rows] = lax.dot_general(eye, ints.astype(BF16), nt, preferred_element_type=F32)
    cnt_ref[...] = seen


def _attn_out(h, o, mod_l, ng, w_out, w_router_t, tm):
    b, s, d = h.shape
    n_exp = w_router_t.shape[0]
    tok = lambda n: pl.BlockSpec((None, tm, n), lambda bi, i: (bi, i, 0))
    return pl.pallas_call(
        _attn_out_body,
        grid=(b, s // tm),
        in_specs=[
            pl.BlockSpec((None, 6, d), lambda bi, i: (bi, 0, 0)),
            _full((1, d)), tok(d), tok(o.shape[2]), _full(w_out.shape), _full(w_router_t.shape),
        ],
        out_specs=[tok(d), tok(d), tok(ROUTE_COLS),
                   pl.BlockSpec((None, None, 1, n_exp), lambda bi, i: (bi, i, 0, 0)),
                   pl.BlockSpec((None, None, ROUTE_COLS, tm), lambda bi, i: (bi, i, 0, 0))],
        out_shape=[jax.ShapeDtypeStruct(h.shape, F32), jax.ShapeDtypeStruct(h.shape, BF16),
                   jax.ShapeDtypeStruct((b, s, ROUTE_COLS), F32),
                   jax.ShapeDtypeStruct((b, s // tm, 1, n_exp), F32),
                   jax.ShapeDtypeStruct((b, s // tm, ROUTE_COLS, tm), F32)],
        compiler_params=_params("arbitrary", "arbitrary"),
        name="attn_out_router",
    )(mod_l, ng, h, o, w_out, w_router_t)


ROW_ALIGN = 8
GATE_LANES = LANES
G_W1, G_W2, G_E1 = 0, 3, 6


def _chunk_sizes(max_rows):
    sizes = []
    size = ROW_ALIGN
    while size <= max_rows:
        sizes.append(size)
        size *= 2
    return tuple(reversed(sizes))


def _chunked_copies(n, src, src_row, dst, dst_row, sems, sem_index, sizes):
    pairs = []
    for k, size in enumerate(sizes):
        done = n & (-2 * size)
        copy = pltpu.make_async_copy(
            src.at[pl.ds(pl.multiple_of(src_row + done, ROW_ALIGN), size)],
            dst.at[pl.ds(pl.multiple_of(dst_row + done, ROW_ALIGN), size)],
            sems.at[(*sem_index, k)])
        pairs.append(((n & size) != 0, copy))
    return pairs


def _start_all(pairs):
    for cond, copy in pairs:
        pl.when(cond)(copy.start)


def _wait_all(pairs):
    for cond, copy in pairs:
        pl.when(cond)(copy.wait)


def _selection(route, ls_ref, base, n_exp, r_cap):
    tm = route.shape[0]
    e1 = route[:, R_E1:R_E1 + 1].astype(jnp.int32)
    e2 = route[:, R_E2:R_E2 + 1].astype(jnp.int32)
    row1 = route[:, R_RANK1:R_RANK1 + 1].astype(jnp.int32)
    row2 = route[:, R_RANK2:R_RANK2 + 1].astype(jnp.int32)
    for e in range(n_exp):
        start = ls_ref[base + e]
        row1 = row1 + jnp.where(e1 == e, start, 0)
        row2 = row2 + jnp.where(e2 == e, start, 0)
    col = lax.broadcasted_iota(jnp.int32, (tm, r_cap), 1)
    return jnp.where((col == row1) | (col == row2), 1.0, 0.0).astype(BF16)


def _split3(w):
    a = w.astype(BF16).astype(F32)
    r = w - a
    b = r.astype(BF16).astype(F32)
    return a, b, r - b


def _selection_t(route_t, ls_ref, base, n_exp, r_cap):
    tm = route_t.shape[1]
    e1 = route_t[T_E1:T_E1 + 1, :].astype(jnp.int32)
    e2 = route_t[T_E2:T_E2 + 1, :].astype(jnp.int32)
    row1 = (route_t[T_RANK1_HI:T_RANK1_HI + 1, :] * RANK_RADIX + route_t[T_RANK1_LO:T_RANK1_LO + 1, :]).astype(jnp.int32)
    row2 = (route_t[T_RANK2_HI:T_RANK2_HI + 1, :] * RANK_RADIX + route_t[T_RANK2_LO:T_RANK2_LO + 1, :]).astype(jnp.int32)
    for e in range(n_exp):
        start = ls_ref[base + e]
        row1 = row1 + jnp.where(e1 == e, start, 0)
        row2 = row2 + jnp.where(e2 == e, start, 0)
    ridx = lax.broadcasted_iota(jnp.int32, (r_cap, tm), 0)
    return jnp.where((ridx == row1) | (ridx == row2), 1.0, 0.0).astype(BF16)


def _dispatch_body(ls_ref, off_ref, n8_ref, tail_off_ref, tail_n_ref, na_ref, x_ref, route_ref, route_t_ref,
                   xs_ref, local_ref, sems, *, n_exp, sizes, tm_moe, min_active):
    t = pl.program_id(0)
    base = t * n_exp
    tm, d = x_ref.shape
    r_cap = local_ref.shape[1]
    route = route_ref[...]
    sel = _selection_t(route_t_ref[...], ls_ref, base, n_exp, r_cap)
    lane = lax.broadcasted_iota(jnp.int32, (tm, GATE_LANES), 1)
    extra = jnp.zeros((tm, GATE_LANES), F32)
    terms = _split3(route[:, R_W1:R_W1 + 1]) + _split3(route[:, R_W2:R_W2 + 1]) + (route[:, R_E1:R_E1 + 1],)
    for k, term in enumerate(terms):
        extra = jnp.where(lane == k, term, extra)
    xa = jnp.concatenate([x_ref[...], extra.astype(BF16)], axis=1)
    rows = jnp.dot(sel, xa, preferred_element_type=F32)
    ridx = lax.broadcasted_iota(jnp.int32, (r_cap, 1), 0)
    owner = jnp.zeros((r_cap, 1), jnp.int32)
    for e in range(1, n_exp):
        owner = owner + jnp.where(ridx >= ls_ref[base + e], 1, 0)
    g = rows[:, d:]
    gate1 = g[:, G_W1:G_W1 + 1] + g[:, G_W1 + 1:G_W1 + 2] + g[:, G_W1 + 2:G_W1 + 3]
    gate2 = g[:, G_W2:G_W2 + 1] + g[:, G_W2 + 1:G_W2 + 2] + g[:, G_W2 + 2:G_W2 + 3]
    gate = jnp.where(g[:, G_E1:G_E1 + 1].astype(jnp.int32) == owner, gate1, gate2)
    slot = lax.rem(t, 2)

    def run_copies(tile, slot_):
        b0 = tile * n_exp
        pairs = []
        for e in range(n_exp):
            pairs += _chunked_copies(n8_ref[b0 + e], local_ref.at[slot_], ls_ref[b0 + e], xs_ref, off_ref[b0 + e],
                                     sems, (slot_, e), sizes[0])
        return pairs

    @pl.when(t >= 2)
    def _():
        _wait_all(run_copies(jnp.maximum(t - 2, 0), slot))

    local_ref[slot, :, :d] = rows[:, :d]
    local_ref[slot, :, d:] = jnp.broadcast_to(gate, (r_cap, GATE_LANES))
    _start_all(run_copies(t, slot))

    @pl.when(t == pl.num_programs(0) - 1)
    def _():
        @pl.when(t >= 1)
        def _():
            _wait_all(run_copies(jnp.maximum(t - 1, 0), 1 - slot))

        _wait_all(run_copies(t, slot))
        zeros_ref = local_ref.at[0]
        zeros_ref[...] = jnp.zeros_like(zeros_ref)
        tails = []
        for e in range(n_exp):
            tails += _chunked_copies(tail_n_ref[e], zeros_ref, 0, xs_ref, tail_off_ref[e], sems, (0, e), sizes[1])
        _start_all(tails)
        _wait_all(tails)
        n_tiles = xs_ref.shape[0] // tm_moe
        spare = []
        for k in range(n_tiles - min_active):
            tile = na_ref[0] + k
            copy = pltpu.make_async_copy(
                zeros_ref.at[pl.ds(0, tm_moe)],
                xs_ref.at[pl.ds(pl.multiple_of(jnp.minimum(tile, n_tiles - 1) * tm_moe, tm_moe), tm_moe)],
                sems.at[0, k % n_exp, k // n_exp])
            spare.append((tile < n_tiles, copy))
        _start_all(spare)
        _wait_all(spare)


def _dispatch(xn, route, route_t, tables, n_rows, tm, tm_moe):
    n_tok, d = xn.shape
    n_exp = tables["tail_n"].shape[0]
    r_cap = 2 * tm + LANES
    sizes = _chunk_sizes(tm), _chunk_sizes(tm_moe - ROW_ALIGN)
    assert tm_moe <= r_cap
    min_active = TOP_K * n_tok // tm_moe
    n_sems = max(len(sizes[0]), len(sizes[1]))
    assert n_rows // tm_moe - min_active <= n_exp * n_sems
    grid_spec = pltpu.PrefetchScalarGridSpec(
        num_scalar_prefetch=6,
        grid=(n_tok // tm,),
        in_specs=[pl.BlockSpec((tm, d), lambda t, *_: (t, 0)),
                  pl.BlockSpec((tm, ROUTE_COLS), lambda t, *_: (t, 0)),
                  pl.BlockSpec((None, ROUTE_COLS, tm), lambda t, *_: (t, 0, 0))],
        out_specs=pl.BlockSpec(memory_space=pl.ANY),
        scratch_shapes=[pltpu.VMEM((2, r_cap, d + GATE_LANES), F32),
                        pltpu.SemaphoreType.DMA((2, n_exp, n_sems))],
    )
    return pl.pallas_call(
        functools.partial(_dispatch_body, n_exp=n_exp, sizes=sizes, tm_moe=tm_moe, min_active=min_active),
        grid_spec=grid_spec,
        out_shape=jax.ShapeDtypeStruct((n_rows, d + GATE_LANES), F32),
        compiler_params=_params("arbitrary"),
        name="moe_dispatch",
    )(tables["ls"], tables["off"], tables["n8"], tables["tail_off"], tables["tail_n"], tables["n_active"],
      xn, route, route_t)


def _moe_body(te_ref, na_ref, x_ref, wg_ref, wu_ref, wd_ref, o_ref, xb_ref, acc_ref):
    i = pl.program_id(0)
    j = pl.program_id(1)
    d = o_ref.shape[1]

    @pl.when(i < na_ref[0])
    def _():
        @pl.when(j == 0)
        def _():
            xb_ref[...] = x_ref[:, :d].astype(BF16)
            acc_ref[...] = jnp.zeros_like(acc_ref)

        acc_ref[...] += _swiglu_part(xb_ref[...], wg_ref[...], wu_ref[...], wd_ref[...])

        @pl.when(j == pl.num_programs(1) - 1)
        def _():
            o_ref[...] = acc_ref[...] * x_ref[:, d:d + 1]

    @pl.when((i >= na_ref[0]) & (j == 0))
    def _():
        o_ref[...] = jnp.zeros_like(o_ref)


def _moe(xs, tile_expert, n_active, w_gate, w_up, w_down, tm, tf):
    n_rows, width = xs.shape
    n_exp, d, f = w_gate.shape
    nj = f // tf
    row = lambda i, j, te, na: (i, 0)
    def jj(i, j, na):
        tile = jnp.minimum(i, na[0] - 1)
        visit = jnp.where(i < na[0], j, nj - 1)
        return jnp.where(tile % 2 == 1, nj - 1 - visit, visit)
    grid_spec = pltpu.PrefetchScalarGridSpec(
        num_scalar_prefetch=2,
        grid=(n_rows // tm, nj),
        in_specs=[
            pl.BlockSpec((tm, width), row),
            pl.BlockSpec((None, d, tf), lambda i, j, te, na: (te[i], 0, jj(i, j, na))),
            pl.BlockSpec((None, d, tf), lambda i, j, te, na: (te[i], 0, jj(i, j, na))),
            pl.BlockSpec((None, tf, d), lambda i, j, te, na: (te[i], jj(i, j, na), 0)),
        ],
        out_specs=pl.BlockSpec((tm, d), row),
        scratch_shapes=[pltpu.VMEM((tm, d), BF16), pltpu.VMEM((tm, d), F32)],
    )
    return pl.pallas_call(
        _moe_body,
        grid_spec=grid_spec,
        out_shape=jax.ShapeDtypeStruct((n_rows, d), F32),
        compiler_params=_params("arbitrary", "arbitrary"),
        name="moe_ffn",
    )(tile_expert, n_active, xs, w_gate, w_up, w_down)


def _combine_body(ls_ref, off_ref, n8_ref, route_ref, h_ref, mod_ref, fg_ref, ys_ref, o_ref,
                  local_ref, sems, *, n_exp, sizes):
    n_steps = pl.num_programs(0) * pl.num_programs(1)
    t = pl.program_id(0) * pl.num_programs(1) + pl.program_id(1)
    base = t * n_exp
    r_cap = local_ref.shape[1]
    slot = lax.rem(t, 2)

    def run_copies(tile, slot_):
        b0 = tile * n_exp
        pairs = []
        for e in range(n_exp):
            pairs += _chunked_copies(n8_ref[b0 + e], ys_ref, off_ref[b0 + e], local_ref.at[slot_], ls_ref[b0 + e],
                                     sems, (slot_, e), sizes)
        return pairs

    def fetch(tile, slot_):
        local_ref[slot_] = jnp.zeros(local_ref.shape[1:], local_ref.dtype)
        _start_all(run_copies(tile, slot_))

    @pl.when(t == 0)
    def _():
        fetch(t, slot)

    @pl.when(t + 1 < n_steps)
    def _():
        fetch(jnp.minimum(t + 1, n_steps - 1), 1 - slot)

    sel = _selection(route_ref[...], ls_ref, base, n_exp, r_cap)
    _wait_all(run_copies(t, slot))
    y = jnp.dot(sel, local_ref[slot].astype(BF16), preferred_element_type=F32)
    h2 = h_ref[...] + mod_ref[5:6, :] * y
    ms = jnp.mean(h2 * h2, axis=-1, keepdims=True)
    o_ref[...] = h2 * lax.rsqrt(ms + EPS) * fg_ref[...]


def _combine(ys, route, h1, mod_l, final_g, tables, tm):
    b, s, d = h1.shape
    n_exp = tables["tail_n"].shape[0]
    r_cap = 2 * tm + LANES
    sizes = _chunk_sizes(tm)
    tok = lambda n: pl.BlockSpec((None, tm, n), lambda bi, i, *_: (bi, i, 0))
    grid_spec = pltpu.PrefetchScalarGridSpec(
        num_scalar_prefetch=3,
        grid=(b, s // tm),
        in_specs=[tok(ROUTE_COLS), tok(d),
                  pl.BlockSpec((None, 6, d), lambda bi, i, *_: (bi, 0, 0)),
                  pl.BlockSpec((1, d), lambda bi, i, *_: (0, 0)),
                  pl.BlockSpec(memory_space=pl.ANY)],
        out_specs=tok(d),
        scratch_shapes=[pltpu.VMEM((2, r_cap, d), F32), pltpu.SemaphoreType.DMA((2, n_exp, len(sizes)))],
    )
    return pl.pallas_call(
        functools.partial(_combine_body, n_exp=n_exp, sizes=sizes),
        grid_spec=grid_spec,
        out_shape=jax.ShapeDtypeStruct(h1.shape, F32),
        compiler_params=_params("arbitrary", "arbitrary"),
        name="moe_combine",
    )(tables["ls"], tables["off"], tables["n8"], route, h1, mod_l, final_g, ys)


def _routing_tables(cnt, tm_moe, n_tiles_moe):
    n = cnt.astype(jnp.int32)
    n8 = (n + ROW_ALIGN - 1) // ROW_ALIGN * ROW_ALIGN
    ls = jnp.cumsum(n8, axis=1) - n8
    rows = jnp.sum(n8, axis=0)
    rows_pad = (rows + tm_moe - 1) // tm_moe * tm_moe
    ends = jnp.cumsum(rows_pad)
    ebase = ends - rows_pad
    off = ebase[None, :] + jnp.cumsum(n8, axis=0) - n8
    n_active = ends[-1] // tm_moe
    tile_start = jnp.arange(n_tiles_moe, dtype=jnp.int32) * tm_moe
    te = jnp.sum((tile_start[:, None] >= ends[None, :]).astype(jnp.int32), axis=1)
    te = jnp.where(tile_start < ends[-1], te, te[n_active - 1])
    flat = lambda a: a.reshape(-1).astype(jnp.int32)
    return dict(ls=flat(ls), off=flat(off), n8=flat(n8), tail_off=flat(ebase + rows),
                tail_n=flat(rows_pad - rows), tile_expert=te, n_active=flat(n_active))


def _rope_tables(seq):
    pos = jnp.arange(seq, dtype=jnp.int32)
    lane = jnp.arange(LANES, dtype=jnp.int32) % HEAD_DIM
    quarter = HEAD_DIM // 4
    inv = ROPE_BASE ** (-(lane % quarter).astype(F32) / quarter)
    coord = jnp.where((lane < HEAD_DIM // 2)[None, :], (pos // GRID_W)[:, None], (pos % GRID_W)[:, None])
    ang = coord.astype(F32) * inv[None, :]
    second = ((lane % (HEAD_DIM // 2)) >= quarter)[None, :]
    sin = jnp.sin(ang)
    return jnp.cos(ang), jnp.where(second, 0.0, -sin), jnp.where(second, sin, 0.0)


def _head_order(n_q_heads):
    order = []
    for p in range(n_q_heads // (2 * Q_PER_KV)):
        for g in range(Q_PER_KV):
            order += [2 * p * Q_PER_KV + g, (2 * p + 1) * Q_PER_KV + g]
    return order


def kernel(x, c, ctx, c_ctx, ada_w, ada_b, norm_g, final_g, a_w_in, a_v_g, a_v_b, a_w_s, a_b_s, a_w_out,
           b_w_qkv, b_sink, b_w_out, ffn_w_gate, ffn_w_up, ffn_w_down,
           moe_w_router, moe_w_gate, moe_w_up, moe_w_down):
    batch, seq, d = x.shape
    ctx_len = ctx.shape[1]
    depth = ada_w.shape[0]
    assert depth == 2 and seq % 1024 == 0 and ctx_len % CHUNK == 0

    mod_rows = 8
    c_rows = jnp.concatenate([c, c_ctx[None, :], jnp.zeros((mod_rows - batch - 1, d), F32)], axis=0)
    mod = _ada_mod(c_rows, ada_w, ada_b).reshape(depth, mod_rows, 6, d)
    lat_row = lambda bi: bi
    ctx_row = lambda bi: batch

    a_width = a_w_out.shape[1]
    groups = a_w_s.shape[1]
    w_in = a_w_in[0].astype(BF16)
    w_s = a_w_s[0]
    wpair = jnp.concatenate([w_s[0::2], w_s[1::2]], axis=2).astype(BF16)
    bias = jnp.repeat(a_b_s[0].T, a_width // groups, axis=1)
    w_out_a = a_w_out[0].astype(BF16)
    ng0 = norm_g[0, 0][None, :]
    ng1 = norm_g[0, 1][None, :]
    wg0, wu0, wd0 = (w[0].astype(BF16) for w in (ffn_w_gate, ffn_w_up, ffn_w_down))
    vg, vb = a_v_g[0][None, :], a_v_b[0][None, :]

    def layer0(t, row):
        t = _mixer_a(t, mod[0], row, ng0, w_in, vg, vb, wpair, bias, w_out_a, tm=1024)
        return _dense_ffn(t, mod[0], row, ng1, wg0, wu0, wd0, tm=512, tf=1792)

    h = layer0(x, lat_row)
    z = layer0(ctx.reshape(1, batch * ctx_len, d), ctx_row)

    w_qkv = b_w_qkv[0]
    n_heads = b_sink.shape[1]
    nq = n_heads * HEAD_DIM
    n_kv = n_heads // Q_PER_KV
    nkv = n_kv * HEAD_DIM
    order = jnp.array(_head_order(n_heads), dtype=jnp.int32)
    wq = w_qkv[:, :nq].reshape(d, n_heads, HEAD_DIM)[:, order].reshape(d, nq)
    wk = w_qkv[:, nq:nq + nkv].reshape(d, n_kv, HEAD_DIM)
    zero = jnp.zeros_like(wk)
    even = (jnp.arange(n_kv) % 2 == 0)[None, :, None]
    wk_ext = jnp.concatenate([jnp.where(even, wk, zero), jnp.where(even, zero, wk)], axis=2).reshape(d, 2 * nkv)
    wv = w_qkv[:, nq + nkv:]
    w_lat = jnp.concatenate([wq, wk_ext, wv], axis=1).astype(BF16)
    w_ctx = jnp.concatenate([wk_ext, wv], axis=1).astype(BF16)
    w_out_b = b_w_out[0].reshape(n_heads, HEAD_DIM, d)[order].reshape(nq, d).astype(BF16)
    sink = b_sink[0]
    ng0 = norm_g[1, 0][None, :]
    ng1 = norm_g[1, 1][None, :]

    tables = _rope_tables(seq)
    q, k, v = _qkv(h, mod[1], lat_row, ng0, w_lat, tables, nq, 2 * nkv, nkv, tm=1024)
    n_ctx = batch * ctx_len
    no_rope = (jnp.ones((n_ctx, LANES), F32), jnp.zeros((n_ctx, LANES), F32), jnp.zeros((n_ctx, LANES), F32))
    kc, vc = _qkv(z, mod[1], ctx_row, ng0, w_ctx, no_rope, 0, 2 * nkv, nkv, tm=512)
    kc = kc.reshape(batch, ctx_len, 2 * nkv)
    vc = vc.reshape(batch, ctx_len, nkv)
    o = _attention(q, k, v, kc, vc, sink, tq=1024)
    tm_tok, tm_moe = 512, 512
    h1, xn, route, cnt, route_t = _attn_out(h, o, mod[1], ng1, w_out_b, moe_w_router[0].T, tm=tm_tok)
    n_exp = moe_w_router.shape[2]
    n_tok = batch * seq
    n_tok_tiles = n_tok // tm_tok
    max_rows = TOP_K * n_tok + n_tok_tiles * n_exp * (ROW_ALIGN - 1) + n_exp * (tm_moe - 1)
    n_moe_tiles = -(-max_rows // tm_moe)
    tables = _routing_tables(cnt.reshape(n_tok_tiles, n_exp), tm_moe, n_moe_tiles)
    xs = _dispatch(xn.reshape(n_tok, d), route.reshape(n_tok, ROUTE_COLS),
                   route_t.reshape(n_tok_tiles, ROUTE_COLS, tm_tok), tables, n_moe_tiles * tm_moe, tm_tok, tm_moe)
    wg1, wu1, wd1 = (w[0].astype(BF16) for w in (moe_w_gate, moe_w_up, moe_w_down))
    ys = _moe(xs, tables["tile_expert"], tables["n_active"], wg1, wu1, wd1, tm=tm_moe, tf=1792)
    return _combine(ys, route, h1, mod[1], final_g[None, :], tables, tm_tok)
```

```python
import functools

import jax
import jax.numpy as jnp
from jax import lax
from jax.experimental import pallas as pl
from jax.experimental.pallas import tpu as pltpu

F32 = jnp.float32
BF16 = jnp.bfloat16

EPS = 1e-6
NEG_INF = -1e30
LANES = 128
HEAD_DIM = 64
Q_PER_KV = 4
WINDOW = 128
GRID_W = 64
ROPE_BASE = 10000.0
CHUNK = 128
TOP_K = 2
VMEM_LIMIT_BYTES = 56 * 1024 * 1024
HIGHEST = lax.Precision.HIGHEST


def _params(*semantics):
    return pltpu.CompilerParams(dimension_semantics=semantics, vmem_limit_bytes=VMEM_LIMIT_BYTES)


def _rms_mod(x, gain, scale, shift):
    ms = jnp.mean(x * x, axis=-1, keepdims=True)
    return (x * lax.rsqrt(ms + EPS) * gain) * (1.0 + scale) + shift


def _full(shape):
    return pl.BlockSpec(shape, lambda *_: (0,) * len(shape))


def _ada_body(c_ref, w_ref, b_ref, o_ref):
    c = c_ref[...]
    a = c * jax.nn.sigmoid(c)
    o_ref[...] = jnp.dot(a, w_ref[...], precision=HIGHEST, preferred_element_type=F32) + b_ref[...]


def _ada_mod(c_rows, ada_w, ada_b, tn=1536):
    depth, d, n = ada_w.shape
    rows = c_rows.shape[0]
    return pl.pallas_call(
        _ada_body,
        grid=(depth, n // tn),
        in_specs=[
            pl.BlockSpec((rows, d), lambda l, j: (0, 0)),
            pl.BlockSpec((None, d, tn), lambda l, j: (l, 0, j)),
            pl.BlockSpec((None, 1, tn), lambda l, j: (l, 0, j)),
        ],
        out_specs=pl.BlockSpec((None, rows, tn), lambda l, j: (l, 0, j)),
        out_shape=jax.ShapeDtypeStruct((depth, rows, n), F32),
        compiler_params=_params("arbitrary", "arbitrary"),
        name="ada_mod",
    )(c_rows, ada_w, ada_b.reshape(depth, 1, n))


def _mixer_a_body(mod_ref, ng_ref, x_ref, win_ref, vg_ref, vb_ref, wpair_ref, bias_ref, wout_ref,
                  o_ref, vlo_ref, vhi_ref, gated_ref):
    tm, d = x_ref.shape
    a_width = wout_ref.shape[0]
    x = x_ref[...]
    hn = _rms_mod(x, ng_ref[...], mod_ref[1:2, :], mod_ref[0:1, :])
    uv = jnp.dot(hn.astype(BF16), win_ref[...], preferred_element_type=F32)
    uv = jax.nn.gelu(uv, approximate=True)
    v = uv[:, a_width:]
    mu = jnp.mean(v, axis=-1, keepdims=True)
    vc = v - mu
    var = jnp.mean(vc * vc, axis=-1, keepdims=True)
    vn = vc * lax.rsqrt(var + EPS) * vg_ref[...] + vb_ref[...]
    lane = lax.broadcasted_iota(jnp.int32, vn.shape, 1)
    low = (lane & (LANES - 1)) < (LANES // 2)
    vlo_ref[...] = jnp.where(low, vn, 0.0).astype(BF16)
    vhi_ref[...] = jnp.where(low, 0.0, vn).astype(BF16)
    n_chunks = tm // CHUNK
    mixed = []
    for j in range(a_width // LANES):
        lanes = slice(j * LANES, (j + 1) * LANES)
        rhs = jnp.concatenate(
            [jnp.concatenate([vlo_ref[c * CHUNK:(c + 1) * CHUNK, lanes], vhi_ref[c * CHUNK:(c + 1) * CHUNK, lanes]],
                             axis=0) for c in range(n_chunks)], axis=1)
        mixed.append(jnp.dot(wpair_ref[j], rhs, preferred_element_type=F32))
    for c in range(n_chunks):
        rows = slice(c * CHUNK, (c + 1) * CHUNK)
        s = jnp.concatenate([m[:, c * LANES:(c + 1) * LANES] for m in mixed], axis=1) + bias_ref[...]
        gated_ref[rows, :] = (uv[rows, :a_width] * s).astype(BF16)
    dh = jnp.dot(gated_ref[...], wout_ref[...], preferred_element_type=F32)
    o_ref[...] = x + mod_ref[2:3, :] * dh


def _mixer_a(x, mod_l, mod_row, ng, w_in, v_g, v_b, wpair, bias, w_out, tm):
    b, s, d = x.shape
    tm = min(tm, s)
    a_width = w_out.shape[0]
    return pl.pallas_call(
        _mixer_a_body,
        grid=(b, s // tm),
        in_specs=[
            pl.BlockSpec((None, 6, d), lambda bi, i: (mod_row(bi), 0, 0)),
            _full((1, d)),
            pl.BlockSpec((None, tm, d), lambda bi, i: (bi, i, 0)),
            _full(w_in.shape), _full((1, a_width)), _full((1, a_width)),
            _full(wpair.shape), _full(bias.shape), _full(w_out.shape),
        ],
        out_specs=pl.BlockSpec((None, tm, d), lambda bi, i: (bi, i, 0)),
        out_shape=jax.ShapeDtypeStruct(x.shape, F32),
        scratch_shapes=[pltpu.VMEM((tm, a_width), BF16)] * 3,
        compiler_params=_params("arbitrary", "arbitrary"),
        name="mixer_a",
    )(mod_l, ng, x, w_in, v_g, v_b, wpair, bias, w_out)


def _swiglu_part(xb, wg, wu, wd):
    g = jnp.dot(xb, wg, preferred_element_type=F32)
    u = jnp.dot(xb, wu, preferred_element_type=F32)
    a = (g * jax.nn.sigmoid(g) * u).astype(BF16)
    return jnp.dot(a, wd, preferred_element_type=F32)


def _dense_ffn_body(mod_ref, ng_ref, x_ref, wg_ref, wu_ref, wd_ref, o_ref, xn_ref, acc_ref):
    j = pl.program_id(2)

    @pl.when(j == 0)
    def _():
        xn_ref[...] = _rms_mod(x_ref[...], ng_ref[...], mod_ref[4:5, :], mod_ref[3:4, :]).astype(BF16)
        acc_ref[...] = jnp.zeros_like(acc_ref)

    acc_ref[...] += _swiglu_part(xn_ref[...], wg_ref[...], wu_ref[...], wd_ref[...])

    @pl.when(j == pl.num_programs(2) - 1)
    def _():
        o_ref[...] = x_ref[...] + mod_ref[5:6, :] * acc_ref[...]


def _dense_ffn(x, mod_l, mod_row, ng, w_gate, w_up, w_down, tm, tf):
    b, s, d = x.shape
    tm = min(tm, s)
    f = w_gate.shape[1]
    nj = f // tf
    fblk = lambda bi, i, j: jnp.where((bi * (s // tm) + i) % 2 == 1, nj - 1 - j, j)
    return pl.pallas_call(
        _dense_ffn_body,
        grid=(b, s // tm, f // tf),
        in_specs=[
            pl.BlockSpec((None, 6, d), lambda bi, i, j: (mod_row(bi), 0, 0)),
            _full((1, d)),
            pl.BlockSpec((None, tm, d), lambda bi, i, j: (bi, i, 0)),
            pl.BlockSpec((d, tf), lambda bi, i, j: (0, fblk(bi, i, j))),
            pl.BlockSpec((d, tf), lambda bi, i, j: (0, fblk(bi, i, j))),
            pl.BlockSpec((tf, d), lambda bi, i, j: (fblk(bi, i, j), 0)),
        ],
        out_specs=pl.BlockSpec((None, tm, d), lambda bi, i, j: (bi, i, 0)),
        out_shape=jax.ShapeDtypeStruct(x.shape, F32),
        scratch_shapes=[pltpu.VMEM((tm, d), BF16), pltpu.VMEM((tm, d), F32)],
        compiler_params=_params("arbitrary", "arbitrary", "arbitrary"),
        name="dense_ffn",
    )(mod_l, ng, x, w_gate, w_up, w_down)


def _rope(y, cos, sin_up, sin_dn):
    w = y.shape[1]
    reps = w // LANES
    tile = lambda t: jnp.concatenate([t] * reps, axis=1)
    up = pltpu.roll(y, w - HEAD_DIM // 4, axis=1)
    dn = pltpu.roll(y, HEAD_DIM // 4, axis=1)
    return y * tile(cos) + up * tile(sin_up) + dn * tile(sin_dn)


def _qkv_body(mod_ref, ng_ref, x_ref, w_ref, cos_ref, sup_ref, sdn_ref, *out_refs, nq, nk):
    hn = _rms_mod(x_ref[...], ng_ref[...], mod_ref[1:2, :], mod_ref[0:1, :])
    y = jnp.dot(hn.astype(BF16), w_ref[...], preferred_element_type=F32)
    cos, sup, sdn = cos_ref[...], sup_ref[...], sdn_ref[...]
    if nq:
        q_ref, k_ref, v_ref = out_refs
        q_ref[...] = (_rope(y[:, :nq], cos, sup, sdn) * HEAD_DIM ** -0.5).astype(BF16)
    else:
        k_ref, v_ref = out_refs
    k_ref[...] = _rope(y[:, nq:nq + nk], cos, sup, sdn).astype(BF16)
    v_ref[...] = y[:, nq + nk:].astype(BF16)


def _qkv(x, mod_l, mod_row, ng, w, tables, nq, nk, nv, tm):
    b, s, d = x.shape
    tm = min(tm, s)
    widths = ([nq] if nq else []) + [nk, nv]
    tab_spec = pl.BlockSpec((tm, LANES), lambda bi, i: (i, 0))
    return pl.pallas_call(
        functools.partial(_qkv_body, nq=nq, nk=nk),
        grid=(b, s // tm),
        in_specs=[
            pl.BlockSpec((None, 6, d), lambda bi, i: (mod_row(bi), 0, 0)),
            _full((1, d)),
            pl.BlockSpec((None, tm, d), lambda bi, i: (bi, i, 0)),
            _full(w.shape), tab_spec, tab_spec, tab_spec,
        ],
        out_specs=[pl.BlockSpec((None, tm, n), lambda bi, i: (bi, i, 0)) for n in widths],
        out_shape=[jax.ShapeDtypeStruct((b, s, n), BF16) for n in widths],
        compiler_params=_params("arbitrary", "arbitrary"),
        name="qkv_rope" if nq else "kv_ctx",
    )(mod_l, ng, x, w, *tables)


def _attn_body(q_ref, kt_ref, kp_ref, kn_ref, vt_ref, vp_ref, vn_ref, kc_ref, vc_ref, sink_ref,
               o_ref, kwin_ref, vwin_ref, vctx_ref, bias_ref, *, seq):
    tq = q_ref.shape[0]
    nq = tq // WINDOW
    ctx = kc_ref.shape[0]
    kw = 3 * WINDOW
    rows = Q_PER_KV * WINDOW
    n_pairs = vt_ref.shape[1] // LANES
    i = pl.program_id(1)
    first_step = (pl.program_id(0) == 0) & (i == 0)

    @pl.when(first_step)
    def _():
        qi = lax.broadcasted_iota(jnp.int32, (WINDOW, kw), 0)
        kr = lax.broadcasted_iota(jnp.int32, (WINDOW, kw), 1)
        band = jnp.abs(qi - kr + WINDOW) <= WINDOW
        for e in range(4):
            ok = band
            if e & 1:
                ok = ok & (kr >= WINDOW)
            if e & 2:
                ok = ok & (kr < 2 * WINDOW)
            bias_ref[e] = jnp.where(ok, 0.0, NEG_INF)

    kwin_ref[0:WINDOW, :] = kp_ref[...]
    kwin_ref[WINDOW:WINDOW + tq, :] = kt_ref[...]
    kwin_ref[WINDOW + tq:, :] = kn_ref[...]
    for p in range(n_pairs):
        src = slice(p * LANES, (p + 1) * LANES)
        dst = slice(2 * p * LANES, (2 * p + 1) * LANES)
        one = slice((2 * p + 1) * LANES, (2 * p + 2) * LANES)
        vwin_ref[0:WINDOW, dst] = vp_ref[:, src]
        vwin_ref[WINDOW:WINDOW + tq, dst] = vt_ref[:, src]
        vwin_ref[WINDOW + tq:, dst] = vn_ref[:, src]
        vwin_ref[:, one] = jnp.ones((tq + 2 * WINDOW, LANES), BF16)
        vctx_ref[0:ctx, dst] = vc_ref[:, src]
        vctx_ref[ctx:, dst] = jnp.zeros((LANES, LANES), BF16)
        sink_row = lax.broadcasted_iota(jnp.int32, (ctx + LANES, LANES), 0) <= ctx
        vctx_ref[:, one] = jnp.where(sink_row, 1.0, 0.0).astype(BF16)
    nt = (((1,), (1,)), ((), ()))
    low = lax.broadcasted_iota(jnp.int32, (rows, LANES), 1) < HEAD_DIM
    n_blocks = seq // WINDOW

    def block(jj, carry):
        r0 = pl.multiple_of(jj * WINDOW, WINDOW)
        blk = i * nq + jj
        edge = jnp.where(blk == 0, 1, 0) + jnp.where(blk == n_blocks - 1, 2, 0)
        bias = bias_ref[edge]
        bias4 = jnp.concatenate([bias] * Q_PER_KV, axis=0)
        for p in range(n_pairs):
            q4 = jnp.concatenate(
                [q_ref[pl.ds(r0, WINDOW), (p * Q_PER_KV + g) * LANES:(p * Q_PER_KV + g + 1) * LANES]
                 for g in range(Q_PER_KV)], axis=0)
            v_all = jnp.concatenate([vwin_ref[pl.ds(r0, kw), 2 * p * LANES:(2 * p + 2) * LANES],
                                     vctx_ref[:, 2 * p * LANES:(2 * p + 2) * LANES]], axis=0)
            halves = []
            for half in range(2):
                h = 2 * p + half
                s_loc = lax.dot_general(q4, kwin_ref[pl.ds(r0, kw), h * LANES:(h + 1) * LANES], nt,
                                        preferred_element_type=F32) + bias4
                s_ctx = lax.dot_general(q4, kc_ref[:, h * LANES:(h + 1) * LANES], nt,
                                        preferred_element_type=F32)
                s = jnp.concatenate([s_loc, s_ctx, sink_ref[h]], axis=1)
                m = jnp.max(s, axis=1, keepdims=True)
                pe = jnp.exp(s - m).astype(BF16)
                o = jnp.dot(pe, v_all, preferred_element_type=F32)
                halves.append(o[:, :LANES] / o[:, LANES:])
            o4 = jnp.where(low, halves[0], halves[1]).astype(o_ref.dtype)
            for g in range(Q_PER_KV):
                c0 = (p * Q_PER_KV + g) * LANES
                o_ref[pl.ds(r0, WINDOW), c0:c0 + LANES] = o4[g * WINDOW:(g + 1) * WINDOW, :]
        return carry

    lax.fori_loop(0, nq, block, 0, unroll=8)


def _attention(q, k, v, kc, vc, sink, tq):
    b, s, dq = q.shape
    nk, nv = k.shape[2], v.shape[2]
    ctx = kc.shape[1]
    nq = tq // WINDOW
    nb = s // WINDOW
    n_kv = sink.shape[0] // Q_PER_KV
    lane0 = jnp.arange(LANES) == 0
    sink_rows = jnp.repeat(sink.reshape(n_kv, Q_PER_KV), WINDOW, axis=1)
    sink_blk = jnp.where(lane0[None, None, :], sink_rows[:, :, None], NEG_INF).astype(F32)
    tile = lambda n: pl.BlockSpec((None, tq, n), lambda bi, i: (bi, i, 0))
    prev = lambda n: pl.BlockSpec((None, WINDOW, n), lambda bi, i: (bi, jnp.maximum(i * nq - 1, 0), 0))
    nxt = lambda n: pl.BlockSpec((None, WINDOW, n), lambda bi, i: (bi, jnp.minimum((i + 1) * nq, nb - 1), 0))
    cblk = lambda n: pl.BlockSpec((None, ctx, n), lambda bi, i: (bi, 0, 0))
    return pl.pallas_call(
        functools.partial(_attn_body, seq=s),
        grid=(b, s // tq),
        in_specs=[
            tile(dq), tile(nk), prev(nk), nxt(nk), tile(nv), prev(nv), nxt(nv), cblk(nk), cblk(nv),
            _full(sink_blk.shape),
        ],
        out_specs=tile(dq),
        out_shape=jax.ShapeDtypeStruct(q.shape, BF16),
        scratch_shapes=[pltpu.VMEM((tq + 2 * WINDOW, nk), BF16),
                        pltpu.VMEM((tq + 2 * WINDOW, 2 * nv), BF16),
                        pltpu.VMEM((ctx + LANES, 2 * nv), BF16),
                        pltpu.VMEM((4, WINDOW, 3 * WINDOW), F32)],
        compiler_params=_params("arbitrary", "arbitrary"),
        name="window_attn",
    )(q, k, k, k, v, v, v, kc, vc, sink_blk)


R_E1, R_E2, R_W1, R_W2, R_RANK1, R_RANK2 = range(6)
ROUTE_COLS = 8
T_E1, T_E2, T_RANK1_HI, T_RANK1_LO, T_RANK2_HI, T_RANK2_LO = range(6)
RANK_RADIX = 16
ROUTE_SUB = 512


def _attn_out_body(mod_ref, ng_ref, h_ref, o_ref, wout_ref, wrt_ref, h1_ref, xn_ref, route_ref, cnt_ref,
                   route_t_ref):
    tm = h_ref.shape[0]
    n_exp = wrt_ref.shape[0]
    sub = min(ROUTE_SUB, tm)
    idx = lax.broadcasted_iota(jnp.int32, (sub, n_exp), 1)
    rcol = lax.broadcasted_iota(jnp.int32, (sub, ROUTE_COLS), 1)
    row = lax.broadcasted_iota(jnp.int32, (sub, sub), 0)
    col = lax.broadcasted_iota(jnp.int32, (sub, sub), 1)
    earlier = jnp.where(col < row, 1.0, 0.0).astype(BF16)
    eye = jnp.where(lax.broadcasted_iota(jnp.int32, (ROUTE_COLS, ROUTE_COLS), 0)
                    == lax.broadcasted_iota(jnp.int32, (ROUTE_COLS, ROUTE_COLS), 1), 1.0, 0.0).astype(BF16)
    nt = (((1,), (1,)), ((), ()))
    seen = jnp.zeros((1, n_exp), F32)
    for sb in range(tm // sub):
        rows = slice(sb * sub, (sb + 1) * sub)
        dh = jnp.dot(o_ref[rows, :], wout_ref[...], preferred_element_type=F32)
        h1 = h_ref[rows, :] + mod_ref[2:3, :] * dh
        h1_ref[rows, :] = h1
        xn = _rms_mod(h1, ng_ref[...], mod_ref[4:5, :], mod_ref[3:4, :])
        xn_ref[rows, :] = xn.astype(BF16)
        logits = jnp.zeros((sub, n_exp), F32)
        for e in range(n_exp):
            logit = jnp.sum(xn * wrt_ref[e:e + 1, :], axis=1, keepdims=True)
            logits = jnp.where(idx == e, logit, logits)
        m1 = jnp.max(logits, axis=1, keepdims=True)
        i1 = jnp.min(jnp.where(logits == m1, idx, n_exp), axis=1, keepdims=True)
        rest = jnp.where(idx == i1, -jnp.inf, logits)
        m2 = jnp.max(rest, axis=1, keepdims=True)
        i2 = jnp.min(jnp.where(rest == m2, idx, n_exp), axis=1, keepdims=True)
        e2 = jnp.exp(m2 - m1)
        w1 = 1.0 / (1.0 + e2)
        w2 = e2 / (1.0 + e2)
        chosen = jnp.where((idx == i1) | (idx == i2), 1.0, 0.0)
        before = jnp.dot(earlier, chosen.astype(BF16), preferred_element_type=F32) + seen
        seen = seen + jnp.sum(chosen, axis=0, keepdims=True)
        rank1 = jnp.sum(jnp.where(idx == i1, before, 0.0), axis=1, keepdims=True)
        rank2 = jnp.sum(jnp.where(idx == i2, before, 0.0), axis=1, keepdims=True)
        rec = jnp.zeros((sub, ROUTE_COLS), F32)
        for c, val in ((R_E1, i1.astype(F32)), (R_E2, i2.astype(F32)), (R_W1, w1), (R_W2, w2),
                       (R_RANK1, rank1), (R_RANK2, rank2)):
            rec = jnp.where(rcol == c, val, rec)
        route_ref[rows, :] = rec
        hi1 = jnp.floor(rank1 * (1.0 / RANK_RADIX))
        hi2 = jnp.floor(rank2 * (1.0 / RANK_RADIX))
        ints = jnp.zeros((sub, ROUTE_COLS), F32)
        for c, val in ((T_E1, i1.astype(F32)), (T_E2, i2.astype(F32)), (T_RANK1_HI, hi1), (T_RANK2_HI, hi2),
                       (T_RANK1_LO, rank1 - RANK_RADIX * hi1), (T_RANK2_LO, rank2 - RANK_RADIX * hi2)):
            ints = jnp.where(rcol == c, val, ints)
        route_t_ref[:, rows] = lax.dot_general(eye, ints.astype(BF16), nt, preferred_element_type=F32)
    cnt_ref[...] = seen


def _attn_out(h, o, mod_l, ng, w_out, w_router_t, tm):
    b, s, d = h.shape
    n_exp = w_router_t.shape[0]
    tok = lambda n: pl.BlockSpec((None, tm, n), lambda bi, i: (bi, i, 0))
    return pl.pallas_call(
        _attn_out_body,
        grid=(b, s // tm),
        in_specs=[
            pl.BlockSpec((None, 6, d), lambda bi, i: (bi, 0, 0)),
            _full((1, d)), tok(d), tok(o.shape[2]), _full(w_out.shape), _full(w_router_t.shape),
        ],
        out_specs=[tok(d), tok(d), tok(ROUTE_COLS),
                   pl.BlockSpec((None, None, 1, n_exp), lambda bi, i: (bi, i, 0, 0)),
                   pl.BlockSpec((None, None, ROUTE_COLS, tm), lambda bi, i: (bi, i, 0, 0))],
        out_shape=[jax.ShapeDtypeStruct(h.shape, F32), jax.ShapeDtypeStruct(h.shape, BF16),
                   jax.ShapeDtypeStruct((b, s, ROUTE_COLS), F32),
                   jax.ShapeDtypeStruct((b, s // tm, 1, n_exp), F32),
                   jax.ShapeDtypeStruct((b, s // tm, ROUTE_COLS, tm), F32)],
        compiler_params=_params("arbitrary", "arbitrary"),
        name="attn_out_router",
    )(mod_l, ng, h, o, w_out, w_router_t)


ROW_ALIGN = 8
GATE_LANES = LANES
G_W1, G_W2, G_E1 = 0, 3, 6


def _chunk_sizes(max_rows):
    sizes = []
    size = ROW_ALIGN
    while size <= max_rows:
        sizes.append(size)
        size *= 2
    return tuple(reversed(sizes))


def _chunked_copies(n, src, src_row, dst, dst_row, sems, sem_index, sizes):
    pairs = []
    for k, size in enumerate(sizes):
        done = n & (-2 * size)
        copy = pltpu.make_async_copy(
            src.at[pl.ds(pl.multiple_of(src_row + done, ROW_ALIGN), size)],
            dst.at[pl.ds(pl.multiple_of(dst_row + done, ROW_ALIGN), size)],
            sems.at[(*sem_index, k)])
        pairs.append(((n & size) != 0, copy))
    return pairs


def _start_all(pairs):
    for cond, copy in pairs:
        pl.when(cond)(copy.start)


def _wait_all(pairs):
    for cond, copy in pairs:
        pl.when(cond)(copy.wait)


def _selection(route, ls_ref, base, n_exp, r_cap):
    tm = route.shape[0]
    e1 = route[:, R_E1:R_E1 + 1].astype(jnp.int32)
    e2 = route[:, R_E2:R_E2 + 1].astype(jnp.int32)
    row1 = route[:, R_RANK1:R_RANK1 + 1].astype(jnp.int32)
    row2 = route[:, R_RANK2:R_RANK2 + 1].astype(jnp.int32)
    for e in range(n_exp):
        start = ls_ref[base + e]
        row1 = row1 + jnp.where(e1 == e, start, 0)
        row2 = row2 + jnp.where(e2 == e, start, 0)
    col = lax.broadcasted_iota(jnp.int32, (tm, r_cap), 1)
    return jnp.where((col == row1) | (col == row2), 1.0, 0.0).astype(BF16)


def _split3(w):
    a = w.astype(BF16).astype(F32)
    r = w - a
    b = r.astype(BF16).astype(F32)
    return a, b, r - b


def _selection_t(route_t, ls_ref, base, n_exp, r_cap):
    tm = route_t.shape[1]
    e1 = route_t[T_E1:T_E1 + 1, :].astype(jnp.int32)
    e2 = route_t[T_E2:T_E2 + 1, :].astype(jnp.int32)
    row1 = (route_t[T_RANK1_HI:T_RANK1_HI + 1, :] * RANK_RADIX + route_t[T_RANK1_LO:T_RANK1_LO + 1, :]).astype(jnp.int32)
    row2 = (route_t[T_RANK2_HI:T_RANK2_HI + 1, :] * RANK_RADIX + route_t[T_RANK2_LO:T_RANK2_LO + 1, :]).astype(jnp.int32)
    for e in range(n_exp):
        start = ls_ref[base + e]
        row1 = row1 + jnp.where(e1 == e, start, 0)
        row2 = row2 + jnp.where(e2 == e, start, 0)
    ridx = lax.broadcasted_iota(jnp.int32, (r_cap, tm), 0)
    return jnp.where((ridx == row1) | (ridx == row2), 1.0, 0.0).astype(BF16)


def _dispatch_body(ls_ref, off_ref, n8_ref, tail_off_ref, tail_n_ref, na_ref, x_ref, route_ref, route_t_ref,
                   xs_ref, local_ref, sems, *, n_exp, sizes, tm_moe, min_active):
    t = pl.program_id(0)
    base = t * n_exp
    tm, d = x_ref.shape
    r_cap = local_ref.shape[1]
    route = route_ref[...]
    sel = _selection_t(route_t_ref[...], ls_ref, base, n_exp, r_cap)
    lane = lax.broadcasted_iota(jnp.int32, (tm, GATE_LANES), 1)
    extra = jnp.zeros((tm, GATE_LANES), F32)
    terms = _split3(route[:, R_W1:R_W1 + 1]) + _split3(route[:, R_W2:R_W2 + 1]) + (route[:, R_E1:R_E1 + 1],)
    for k, term in enumerate(terms):
        extra = jnp.where(lane == k, term, extra)
    xa = jnp.concatenate([x_ref[...], extra.astype(BF16)], axis=1)
    rows = jnp.dot(sel, xa, preferred_element_type=F32)
    ridx = lax.broadcasted_iota(jnp.int32, (r_cap, 1), 0)
    owner = jnp.zeros((r_cap, 1), jnp.int32)
    for e in range(1, n_exp):
        owner = owner + jnp.where(ridx >= ls_ref[base + e], 1, 0)
    g = rows[:, d:]
    gate1 = g[:, G_W1:G_W1 + 1] + g[:, G_W1 + 1:G_W1 + 2] + g[:, G_W1 + 2:G_W1 + 3]
    gate2 = g[:, G_W2:G_W2 + 1] + g[:, G_W2 + 1:G_W2 + 2] + g[:, G_W2 + 2:G_W2 + 3]
    gate = jnp.where(g[:, G_E1:G_E1 + 1].astype(jnp.int32) == owner, gate1, gate2)
    slot = lax.rem(t, 2)

    def run_copies(tile, slot_):
        b0 = tile * n_exp
        pairs = []
        for e in range(n_exp):
            pairs += _chunked_copies(n8_ref[b0 + e], local_ref.at[slot_], ls_ref[b0 + e], xs_ref, off_ref[b0 + e],
                                     sems, (slot_, e), sizes[0])
        return pairs

    @pl.when(t >= 2)
    def _():
        _wait_all(run_copies(jnp.maximum(t - 2, 0), slot))

    local_ref[slot, :, :d] = rows[:, :d]
    local_ref[slot, :, d:] = jnp.broadcast_to(gate, (r_cap, GATE_LANES))
    _start_all(run_copies(t, slot))

    @pl.when(t == pl.num_programs(0) - 1)
    def _():
        @pl.when(t >= 1)
        def _():
            _wait_all(run_copies(jnp.maximum(t - 1, 0), 1 - slot))

        _wait_all(run_copies(t, slot))
        zeros_ref = local_ref.at[0]
        zeros_ref[...] = jnp.zeros_like(zeros_ref)
        tails = []
        for e in range(n_exp):
            tails += _chunked_copies(tail_n_ref[e], zeros_ref, 0, xs_ref, tail_off_ref[e], sems, (0, e), sizes[1])
        _start_all(tails)
        _wait_all(tails)
        n_tiles = xs_ref.shape[0] // tm_moe
        spare = []
        for k in range(n_tiles - min_active):
            tile = na_ref[0] + k
            copy = pltpu.make_async_copy(
                zeros_ref.at[pl.ds(0, tm_moe)],
                xs_ref.at[pl.ds(pl.multiple_of(jnp.minimum(tile, n_tiles - 1) * tm_moe, tm_moe), tm_moe)],
                sems.at[0, k % n_exp, k // n_exp])
            spare.append((tile < n_tiles, copy))
        _start_all(spare)
        _wait_all(spare)


def _dispatch(xn, route, route_t, tables, n_rows, tm, tm_moe):
    n_tok, d = xn.shape
    n_exp = tables["tail_n"].shape[0]
    r_cap = 2 * tm + LANES
    sizes = _chunk_sizes(tm), _chunk_sizes(tm_moe - ROW_ALIGN)
    assert tm_moe <= r_cap
    min_active = TOP_K * n_tok // tm_moe
    n_sems = max(len(sizes[0]), len(sizes[1]))
    assert n_rows // tm_moe - min_active <= n_exp * n_sems
    grid_spec = pltpu.PrefetchScalarGridSpec(
        num_scalar_prefetch=6,
        grid=(n_tok // tm,),
        in_specs=[pl.BlockSpec((tm, d), lambda t, *_: (t, 0)),
                  pl.BlockSpec((tm, ROUTE_COLS), lambda t, *_: (t, 0)),
                  pl.BlockSpec((None, ROUTE_COLS, tm), lambda t, *_: (t, 0, 0))],
        out_specs=pl.BlockSpec(memory_space=pl.ANY),
        scratch_shapes=[pltpu.VMEM((2, r_cap, d + GATE_LANES), F32),
                        pltpu.SemaphoreType.DMA((2, n_exp, n_sems))],
    )
    return pl.pallas_call(
        functools.partial(_dispatch_body, n_exp=n_exp, sizes=sizes, tm_moe=tm_moe, min_active=min_active),
        grid_spec=grid_spec,
        out_shape=jax.ShapeDtypeStruct((n_rows, d + GATE_LANES), F32),
        compiler_params=_params("arbitrary"),
        name="moe_dispatch",
    )(tables["ls"], tables["off"], tables["n8"], tables["tail_off"], tables["tail_n"], tables["n_active"],
      xn, route, route_t)


def _moe_body(te_ref, na_ref, x_ref, wg_ref, wu_ref, wd_ref, o_ref, xb_ref, acc_ref):
    i = pl.program_id(0)
    j = pl.program_id(1)
    d = o_ref.shape[1]

    @pl.when(i < na_ref[0])
    def _():
        @pl.when(j == 0)
        def _():
            xb_ref[...] = x_ref[:, :d].astype(BF16)
            acc_ref[...] = jnp.zeros_like(acc_ref)

        acc_ref[...] += _swiglu_part(xb_ref[...], wg_ref[...], wu_ref[...], wd_ref[...])

        @pl.when(j == pl.num_programs(1) - 1)
        def _():
            o_ref[...] = acc_ref[...] * x_ref[:, d:d + 1]

    @pl.when((i >= na_ref[0]) & (j == 0))
    def _():
        o_ref[...] = jnp.zeros_like(o_ref)


def _moe(xs, tile_expert, n_active, w_gate, w_up, w_down, tm, tf):
    n_rows, width = xs.shape
    n_exp, d, f = w_gate.shape
    nj = f // tf
    row = lambda i, j, te, na: (i, 0)
    def jj(i, j, na):
        tile = jnp.minimum(i, na[0] - 1)
        visit = jnp.where(i < na[0], j, nj - 1)
        return jnp.where(tile % 2 == 1, nj - 1 - visit, visit)
    grid_spec = pltpu.PrefetchScalarGridSpec(
        num_scalar_prefetch=2,
        grid=(n_rows // tm, nj),
        in_specs=[
            pl.BlockSpec((tm, width), row),
            pl.BlockSpec((None, d, tf), lambda i, j, te, na: (te[i], 0, jj(i, j, na))),
            pl.BlockSpec((None, d, tf), lambda i, j, te, na: (te[i], 0, jj(i, j, na))),
            pl.BlockSpec((None, tf, d), lambda i, j, te, na: (te[i], jj(i, j, na), 0)),
        ],
        out_specs=pl.BlockSpec((tm, d), row),
        scratch_shapes=[pltpu.VMEM((tm, d), BF16), pltpu.VMEM((tm, d), F32)],
    )
    return pl.pallas_call(
        _moe_body,
        grid_spec=grid_spec,
        out_shape=jax.ShapeDtypeStruct((n_rows, d), F32),
        compiler_params=_params("arbitrary", "arbitrary"),
        name="moe_ffn",
    )(tile_expert, n_active, xs, w_gate, w_up, w_down)


def _combine_body(ls_ref, off_ref, n8_ref, route_ref, h_ref, mod_ref, fg_ref, ys_ref, o_ref,
                  local_ref, sems, *, n_exp, sizes):
    n_steps = pl.num_programs(0) * pl.num_programs(1)
    t = pl.program_id(0) * pl.num_programs(1) + pl.program_id(1)
    base = t * n_exp
    r_cap = local_ref.shape[1]
    slot = lax.rem(t, 2)

    def run_copies(tile, slot_):
        b0 = tile * n_exp
        pairs = []
        for e in range(n_exp):
            pairs += _chunked_copies(n8_ref[b0 + e], ys_ref, off_ref[b0 + e], local_ref.at[slot_], ls_ref[b0 + e],
                                     sems, (slot_, e), sizes)
        return pairs

    def fetch(tile, slot_):
        local_ref[slot_] = jnp.zeros(local_ref.shape[1:], local_ref.dtype)
        _start_all(run_copies(tile, slot_))

    @pl.when(t == 0)
    def _():
        fetch(t, slot)

    @pl.when(t + 1 < n_steps)
    def _():
        fetch(jnp.minimum(t + 1, n_steps - 1), 1 - slot)

    sel = _selection(route_ref[...], ls_ref, base, n_exp, r_cap)
    _wait_all(run_copies(t, slot))
    y = jnp.dot(sel, local_ref[slot].astype(BF16), preferred_element_type=F32)
    h2 = h_ref[...] + mod_ref[5:6, :] * y
    ms = jnp.mean(h2 * h2, axis=-1, keepdims=True)
    o_ref[...] = h2 * lax.rsqrt(ms + EPS) * fg_ref[...]


def _combine(ys, route, h1, mod_l, final_g, tables, tm):
    b, s, d = h1.shape
    n_exp = tables["tail_n"].shape[0]
    r_cap = 2 * tm + LANES
    sizes = _chunk_sizes(tm)
    tok = lambda n: pl.BlockSpec((None, tm, n), lambda bi, i, *_: (bi, i, 0))
    grid_spec = pltpu.PrefetchScalarGridSpec(
        num_scalar_prefetch=3,
        grid=(b, s // tm),
        in_specs=[tok(ROUTE_COLS), tok(d),
                  pl.BlockSpec((None, 6, d), lambda bi, i, *_: (bi, 0, 0)),
                  pl.BlockSpec((1, d), lambda bi, i, *_: (0, 0)),
                  pl.BlockSpec(memory_space=pl.ANY)],
        out_specs=tok(d),
        scratch_shapes=[pltpu.VMEM((2, r_cap, d), F32), pltpu.SemaphoreType.DMA((2, n_exp, len(sizes)))],
    )
    return pl.pallas_call(
        functools.partial(_combine_body, n_exp=n_exp, sizes=sizes),
        grid_spec=grid_spec,
        out_shape=jax.ShapeDtypeStruct(h1.shape, F32),
        compiler_params=_params("arbitrary", "arbitrary"),
        name="moe_combine",
    )(tables["ls"], tables["off"], tables["n8"], route, h1, mod_l, final_g, ys)


def _routing_tables(cnt, tm_moe, n_tiles_moe):
    n = cnt.astype(jnp.int32)
    n8 = (n + ROW_ALIGN - 1) // ROW_ALIGN * ROW_ALIGN
    ls = jnp.cumsum(n8, axis=1) - n8
    rows = jnp.sum(n8, axis=0)
    rows_pad = (rows + tm_moe - 1) // tm_moe * tm_moe
    ends = jnp.cumsum(rows_pad)
    ebase = ends - rows_pad
    off = ebase[None, :] + jnp.cumsum(n8, axis=0) - n8
    n_active = ends[-1] // tm_moe
    tile_start = jnp.arange(n_tiles_moe, dtype=jnp.int32) * tm_moe
    te = jnp.sum((tile_start[:, None] >= ends[None, :]).astype(jnp.int32), axis=1)
    te = jnp.where(tile_start < ends[-1], te, te[n_active - 1])
    flat = lambda a: a.reshape(-1).astype(jnp.int32)
    return dict(ls=flat(ls), off=flat(off), n8=flat(n8), tail_off=flat(ebase + rows),
                tail_n=flat(rows_pad - rows), tile_expert=te, n_active=flat(n_active))


def _rope_tables(seq):
    pos = jnp.arange(seq, dtype=jnp.int32)
    lane = jnp.arange(LANES, dtype=jnp.int32) % HEAD_DIM
    quarter = HEAD_DIM // 4
    inv = ROPE_BASE ** (-(lane % quarter).astype(F32) / quarter)
    coord = jnp.where((lane < HEAD_DIM // 2)[None, :], (pos // GRID_W)[:, None], (pos % GRID_W)[:, None])
    ang = coord.astype(F32) * inv[None, :]
    second = ((lane % (HEAD_DIM // 2)) >= quarter)[None, :]
    sin = jnp.sin(ang)
    return jnp.cos(ang), jnp.where(second, 0.0, -sin), jnp.where(second, sin, 0.0)


def _head_order(n_q_heads):
    order = []
    for p in range(n_q_heads // (2 * Q_PER_KV)):
        for g in range(Q_PER_KV):
            order += [2 * p * Q_PER_KV + g, (2 * p + 1) * Q_PER_KV + g]
    return order


def kernel(x, c, ctx, c_ctx, ada_w, ada_b, norm_g, final_g, a_w_in, a_v_g, a_v_b, a_w_s, a_b_s, a_w_out,
           b_w_qkv, b_sink, b_w_out, ffn_w_gate, ffn_w_up, ffn_w_down,
           moe_w_router, moe_w_gate, moe_w_up, moe_w_down):
    batch, seq, d = x.shape
    ctx_len = ctx.shape[1]
    depth = ada_w.shape[0]
    assert depth == 2 and seq % 1024 == 0 and ctx_len % CHUNK == 0

    mod_rows = 8
    c_rows = jnp.concatenate([c, c_ctx[None, :], jnp.zeros((mod_rows - batch - 1, d), F32)], axis=0)
    mod = _ada_mod(c_rows, ada_w, ada_b).reshape(depth, mod_rows, 6, d)
    lat_row = lambda bi: bi
    ctx_row = lambda bi: batch

    a_width = a_w_out.shape[1]
    groups = a_w_s.shape[1]
    w_in = a_w_in[0].astype(BF16)
    w_s = a_w_s[0]
    wpair = jnp.concatenate([w_s[0::2], w_s[1::2]], axis=2).astype(BF16)
    bias = jnp.repeat(a_b_s[0].T, a_width // groups, axis=1)
    w_out_a = a_w_out[0].astype(BF16)
    ng0 = norm_g[0, 0][None, :]
    ng1 = norm_g[0, 1][None, :]
    wg0, wu0, wd0 = (w[0].astype(BF16) for w in (ffn_w_gate, ffn_w_up, ffn_w_down))
    vg, vb = a_v_g[0][None, :], a_v_b[0][None, :]

    def layer0(t, row):
        t = _mixer_a(t, mod[0], row, ng0, w_in, vg, vb, wpair, bias, w_out_a, tm=1024)
        return _dense_ffn(t, mod[0], row, ng1, wg0, wu0, wd0, tm=512, tf=1792)

    h = layer0(x, lat_row)
    z = layer0(ctx.reshape(1, batch * ctx_len, d), ctx_row)

    w_qkv = b_w_qkv[0]
    n_heads = b_sink.shape[1]
    nq = n_heads * HEAD_DIM
    n_kv = n_heads // Q_PER_KV
    nkv = n_kv * HEAD_DIM
    order = jnp.array(_head_order(n_heads), dtype=jnp.int32)
    wq = w_qkv[:, :nq].reshape(d, n_heads, HEAD_DIM)[:, order].reshape(d, nq)
    wk = w_qkv[:, nq:nq + nkv].reshape(d, n_kv, HEAD_DIM)
    zero = jnp.zeros_like(wk)
    even = (jnp.arange(n_kv) % 2 == 0)[None, :, None]
    wk_ext = jnp.concatenate([jnp.where(even, wk, zero), jnp.where(even, zero, wk)], axis=2).reshape(d, 2 * nkv)
    wv = w_qkv[:, nq + nkv:]
    w_lat = jnp.concatenate([wq, wk_ext, wv], axis=1).astype(BF16)
    w_ctx = jnp.concatenate([wk_ext, wv], axis=1).astype(BF16)
    w_out_b = b_w_out[0].reshape(n_heads, HEAD_DIM, d)[order].reshape(nq, d).astype(BF16)
    sink = b_sink[0]
    ng0 = norm_g[1, 0][None, :]
    ng1 = norm_g[1, 1][None, :]

    tables = _rope_tables(seq)
    q, k, v = _qkv(h, mod[1], lat_row, ng0, w_lat, tables, nq, 2 * nkv, nkv, tm=1024)
    n_ctx = batch * ctx_len
    no_rope = (jnp.ones((n_ctx, LANES), F32), jnp.zeros((n_ctx, LANES), F32), jnp.zeros((n_ctx, LANES), F32))
    kc, vc = _qkv(z, mod[1], ctx_row, ng0, w_ctx, no_rope, 0, 2 * nkv, nkv, tm=512)
    kc = kc.reshape(batch, ctx_len, 2 * nkv)
    vc = vc.reshape(batch, ctx_len, nkv)
    o = _attention(q, k, v, kc, vc, sink, tq=1024)
    tm_tok, tm_moe = 512, 512
    h1, xn, route, cnt, route_t = _attn_out(h, o, mod[1], ng1, w_out_b, moe_w_router[0].T, tm=tm_tok)
    n_exp = moe_w_router.shape[2]
    n_tok = batch * seq
    n_tok_tiles = n_tok // tm_tok
    max_rows = TOP_K * n_tok + n_tok_tiles * n_exp * (ROW_ALIGN - 1) + n_exp * (tm_moe - 1)
    n_moe_tiles = -(-max_rows // tm_moe)
    tables = _routing_tables(cnt.reshape(n_tok_tiles, n_exp), tm_moe, n_moe_tiles)
    xs = _dispatch(xn.reshape(n_tok, d), route.reshape(n_tok, ROUTE_COLS),
                   route_t.reshape(n_tok_tiles, ROUTE_COLS, tm_tok), tables, n_moe_tiles * tm_moe, tm_tok, tm_moe)
    wg1, wu1, wd1 = (w[0].astype(BF16) for w in (moe_w_gate, moe_w_up, moe_w_down))
    ys = _moe(xs, tables["tile_expert"], tables["n_active"], wg1, wu1, wd1, tm=tm_moe, tf=1792)
    return _combine(ys, route, h1, mod[1], final_g[None, :], tables, tm_tok)
```

```python
import functools

import jax
import jax.numpy as jnp
from jax import lax
from jax.experimental import pallas as pl
from jax.experimental.pallas import tpu as pltpu

F32 = jnp.float32
BF16 = jnp.bfloat16

EPS = 1e-6
NEG_INF = -1e30
LANES = 128
HEAD_DIM = 64
Q_PER_KV = 4
WINDOW = 128
GRID_W = 64
ROPE_BASE = 10000.0
CHUNK = 128
TOP_K = 2
VMEM_LIMIT_BYTES = 56 * 1024 * 1024
HIGHEST = lax.Precision.HIGHEST

TILE_ADA_LANES = 1536
TILE_MIXER_ROWS = 1024
TILE_QKV_ROWS = 1024
TILE_ATTN_ROWS = 1024
TILE_TOKEN_ROWS = 512
TILE_FFN_ROWS = 512
TILE_FFN_LANES = 1792


def _params(*semantics):
    return pltpu.CompilerParams(dimension_semantics=semantics, vmem_limit_bytes=VMEM_LIMIT_BYTES)


def _rms_mod(x, gain, scale, shift):
    ms = jnp.mean(x * x, axis=-1, keepdims=True)
    return (x * lax.rsqrt(ms + EPS) * gain) * (1.0 + scale) + shift


def _full(shape):
    return pl.BlockSpec(shape, lambda *_: (0,) * len(shape))


def _ada_body(c_ref, w_ref, b_ref, o_ref):
    c = c_ref[...]
    a = c * jax.nn.sigmoid(c)
    o_ref[...] = jnp.dot(a, w_ref[...], precision=HIGHEST, preferred_element_type=F32) + b_ref[...]


def _ada_mod(c_rows, ada_w, ada_b, tn=TILE_ADA_LANES):
    depth, d, n = ada_w.shape
    rows = c_rows.shape[0]
    return pl.pallas_call(
        _ada_body,
        grid=(depth, n // tn),
        in_specs=[
            pl.BlockSpec((rows, d), lambda l, j: (0, 0)),
            pl.BlockSpec((None, d, tn), lambda l, j: (l, 0, j)),
            pl.BlockSpec((None, 1, tn), lambda l, j: (l, 0, j)),
        ],
        out_specs=pl.BlockSpec((None, rows, tn), lambda l, j: (l, 0, j)),
        out_shape=jax.ShapeDtypeStruct((depth, rows, n), F32),
        compiler_params=_params("arbitrary", "arbitrary"),
        name="ada_mod",
    )(c_rows, ada_w, ada_b.reshape(depth, 1, n))


def _mixer_a_body(mod_ref, ng_ref, x_ref, win_ref, vg_ref, vb_ref, wpair_ref, bias_ref, wout_ref,
                  o_ref, vlo_ref, vhi_ref, gated_ref):
    tm, d = x_ref.shape
    a_width = wout_ref.shape[0]
    x = x_ref[...]
    hn = _rms_mod(x, ng_ref[...], mod_ref[1:2, :], mod_ref[0:1, :])
    uv = jnp.dot(hn.astype(BF16), win_ref[...], preferred_element_type=F32)
    uv = jax.nn.gelu(uv, approximate=True)
    v = uv[:, a_width:]
    mu = jnp.mean(v, axis=-1, keepdims=True)
    vc = v - mu
    var = jnp.mean(vc * vc, axis=-1, keepdims=True)
    vn = vc * lax.rsqrt(var + EPS) * vg_ref[...] + vb_ref[...]
    lane = lax.broadcasted_iota(jnp.int32, vn.shape, 1)
    low = (lane & (LANES - 1)) < (LANES // 2)
    vlo_ref[...] = jnp.where(low, vn, 0.0).astype(BF16)
    vhi_ref[...] = jnp.where(low, 0.0, vn).astype(BF16)
    n_chunks = tm // CHUNK
    mixed = []
    for j in range(a_width // LANES):
        lanes = slice(j * LANES, (j + 1) * LANES)
        rhs = jnp.concatenate(
            [jnp.concatenate([vlo_ref[c * CHUNK:(c + 1) * CHUNK, lanes], vhi_ref[c * CHUNK:(c + 1) * CHUNK, lanes]],
                             axis=0) for c in range(n_chunks)], axis=1)
        mixed.append(jnp.dot(wpair_ref[j], rhs, preferred_element_type=F32))
    for c in range(n_chunks):
        rows = slice(c * CHUNK, (c + 1) * CHUNK)
        s = jnp.concatenate([m[:, c * LANES:(c + 1) * LANES] for m in mixed], axis=1) + bias_ref[...]
        gated_ref[rows, :] = (uv[rows, :a_width] * s).astype(BF16)
    dh = jnp.dot(gated_ref[...], wout_ref[...], preferred_element_type=F32)
    o_ref[...] = x + mod_ref[2:3, :] * dh


def _mixer_a(x, mod_l, mod_row, ng, w_in, v_g, v_b, wpair, bias, w_out, tm):
    b, s, d = x.shape
    tm = min(tm, s)
    a_width = w_out.shape[0]
    return pl.pallas_call(
        _mixer_a_body,
        grid=(b, s // tm),
        in_specs=[
            pl.BlockSpec((None, 6, d), lambda bi, i: (mod_row(bi), 0, 0)),
            _full((1, d)),
            pl.BlockSpec((None, tm, d), lambda bi, i: (bi, i, 0)),
            _full(w_in.shape), _full((1, a_width)), _full((1, a_width)),
            _full(wpair.shape), _full(bias.shape), _full(w_out.shape),
        ],
        out_specs=pl.BlockSpec((None, tm, d), lambda bi, i: (bi, i, 0)),
        out_shape=jax.ShapeDtypeStruct(x.shape, F32),
        scratch_shapes=[pltpu.VMEM((tm, a_width), BF16)] * 3,
        compiler_params=_params("arbitrary", "arbitrary"),
        name="mixer_a",
    )(mod_l, ng, x, w_in, v_g, v_b, wpair, bias, w_out)


def _swiglu_part(xb, wg, wu, wd):
    g = jnp.dot(xb, wg, preferred_element_type=F32)
    u = jnp.dot(xb, wu, preferred_element_type=F32)
    a = (g * jax.nn.sigmoid(g) * u).astype(BF16)
    return jnp.dot(a, wd, preferred_element_type=F32)


def _dense_ffn_body(mod_ref, ng_ref, x_ref, wg_ref, wu_ref, wd_ref, o_ref, xn_ref, acc_ref):
    j = pl.program_id(2)

    @pl.when(j == 0)
    def _():
        xn_ref[...] = _rms_mod(x_ref[...], ng_ref[...], mod_ref[4:5, :], mod_ref[3:4, :]).astype(BF16)
        acc_ref[...] = jnp.zeros_like(acc_ref)

    acc_ref[...] += _swiglu_part(xn_ref[...], wg_ref[...], wu_ref[...], wd_ref[...])

    @pl.when(j == pl.num_programs(2) - 1)
    def _():
        o_ref[...] = x_ref[...] + mod_ref[5:6, :] * acc_ref[...]


def _dense_ffn(x, mod_l, mod_row, ng, w_gate, w_up, w_down, tm, tf):
    b, s, d = x.shape
    tm = min(tm, s)
    f = w_gate.shape[1]
    nj = f // tf
    fblk = lambda bi, i, j: jnp.where((bi * (s // tm) + i) % 2 == 1, nj - 1 - j, j)
    return pl.pallas_call(
        _dense_ffn_body,
        grid=(b, s // tm, f // tf),
        in_specs=[
            pl.BlockSpec((None, 6, d), lambda bi, i, j: (mod_row(bi), 0, 0)),
            _full((1, d)),
            pl.BlockSpec((None, tm, d), lambda bi, i, j: (bi, i, 0)),
            pl.BlockSpec((d, tf), lambda bi, i, j: (0, fblk(bi, i, j))),
            pl.BlockSpec((d, tf), lambda bi, i, j: (0, fblk(bi, i, j))),
            pl.BlockSpec((tf, d), lambda bi, i, j: (fblk(bi, i, j), 0)),
        ],
        out_specs=pl.BlockSpec((None, tm, d), lambda bi, i, j: (bi, i, 0)),
        out_shape=jax.ShapeDtypeStruct(x.shape, F32),
        scratch_shapes=[pltpu.VMEM((tm, d), BF16), pltpu.VMEM((tm, d), F32)],
        compiler_params=_params("arbitrary", "arbitrary", "arbitrary"),
        name="dense_ffn",
    )(mod_l, ng, x, w_gate, w_up, w_down)


def _rope(y, cos, sin_up, sin_dn):
    w = y.shape[1]
    reps = w // LANES
    tile = lambda t: jnp.concatenate([t] * reps, axis=1)
    up = pltpu.roll(y, w - HEAD_DIM // 4, axis=1)
    dn = pltpu.roll(y, HEAD_DIM // 4, axis=1)
    return y * tile(cos) + up * tile(sin_up) + dn * tile(sin_dn)


def _qkv_body(mod_ref, ng_ref, x_ref, w_ref, cos_ref, sup_ref, sdn_ref, *out_refs, nq, nk):
    hn = _rms_mod(x_ref[...], ng_ref[...], mod_ref[1:2, :], mod_ref[0:1, :])
    y = jnp.dot(hn.astype(BF16), w_ref[...], preferred_element_type=F32)
    cos, sup, sdn = cos_ref[...], sup_ref[...], sdn_ref[...]
    if nq:
        q_ref, k_ref, v_ref = out_refs
        q_ref[...] = (_rope(y[:, :nq], cos, sup, sdn) * HEAD_DIM ** -0.5).astype(BF16)
    else:
        k_ref, v_ref = out_refs
    k = _rope(y[:, nq:nq + nk], cos, sup, sdn)
    low = (lax.broadcasted_iota(jnp.int32, (k.shape[0], LANES), 1) < HEAD_DIM)
    cols = []
    for p in range(nk // LANES):
        pair = k[:, p * LANES:(p + 1) * LANES]
        cols += [jnp.where(low, pair, 0.0), jnp.where(low, 0.0, pair)]
    k_ref[...] = jnp.concatenate(cols, axis=1).astype(BF16)
    v_ref[...] = y[:, nq + nk:].astype(BF16)


def _qkv(x, mod_l, mod_row, ng, w, tables, nq, nk, nv, tm):
    b, s, d = x.shape
    tm = min(tm, s)
    widths = ([nq] if nq else []) + [2 * nk, nv]
    tab_spec = pl.BlockSpec((tm, LANES), lambda bi, i: (i, 0))
    return pl.pallas_call(
        functools.partial(_qkv_body, nq=nq, nk=nk),
        grid=(b, s // tm),
        in_specs=[
            pl.BlockSpec((None, 6, d), lambda bi, i: (mod_row(bi), 0, 0)),
            _full((1, d)),
            pl.BlockSpec((None, tm, d), lambda bi, i: (bi, i, 0)),
            _full(w.shape), tab_spec, tab_spec, tab_spec,
        ],
        out_specs=[pl.BlockSpec((None, tm, n), lambda bi, i: (bi, i, 0)) for n in widths],
        out_shape=[jax.ShapeDtypeStruct((b, s, n), BF16) for n in widths],
        compiler_params=_params("arbitrary", "arbitrary"),
        name="qkv_rope" if nq else "kv_ctx",
    )(mod_l, ng, x, w, *tables)


def _attn_body(q_ref, kt_ref, kp_ref, kn_ref, vt_ref, vp_ref, vn_ref, kc_ref, vc_ref, sink_ref,
               o_ref, kwin_ref, vwin_ref, vctx_ref, bias_ref, *, seq):
    tq = q_ref.shape[0]
    nq = tq // WINDOW
    ctx = kc_ref.shape[0]
    kw = 3 * WINDOW
    rows = Q_PER_KV * WINDOW
    n_pairs = vt_ref.shape[1] // LANES
    i = pl.program_id(1)
    first_step = (pl.program_id(0) == 0) & (i == 0)

    @pl.when(first_step)
    def _():
        qi = lax.broadcasted_iota(jnp.int32, (WINDOW, kw), 0)
        kr = lax.broadcasted_iota(jnp.int32, (WINDOW, kw), 1)
        band = jnp.abs(qi - kr + WINDOW) <= WINDOW
        for e in range(4):
            ok = band
            if e & 1:
                ok = ok & (kr >= WINDOW)
            if e & 2:
                ok = ok & (kr < 2 * WINDOW)
            bias_ref[e] = jnp.where(ok, 0.0, NEG_INF)

    kwin_ref[0:WINDOW, :] = kp_ref[...]
    kwin_ref[WINDOW:WINDOW + tq, :] = kt_ref[...]
    kwin_ref[WINDOW + tq:, :] = kn_ref[...]
    for p in range(n_pairs):
        src = slice(p * LANES, (p + 1) * LANES)
        dst = slice(2 * p * LANES, (2 * p + 1) * LANES)
        one = slice((2 * p + 1) * LANES, (2 * p + 2) * LANES)
        vwin_ref[0:WINDOW, dst] = vp_ref[:, src]
        vwin_ref[WINDOW:WINDOW + tq, dst] = vt_ref[:, src]
        vwin_ref[WINDOW + tq:, dst] = vn_ref[:, src]
        vwin_ref[:, one] = jnp.ones((tq + 2 * WINDOW, LANES), BF16)
        vctx_ref[0:ctx, dst] = vc_ref[:, src]
        vctx_ref[ctx:, dst] = jnp.zeros((LANES, LANES), BF16)
        sink_row = lax.broadcasted_iota(jnp.int32, (ctx + LANES, LANES), 0) <= ctx
        vctx_ref[:, one] = jnp.where(sink_row, 1.0, 0.0).astype(BF16)
    nt = (((1,), (1,)), ((), ()))
    low = lax.broadcasted_iota(jnp.int32, (rows, LANES), 1) < HEAD_DIM
    n_blocks = seq // WINDOW

    def block(jj, carry):
        r0 = pl.multiple_of(jj * WINDOW, WINDOW)
        blk = i * nq + jj
        edge = jnp.where(blk == 0, 1, 0) + jnp.where(blk == n_blocks - 1, 2, 0)
        bias = bias_ref[edge]
        bias4 = jnp.concatenate([bias] * Q_PER_KV, axis=0)
        for p in range(n_pairs):
            q4 = jnp.concatenate(
                [q_ref[pl.ds(r0, WINDOW), (p * Q_PER_KV + g) * LANES:(p * Q_PER_KV + g + 1) * LANES]
                 for g in range(Q_PER_KV)], axis=0)
            v_all = jnp.concatenate([vwin_ref[pl.ds(r0, kw), 2 * p * LANES:(2 * p + 2) * LANES],
                                     vctx_ref[:, 2 * p * LANES:(2 * p + 2) * LANES]], axis=0)
            halves = []
            for half in range(2):
                h = 2 * p + half
                s_loc = lax.dot_general(q4, kwin_ref[pl.ds(r0, kw), h * LANES:(h + 1) * LANES], nt,
                                        preferred_element_type=F32) + bias4
                s_ctx = lax.dot_general(q4, kc_ref[:, h * LANES:(h + 1) * LANES], nt,
                                        preferred_element_type=F32)
                s = jnp.concatenate([s_loc, s_ctx, sink_ref[h]], axis=1)
                m = jnp.max(s, axis=1, keepdims=True)
                pe = jnp.exp(s - m).astype(BF16)
                o = jnp.dot(pe, v_all, preferred_element_type=F32)
                halves.append(o[:, :LANES] / o[:, LANES:])
            o4 = jnp.where(low, halves[0], halves[1]).astype(o_ref.dtype)
            for g in range(Q_PER_KV):
                c0 = (p * Q_PER_KV + g) * LANES
                o_ref[pl.ds(r0, WINDOW), c0:c0 + LANES] = o4[g * WINDOW:(g + 1) * WINDOW, :]
        return carry

    lax.fori_loop(0, nq, block, 0, unroll=8)


def _attention(q, k, v, kc, vc, sink, tq):
    b, s, dq = q.shape
    nk, nv = k.shape[2], v.shape[2]
    ctx = kc.shape[1]
    nq = tq // WINDOW
    nb = s // WINDOW
    n_kv = sink.shape[0] // Q_PER_KV
    lane0 = jnp.arange(LANES) == 0
    sink_rows = jnp.repeat(sink.reshape(n_kv, Q_PER_KV), WINDOW, axis=1)
    sink_blk = jnp.where(lane0[None, None, :], sink_rows[:, :, None], NEG_INF).astype(F32)
    tile = lambda n: pl.BlockSpec((None, tq, n), lambda bi, i: (bi, i, 0))
    prev = lambda n: pl.BlockSpec((None, WINDOW, n), lambda bi, i: (bi, jnp.maximum(i * nq - 1, 0), 0))
    nxt = lambda n: pl.BlockSpec((None, WINDOW, n), lambda bi, i: (bi, jnp.minimum((i + 1) * nq, nb - 1), 0))
    cblk = lambda n: pl.BlockSpec((None, ctx, n), lambda bi, i: (bi, 0, 0))
    return pl.pallas_call(
        functools.partial(_attn_body, seq=s),
        grid=(b, s // tq),
        in_specs=[
            tile(dq), tile(nk), prev(nk), nxt(nk), tile(nv), prev(nv), nxt(nv), cblk(nk), cblk(nv),
            _full(sink_blk.shape),
        ],
        out_specs=tile(dq),
        out_shape=jax.ShapeDtypeStruct(q.shape, BF16),
        scratch_shapes=[pltpu.VMEM((tq + 2 * WINDOW, nk), BF16),
                        pltpu.VMEM((tq + 2 * WINDOW, 2 * nv), BF16),
                        pltpu.VMEM((ctx + LANES, 2 * nv), BF16),
                        pltpu.VMEM((4, WINDOW, 3 * WINDOW), F32)],
        compiler_params=_params("arbitrary", "arbitrary"),
        name="window_attn",
    )(q, k, k, k, v, v, v, kc, vc, sink_blk)


R_E1, R_E2, R_W1, R_W2, R_RANK1, R_RANK2 = range(6)
ROUTE_COLS = 8
T_E1, T_E2, T_RANK1_HI, T_RANK1_LO, T_RANK2_HI, T_RANK2_LO = range(6)
RANK_RADIX = 16
ROUTE_SUB = 512


def _attn_out_body(mod_ref, ng_ref, h_ref, o_ref, wout_ref, wrt_ref, h1_ref, xn_ref, route_ref, cnt_ref,
                   route_t_ref):
    tm = h_ref.shape[0]
    n_exp = wrt_ref.shape[0]
    sub = min(ROUTE_SUB, tm)
    idx = lax.broadcasted_iota(jnp.int32, (sub, n_exp), 1)
    rcol = lax.broadcasted_iota(jnp.int32, (sub, ROUTE_COLS), 1)
    row = lax.broadcasted_iota(jnp.int32, (sub, sub), 0)
    col = lax.broadcasted_iota(jnp.int32, (sub, sub), 1)
    earlier = jnp.where(col < row, 1.0, 0.0).astype(BF16)
    eye = jnp.where(lax.broadcasted_iota(jnp.int32, (ROUTE_COLS, ROUTE_COLS), 0)
                    == lax.broadcasted_iota(jnp.int32, (ROUTE_COLS, ROUTE_COLS), 1), 1.0, 0.0).astype(BF16)
    nt = (((1,), (1,)), ((), ()))
    seen = jnp.zeros((1, n_exp), F32)
    for sb in range(tm // sub):
        rows = slice(sb * sub, (sb + 1) * sub)
        dh = jnp.dot(o_ref[rows, :], wout_ref[...], preferred_element_type=F32)
        h1 = h_ref[rows, :] + mod_ref[2:3, :] * dh
        h1_ref[rows, :] = h1
        xn = _rms_mod(h1, ng_ref[...], mod_ref[4:5, :], mod_ref[3:4, :])
        xn_ref[rows, :] = xn.astype(BF16)
        logits = jnp.zeros((sub, n_exp), F32)
        for e in range(n_exp):
            logit = jnp.sum(xn * wrt_ref[e:e + 1, :], axis=1, keepdims=True)
            logits = jnp.where(idx == e, logit, logits)
        m1 = jnp.max(logits, axis=1, keepdims=True)
        i1 = jnp.min(jnp.where(logits == m1, idx, n_exp), axis=1, keepdims=True)
        rest = jnp.where(idx == i1, -jnp.inf, logits)
        m2 = jnp.max(rest, axis=1, keepdims=True)
        i2 = jnp.min(jnp.where(rest == m2, idx, n_exp), axis=1, keepdims=True)
        e2 = jnp.exp(m2 - m1)
        w1 = 1.0 / (1.0 + e2)
        w2 = e2 / (1.0 + e2)
        chosen = jnp.where((idx == i1) | (idx == i2), 1.0, 0.0)
        before = jnp.dot(earlier, chosen.astype(BF16), preferred_element_type=F32) + seen
        seen = seen + jnp.sum(chosen, axis=0, keepdims=True)
        rank1 = jnp.sum(jnp.where(idx == i1, before, 0.0), axis=1, keepdims=True)
        rank2 = jnp.sum(jnp.where(idx == i2, before, 0.0), axis=1, keepdims=True)
        rec = jnp.zeros((sub, ROUTE_COLS), F32)
        for c, val in ((R_E1, i1.astype(F32)), (R_E2, i2.astype(F32)), (R_W1, w1), (R_W2, w2),
                       (R_RANK1, rank1), (R_RANK2, rank2)):
            rec = jnp.where(rcol == c, val, rec)
        route_ref[rows, :] = rec
        hi1 = jnp.floor(rank1 * (1.0 / RANK_RADIX))
        hi2 = jnp.floor(rank2 * (1.0 / RANK_RADIX))
        ints = jnp.zeros((sub, ROUTE_COLS), F32)
        for c, val in ((T_E1, i1.astype(F32)), (T_E2, i2.astype(F32)), (T_RANK1_HI, hi1), (T_RANK2_HI, hi2),
                       (T_RANK1_LO, rank1 - RANK_RADIX * hi1), (T_RANK2_LO, rank2 - RANK_RADIX * hi2)):
            ints = jnp.where(rcol == c, val, ints)
        route_t_ref[:, rows] = lax.dot_general(eye, ints.astype(BF16), nt, preferred_element_type=F32)
    cnt_ref[...] = seen


def _attn_out(h, o, mod_l, ng, w_out, w_router_t, tm):
    b, s, d = h.shape
    n_exp = w_router_t.shape[0]
    tok = lambda n: pl.BlockSpec((None, tm, n), lambda bi, i: (bi, i, 0))
    return pl.pallas_call(
        _attn_out_body,
        grid=(b, s // tm),
        in_specs=[
            pl.BlockSpec((None, 6, d), lambda bi, i: (bi, 0, 0)),
            _full((1, d)), tok(d), tok(o.shape[2]), _full(w_out.shape), _full(w_router_t.shape),
        ],
        out_specs=[tok(d), tok(d), tok(ROUTE_COLS),
                   pl.BlockSpec((None, None, 1, n_exp), lambda bi, i: (bi, i, 0, 0)),
                   pl.BlockSpec((None, None, ROUTE_COLS, tm), lambda bi, i: (bi, i, 0, 0))],
        out_shape=[jax.ShapeDtypeStruct(h.shape, F32), jax.ShapeDtypeStruct(h.shape, BF16),
                   jax.ShapeDtypeStruct((b, s, ROUTE_COLS), F32),
                   jax.ShapeDtypeStruct((b, s // tm, 1, n_exp), F32),
                   jax.ShapeDtypeStruct((b, s // tm, ROUTE_COLS, tm), F32)],
        compiler_params=_params("arbitrary", "arbitrary"),
        name="attn_out_router",
    )(mod_l, ng, h, o, w_out, w_router_t)


ROW_ALIGN = 8
GATE_LANES = LANES
G_W1, G_W2, G_E1 = 0, 3, 6


def _chunk_sizes(max_rows):
    sizes = []
    size = ROW_ALIGN
    while size <= max_rows:
        sizes.append(size)
        size *= 2
    return tuple(reversed(sizes))


def _chunked_copies(n, src, src_row, dst, dst_row, sems, sem_index, sizes):
    pairs = []
    for k, size in enumerate(sizes):
        done = n & (-2 * size)
        copy = pltpu.make_async_copy(
            src.at[pl.ds(pl.multiple_of(src_row + done, ROW_ALIGN), size)],
            dst.at[pl.ds(pl.multiple_of(dst_row + done, ROW_ALIGN), size)],
            sems.at[(*sem_index, k)])
        pairs.append(((n & size) != 0, copy))
    return pairs


def _start_all(pairs):
    for cond, copy in pairs:
        pl.when(cond)(copy.start)


def _wait_all(pairs):
    for cond, copy in pairs:
        pl.when(cond)(copy.wait)


def _selection(route, ls_ref, base, n_exp, r_cap):
    tm = route.shape[0]
    e1 = route[:, R_E1:R_E1 + 1].astype(jnp.int32)
    e2 = route[:, R_E2:R_E2 + 1].astype(jnp.int32)
    row1 = route[:, R_RANK1:R_RANK1 + 1].astype(jnp.int32)
    row2 = route[:, R_RANK2:R_RANK2 + 1].astype(jnp.int32)
    for e in range(n_exp):
        start = ls_ref[base + e]
        row1 = row1 + jnp.where(e1 == e, start, 0)
        row2 = row2 + jnp.where(e2 == e, start, 0)
    col = lax.broadcasted_iota(jnp.int32, (tm, r_cap), 1)
    return jnp.where((col == row1) | (col == row2), 1.0, 0.0).astype(BF16)


def _split3(w):
    a = w.astype(BF16).astype(F32)
    r = w - a
    b = r.astype(BF16).astype(F32)
    return a, b, r - b


def _selection_t(route_t, ls_ref, base, n_exp, r_cap):
    tm = route_t.shape[1]
    e1 = route_t[T_E1:T_E1 + 1, :].astype(jnp.int32)
    e2 = route_t[T_E2:T_E2 + 1, :].astype(jnp.int32)
    row1 = (route_t[T_RANK1_HI:T_RANK1_HI + 1, :] * RANK_RADIX + route_t[T_RANK1_LO:T_RANK1_LO + 1, :]).astype(jnp.int32)
    row2 = (route_t[T_RANK2_HI:T_RANK2_HI + 1, :] * RANK_RADIX + route_t[T_RANK2_LO:T_RANK2_LO + 1, :]).astype(jnp.int32)
    for e in range(n_exp):
        start = ls_ref[base + e]
        row1 = row1 + jnp.where(e1 == e, start, 0)
        row2 = row2 + jnp.where(e2 == e, start, 0)
    ridx = lax.broadcasted_iota(jnp.int32, (r_cap, tm), 0)
    return jnp.where((ridx == row1) | (ridx == row2), 1.0, 0.0).astype(BF16)


def _dispatch_body(ls_ref, off_ref, n8_ref, tail_off_ref, tail_n_ref, na_ref, x_ref, route_ref, route_t_ref,
                   xs_ref, local_ref, sems, *, n_exp, sizes, tm_moe, min_active):
    t = pl.program_id(0)
    base = t * n_exp
    tm, d = x_ref.shape
    r_cap = local_ref.shape[1]
    route = route_ref[...]
    sel = _selection_t(route_t_ref[...], ls_ref, base, n_exp, r_cap)
    lane = lax.broadcasted_iota(jnp.int32, (tm, GATE_LANES), 1)
    extra = jnp.zeros((tm, GATE_LANES), F32)
    terms = _split3(route[:, R_W1:R_W1 + 1]) + _split3(route[:, R_W2:R_W2 + 1]) + (route[:, R_E1:R_E1 + 1],)
    for k, term in enumerate(terms):
        extra = jnp.where(lane == k, term, extra)
    xa = jnp.concatenate([x_ref[...], extra.astype(BF16)], axis=1)
    rows = jnp.dot(sel, xa, preferred_element_type=F32)
    ridx = lax.broadcasted_iota(jnp.int32, (r_cap, 1), 0)
    owner = jnp.zeros((r_cap, 1), jnp.int32)
    for e in range(1, n_exp):
        owner = owner + jnp.where(ridx >= ls_ref[base + e], 1, 0)
    g = rows[:, d:]
    gate1 = g[:, G_W1:G_W1 + 1] + g[:, G_W1 + 1:G_W1 + 2] + g[:, G_W1 + 2:G_W1 + 3]
    gate2 = g[:, G_W2:G_W2 + 1] + g[:, G_W2 + 1:G_W2 + 2] + g[:, G_W2 + 2:G_W2 + 3]
    gate = jnp.where(g[:, G_E1:G_E1 + 1].astype(jnp.int32) == owner, gate1, gate2)
    slot = lax.rem(t, 2)

    def run_copies(tile, slot_):
        b0 = tile * n_exp
        pairs = []
        for e in range(n_exp):
            pairs += _chunked_copies(n8_ref[b0 + e], local_ref.at[slot_], ls_ref[b0 + e], xs_ref, off_ref[b0 + e],
                                     sems, (slot_, e), sizes[0])
        return pairs

    @pl.when(t >= 2)
    def _():
        _wait_all(run_copies(jnp.maximum(t - 2, 0), slot))

    local_ref[slot, :, :d] = rows[:, :d]
    local_ref[slot, :, d:] = jnp.broadcast_to(gate, (r_cap, GATE_LANES))
    _start_all(run_copies(t, slot))

    @pl.when(t == pl.num_programs(0) - 1)
    def _():
        @pl.when(t >= 1)
        def _():
            _wait_all(run_copies(jnp.maximum(t - 1, 0), 1 - slot))

        _wait_all(run_copies(t, slot))
        zeros_ref = local_ref.at[0]
        zeros_ref[...] = jnp.zeros_like(zeros_ref)
        tails = []
        for e in range(n_exp):
            tails += _chunked_copies(tail_n_ref[e], zeros_ref, 0, xs_ref, tail_off_ref[e], sems, (0, e), sizes[1])
        _start_all(tails)
        _wait_all(tails)
        n_tiles = xs_ref.shape[0] // tm_moe
        spare = []
        for k in range(n_tiles - min_active):
            tile = na_ref[0] + k
            copy = pltpu.make_async_copy(
                zeros_ref.at[pl.ds(0, tm_moe)],
                xs_ref.at[pl.ds(pl.multiple_of(jnp.minimum(tile, n_tiles - 1) * tm_moe, tm_moe), tm_moe)],
                sems.at[0, k % n_exp, k // n_exp])
            spare.append((tile < n_tiles, copy))
        _start_all(spare)
        _wait_all(spare)


def _dispatch(xn, route, route_t, tables, n_rows, tm, tm_moe):
    n_tok, d = xn.shape
    n_exp = tables["tail_n"].shape[0]
    r_cap = 2 * tm + LANES
    sizes = _chunk_sizes(tm), _chunk_sizes(tm_moe - ROW_ALIGN)
    assert tm_moe <= r_cap
    min_active = TOP_K * n_tok // tm_moe
    n_sems = max(len(sizes[0]), len(sizes[1]))
    assert n_rows // tm_moe - min_active <= n_exp * n_sems
    grid_spec = pltpu.PrefetchScalarGridSpec(
        num_scalar_prefetch=6,
        grid=(n_tok // tm,),
        in_specs=[pl.BlockSpec((tm, d), lambda t, *_: (t, 0)),
                  pl.BlockSpec((tm, ROUTE_COLS), lambda t, *_: (t, 0)),
                  pl.BlockSpec((None, ROUTE_COLS, tm), lambda t, *_: (t, 0, 0))],
        out_specs=pl.BlockSpec(memory_space=pl.ANY),
        scratch_shapes=[pltpu.VMEM((2, r_cap, d + GATE_LANES), F32),
                        pltpu.SemaphoreType.DMA((2, n_exp, n_sems))],
    )
    return pl.pallas_call(
        functools.partial(_dispatch_body, n_exp=n_exp, sizes=sizes, tm_moe=tm_moe, min_active=min_active),
        grid_spec=grid_spec,
        out_shape=jax.ShapeDtypeStruct((n_rows, d + GATE_LANES), F32),
        compiler_params=_params("arbitrary"),
        name="moe_dispatch",
    )(tables["ls"], tables["off"], tables["n8"], tables["tail_off"], tables["tail_n"], tables["n_active"],
      xn, route, route_t)


def _moe_body(te_ref, na_ref, x_ref, wg_ref, wu_ref, wd_ref, o_ref, xb_ref, acc_ref):
    i = pl.program_id(0)
    j = pl.program_id(1)
    d = o_ref.shape[1]

    @pl.when(i < na_ref[0])
    def _():
        @pl.when(j == 0)
        def _():
            xb_ref[...] = x_ref[:, :d].astype(BF16)
            acc_ref[...] = jnp.zeros_like(acc_ref)

        acc_ref[...] += _swiglu_part(xb_ref[...], wg_ref[...], wu_ref[...], wd_ref[...])

        @pl.when(j == pl.num_programs(1) - 1)
        def _():
            o_ref[...] = acc_ref[...] * x_ref[:, d:d + 1]

    @pl.when((i >= na_ref[0]) & (j == 0))
    def _():
        o_ref[...] = jnp.zeros_like(o_ref)


def _moe(xs, tile_expert, n_active, w_gate, w_up, w_down, tm, tf):
    n_rows, width = xs.shape
    n_exp, d, f = w_gate.shape
    nj = f // tf
    row = lambda i, j, te, na: (i, 0)
    def jj(i, j, na):
        tile = jnp.minimum(i, na[0] - 1)
        visit = jnp.where(i < na[0], j, nj - 1)
        return jnp.where(tile % 2 == 1, nj - 1 - visit, visit)
    grid_spec = pltpu.PrefetchScalarGridSpec(
        num_scalar_prefetch=2,
        grid=(n_rows // tm, nj),
        in_specs=[
            pl.BlockSpec((tm, width), row),
            pl.BlockSpec((None, d, tf), lambda i, j, te, na: (te[i], 0, jj(i, j, na))),
            pl.BlockSpec((None, d, tf), lambda i, j, te, na: (te[i], 0, jj(i, j, na))),
            pl.BlockSpec((None, tf, d), lambda i, j, te, na: (te[i], jj(i, j, na), 0)),
        ],
        out_specs=pl.BlockSpec((tm, d), row),
        scratch_shapes=[pltpu.VMEM((tm, d), BF16), pltpu.VMEM((tm, d), F32)],
    )
    return pl.pallas_call(
        _moe_body,
        grid_spec=grid_spec,
        out_shape=jax.ShapeDtypeStruct((n_rows, d), F32),
        compiler_params=_params("arbitrary", "arbitrary"),
        name="moe_ffn",
    )(tile_expert, n_active, xs, w_gate, w_up, w_down)


def _combine_body(ls_ref, off_ref, n8_ref, route_ref, h_ref, mod_ref, fg_ref, ys_ref, o_ref,
                  local_ref, sems, *, n_exp, sizes):
    n_steps = pl.num_programs(0) * pl.num_programs(1)
    t = pl.program_id(0) * pl.num_programs(1) + pl.program_id(1)
    base = t * n_exp
    r_cap = local_ref.shape[1]
    slot = lax.rem(t, 2)

    def run_copies(tile, slot_):
        b0 = tile * n_exp
        pairs = []
        for e in range(n_exp):
            pairs += _chunked_copies(n8_ref[b0 + e], ys_ref, off_ref[b0 + e], local_ref.at[slot_], ls_ref[b0 + e],
                                     sems, (slot_, e), sizes)
        return pairs

    def fetch(tile, slot_):
        local_ref[slot_] = jnp.zeros(local_ref.shape[1:], local_ref.dtype)
        _start_all(run_copies(tile, slot_))

    @pl.when(t == 0)
    def _():
        fetch(t, slot)

    @pl.when(t + 1 < n_steps)
    def _():
        fetch(jnp.minimum(t + 1, n_steps - 1), 1 - slot)

    sel = _selection(route_ref[...], ls_ref, base, n_exp, r_cap)
    _wait_all(run_copies(t, slot))
    y = jnp.dot(sel, local_ref[slot].astype(BF16), preferred_element_type=F32)
    h2 = h_ref[...] + mod_ref[5:6, :] * y
    ms = jnp.mean(h2 * h2, axis=-1, keepdims=True)
    o_ref[...] = h2 * lax.rsqrt(ms + EPS) * fg_ref[...]


def _combine(ys, route, h1, mod_l, final_g, tables, tm):
    b, s, d = h1.shape
    n_exp = tables["tail_n"].shape[0]
    r_cap = 2 * tm + LANES
    sizes = _chunk_sizes(tm)
    tok = lambda n: pl.BlockSpec((None, tm, n), lambda bi, i, *_: (bi, i, 0))
    grid_spec = pltpu.PrefetchScalarGridSpec(
        num_scalar_prefetch=3,
        grid=(b, s // tm),
        in_specs=[tok(ROUTE_COLS), tok(d),
                  pl.BlockSpec((None, 6, d), lambda bi, i, *_: (bi, 0, 0)),
                  pl.BlockSpec((1, d), lambda bi, i, *_: (0, 0)),
                  pl.BlockSpec(memory_space=pl.ANY)],
        out_specs=tok(d),
        scratch_shapes=[pltpu.VMEM((2, r_cap, d), F32), pltpu.SemaphoreType.DMA((2, n_exp, len(sizes)))],
    )
    return pl.pallas_call(
        functools.partial(_combine_body, n_exp=n_exp, sizes=sizes),
        grid_spec=grid_spec,
        out_shape=jax.ShapeDtypeStruct(h1.shape, F32),
        compiler_params=_params("arbitrary", "arbitrary"),
        name="moe_combine",
    )(tables["ls"], tables["off"], tables["n8"], route, h1, mod_l, final_g, ys)


def _routing_tables(cnt, tm_moe, n_tiles_moe):
    n = cnt.astype(jnp.int32)
    n8 = (n + ROW_ALIGN - 1) // ROW_ALIGN * ROW_ALIGN
    ls = jnp.cumsum(n8, axis=1) - n8
    rows = jnp.sum(n8, axis=0)
    rows_pad = (rows + tm_moe - 1) // tm_moe * tm_moe
    ends = jnp.cumsum(rows_pad)
    ebase = ends - rows_pad
    off = ebase[None, :] + jnp.cumsum(n8, axis=0) - n8
    n_active = ends[-1] // tm_moe
    tile_start = jnp.arange(n_tiles_moe, dtype=jnp.int32) * tm_moe
    te = jnp.sum((tile_start[:, None] >= ends[None, :]).astype(jnp.int32), axis=1)
    te = jnp.where(tile_start < ends[-1], te, te[n_active - 1])
    flat = lambda a: a.reshape(-1).astype(jnp.int32)
    return dict(ls=flat(ls), off=flat(off), n8=flat(n8), tail_off=flat(ebase + rows),
                tail_n=flat(rows_pad - rows), tile_expert=te, n_active=flat(n_active))


def _rope_tables(seq):
    pos = jnp.arange(seq, dtype=jnp.int32)
    lane = jnp.arange(LANES, dtype=jnp.int32) % HEAD_DIM
    quarter = HEAD_DIM // 4
    inv = ROPE_BASE ** (-(lane % quarter).astype(F32) / quarter)
    coord = jnp.where((lane < HEAD_DIM // 2)[None, :], (pos // GRID_W)[:, None], (pos % GRID_W)[:, None])
    ang = coord.astype(F32) * inv[None, :]
    second = ((lane % (HEAD_DIM // 2)) >= quarter)[None, :]
    sin = jnp.sin(ang)
    return jnp.cos(ang), jnp.where(second, 0.0, -sin), jnp.where(second, sin, 0.0)


def _head_order(n_q_heads):
    order = []
    for p in range(n_q_heads // (2 * Q_PER_KV)):
        for g in range(Q_PER_KV):
            order += [2 * p * Q_PER_KV + g, (2 * p + 1) * Q_PER_KV + g]
    return order


def kernel(x, c, ctx, c_ctx, ada_w, ada_b, norm_g, final_g, a_w_in, a_v_g, a_v_b, a_w_s, a_b_s, a_w_out,
           b_w_qkv, b_sink, b_w_out, ffn_w_gate, ffn_w_up, ffn_w_down,
           moe_w_router, moe_w_gate, moe_w_up, moe_w_down):
    batch, seq, d = x.shape
    ctx_len = ctx.shape[1]
    depth = ada_w.shape[0]
    assert depth == 2 and ctx_len % CHUNK == 0
    assert all(seq % t == 0 for t in (TILE_MIXER_ROWS, TILE_QKV_ROWS, TILE_ATTN_ROWS, TILE_TOKEN_ROWS,
                                      TILE_FFN_ROWS))

    mod_rows = 8
    c_rows = jnp.concatenate([c, c_ctx[None, :], jnp.zeros((mod_rows - batch - 1, d), F32)], axis=0)
    mod = _ada_mod(c_rows, ada_w, ada_b).reshape(depth, mod_rows, 6, d)
    lat_row = lambda bi: bi
    ctx_row = lambda bi: batch

    a_width = a_w_out.shape[1]
    groups = a_w_s.shape[1]
    w_in = a_w_in[0].astype(BF16)
    w_s = a_w_s[0]
    wpair = jnp.concatenate([w_s[0::2], w_s[1::2]], axis=2).astype(BF16)
    bias = jnp.repeat(a_b_s[0].T, a_width // groups, axis=1)
    w_out_a = a_w_out[0].astype(BF16)
    ng0 = norm_g[0, 0][None, :]
    ng1 = norm_g[0, 1][None, :]
    wg0, wu0, wd0 = (w[0].astype(BF16) for w in (ffn_w_gate, ffn_w_up, ffn_w_down))
    vg, vb = a_v_g[0][None, :], a_v_b[0][None, :]

    def layer0(t, row):
        t = _mixer_a(t, mod[0], row, ng0, w_in, vg, vb, wpair, bias, w_out_a, tm=TILE_MIXER_ROWS)
        return _dense_ffn(t, mod[0], row, ng1, wg0, wu0, wd0, tm=TILE_FFN_ROWS, tf=TILE_FFN_LANES)

    h = layer0(x, lat_row)
    z = layer0(ctx.reshape(1, batch * ctx_len, d), ctx_row)

    w_qkv = b_w_qkv[0]
    n_heads = b_sink.shape[1]
    nq = n_heads * HEAD_DIM
    n_kv = n_heads // Q_PER_KV
    nkv = n_kv * HEAD_DIM
    order = jnp.array(_head_order(n_heads), dtype=jnp.int32)
    wq = w_qkv[:, :nq].reshape(d, n_heads, HEAD_DIM)[:, order].reshape(d, nq)
    w_lat = jnp.concatenate([wq, w_qkv[:, nq:]], axis=1).astype(BF16)
    w_ctx = w_qkv[:, nq:].astype(BF16)
    w_out_b = b_w_out[0].reshape(n_heads, HEAD_DIM, d)[order].reshape(nq, d).astype(BF16)
    sink = b_sink[0]
    ng0 = norm_g[1, 0][None, :]
    ng1 = norm_g[1, 1][None, :]

    tables = _rope_tables(seq)
    q, k, v = _qkv(h, mod[1], lat_row, ng0, w_lat, tables, nq, nkv, nkv, tm=TILE_QKV_ROWS)
    n_ctx = batch * ctx_len
    no_rope = (jnp.ones((n_ctx, LANES), F32), jnp.zeros((n_ctx, LANES), F32), jnp.zeros((n_ctx, LANES), F32))
    kc, vc = _qkv(z, mod[1], ctx_row, ng0, w_ctx, no_rope, 0, nkv, nkv, tm=TILE_QKV_ROWS)
    kc = kc.reshape(batch, ctx_len, 2 * nkv)
    vc = vc.reshape(batch, ctx_len, nkv)
    o = _attention(q, k, v, kc, vc, sink, tq=TILE_ATTN_ROWS)
    tm_tok, tm_moe = TILE_TOKEN_ROWS, TILE_FFN_ROWS
    h1, xn, route, cnt, route_t = _attn_out(h, o, mod[1], ng1, w_out_b, moe_w_router[0].T, tm=tm_tok)
    n_exp = moe_w_router.shape[2]
    n_tok = batch * seq
    n_tok_tiles = n_tok // tm_tok
    max_rows = TOP_K * n_tok + n_tok_tiles * n_exp * (ROW_ALIGN - 1) + n_exp * (tm_moe - 1)
    n_moe_tiles = -(-max_rows // tm_moe)
    tables = _routing_tables(cnt.reshape(n_tok_tiles, n_exp), tm_moe, n_moe_tiles)
    xs = _dispatch(xn.reshape(n_tok, d), route.reshape(n_tok, ROUTE_COLS),
                   route_t.reshape(n_tok_tiles, ROUTE_COLS, tm_tok), tables, n_moe_tiles * tm_moe, tm_tok, tm_moe)
    wg1, wu1, wd1 = (w[0].astype(BF16) for w in (moe_w_gate, moe_w_up, moe_w_down))
    ys = _moe(xs, tables["tile_expert"], tables["n_active"], wg1, wu1, wd1, tm=tm_moe, tf=TILE_FFN_LANES)
    return _combine(ys, route, h1, mod[1], final_g[None, :], tables, tm_tok)
```

```python
import functools

import jax
import jax.numpy as jnp
from jax import lax
from jax.experimental import pallas as pl
from jax.experimental.pallas import tpu as pltpu

F32 = jnp.float32
BF16 = jnp.bfloat16

EPS = 1e-6
NEG_INF = -1e30
LANES = 128
HEAD_DIM = 64
Q_PER_KV = 4
WINDOW = 128
GRID_W = 64
ROPE_BASE = 10000.0
CHUNK = 128
TOP_K = 2
VMEM_LIMIT_BYTES = 56 * 1024 * 1024
HIGHEST = lax.Precision.HIGHEST

TILE_ADA_LANES = 1536
TILE_MIXER_ROWS = 1024
TILE_QKV_ROWS = 1024
TILE_ATTN_ROWS = 1024
TILE_TOKEN_ROWS = 512
TILE_FFN_ROWS = 512
TILE_FFN_LANES = 1792


def _params(*semantics):
    return pltpu.CompilerParams(dimension_semantics=semantics, vmem_limit_bytes=VMEM_LIMIT_BYTES)


def _rms_mod(x, gain, scale, shift):
    ms = jnp.mean(x * x, axis=-1, keepdims=True)
    return (x * lax.rsqrt(ms + EPS) * gain) * (1.0 + scale) + shift


def _full(shape):
    return pl.BlockSpec(shape, lambda *_: (0,) * len(shape))


def _ada_body(c_ref, w_ref, b_ref, o_ref):
    c = c_ref[...]
    a = c * jax.nn.sigmoid(c)
    o_ref[...] = jnp.dot(a, w_ref[...], precision=HIGHEST, preferred_element_type=F32) + b_ref[...]


def _ada_mod(c_rows, ada_w, ada_b, tn=TILE_ADA_LANES):
    depth, d, n = ada_w.shape
    rows = c_rows.shape[0]
    return pl.pallas_call(
        _ada_body,
        grid=(depth, n // tn),
        in_specs=[
            pl.BlockSpec((rows, d), lambda l, j: (0, 0)),
            pl.BlockSpec((None, d, tn), lambda l, j: (l, 0, j)),
            pl.BlockSpec((None, 1, tn), lambda l, j: (l, 0, j)),
        ],
        out_specs=pl.BlockSpec((None, rows, tn), lambda l, j: (l, 0, j)),
        out_shape=jax.ShapeDtypeStruct((depth, rows, n), F32),
        compiler_params=_params("arbitrary", "arbitrary"),
        name="ada_mod",
    )(c_rows, ada_w, ada_b.reshape(depth, 1, n))


def _mixer_a_body(mod_ref, ng_ref, x_ref, win_ref, vg_ref, vb_ref, wpair_ref, bias_ref, wout_ref,
                  o_ref, vlo_ref, vhi_ref, gated_ref):
    tm, d = x_ref.shape
    a_width = wout_ref.shape[0]
    x = x_ref[...]
    hn = _rms_mod(x, ng_ref[...], mod_ref[1:2, :], mod_ref[0:1, :])
    uv = jnp.dot(hn.astype(BF16), win_ref[...], preferred_element_type=F32)
    uv = jax.nn.gelu(uv, approximate=True)
    v = uv[:, a_width:]
    mu = jnp.mean(v, axis=-1, keepdims=True)
    vc = v - mu
    var = jnp.mean(vc * vc, axis=-1, keepdims=True)
    vn = vc * lax.rsqrt(var + EPS) * vg_ref[...] + vb_ref[...]
    lane = lax.broadcasted_iota(jnp.int32, vn.shape, 1)
    low = (lane & (LANES - 1)) < (LANES // 2)
    vlo_ref[...] = jnp.where(low, vn, 0.0).astype(BF16)
    vhi_ref[...] = jnp.where(low, 0.0, vn).astype(BF16)
    n_chunks = tm // CHUNK
    mixed = []
    for j in range(a_width // LANES):
        lanes = slice(j * LANES, (j + 1) * LANES)
        rhs = jnp.concatenate(
            [jnp.concatenate([vlo_ref[c * CHUNK:(c + 1) * CHUNK, lanes], vhi_ref[c * CHUNK:(c + 1) * CHUNK, lanes]],
                             axis=0) for c in range(n_chunks)], axis=1)
        mixed.append(jnp.dot(wpair_ref[j], rhs, preferred_element_type=F32))
    for c in range(n_chunks):
        rows = slice(c * CHUNK, (c + 1) * CHUNK)
        s = jnp.concatenate([m[:, c * LANES:(c + 1) * LANES] for m in mixed], axis=1) + bias_ref[...]
        gated_ref[rows, :] = (uv[rows, :a_width] * s).astype(BF16)
    dh = jnp.dot(gated_ref[...], wout_ref[...], preferred_element_type=F32)
    o_ref[...] = x + mod_ref[2:3, :] * dh


def _mixer_a(x, mod_l, mod_row, ng, w_in, v_g, v_b, wpair, bias, w_out, tm):
    b, s, d = x.shape
    tm = min(tm, s)
    a_width = w_out.shape[0]
    return pl.pallas_call(
        _mixer_a_body,
        grid=(b, s // tm),
        in_specs=[
            pl.BlockSpec((None, 6, d), lambda bi, i: (mod_row(bi), 0, 0)),
            _full((1, d)),
            pl.BlockSpec((None, tm, d), lambda bi, i: (bi, i, 0)),
            _full(w_in.shape), _full((1, a_width)), _full((1, a_width)),
            _full(wpair.shape), _full(bias.shape), _full(w_out.shape),
        ],
        out_specs=pl.BlockSpec((None, tm, d), lambda bi, i: (bi, i, 0)),
        out_shape=jax.ShapeDtypeStruct(x.shape, F32),
        scratch_shapes=[pltpu.VMEM((tm, a_width), BF16)] * 3,
        compiler_params=_params("arbitrary", "arbitrary"),
        name="mixer_a",
    )(mod_l, ng, x, w_in, v_g, v_b, wpair, bias, w_out)


def _swiglu_part(xb, wg, wu, wd):
    g = jnp.dot(xb, wg, preferred_element_type=F32)
    u = jnp.dot(xb, wu, preferred_element_type=F32)
    a = (g * jax.nn.sigmoid(g) * u).astype(BF16)
    return jnp.dot(a, wd, preferred_element_type=F32)


def _dense_ffn_body(mod_ref, ng_ref, x_ref, wg_ref, wu_ref, wd_ref, o_ref, xn_ref, acc_ref):
    j = pl.program_id(2)

    @pl.when(j == 0)
    def _():
        xn_ref[...] = _rms_mod(x_ref[...], ng_ref[...], mod_ref[4:5, :], mod_ref[3:4, :]).astype(BF16)
        acc_ref[...] = jnp.zeros_like(acc_ref)

    acc_ref[...] += _swiglu_part(xn_ref[...], wg_ref[...], wu_ref[...], wd_ref[...])

    @pl.when(j == pl.num_programs(2) - 1)
    def _():
        o_ref[...] = x_ref[...] + mod_ref[5:6, :] * acc_ref[...]


def _dense_ffn(x, mod_l, mod_row, ng, w_gate, w_up, w_down, tm, tf):
    b, s, d = x.shape
    tm = min(tm, s)
    f = w_gate.shape[1]
    nj = f // tf
    fblk = lambda bi, i, j: jnp.where((bi * (s // tm) + i) % 2 == 1, nj - 1 - j, j)
    wmode = pl.Buffered(1) if nj == 1 else None
    return pl.pallas_call(
        _dense_ffn_body,
        grid=(b, s // tm, f // tf),
        in_specs=[
            pl.BlockSpec((None, 6, d), lambda bi, i, j: (mod_row(bi), 0, 0)),
            _full((1, d)),
            pl.BlockSpec((None, tm, d), lambda bi, i, j: (bi, i, 0)),
            pl.BlockSpec((d, tf), lambda bi, i, j: (0, fblk(bi, i, j)), pipeline_mode=wmode),
            pl.BlockSpec((d, tf), lambda bi, i, j: (0, fblk(bi, i, j)), pipeline_mode=wmode),
            pl.BlockSpec((tf, d), lambda bi, i, j: (fblk(bi, i, j), 0), pipeline_mode=wmode),
        ],
        out_specs=pl.BlockSpec((None, tm, d), lambda bi, i, j: (bi, i, 0)),
        out_shape=jax.ShapeDtypeStruct(x.shape, F32),
        scratch_shapes=[pltpu.VMEM((tm, d), BF16), pltpu.VMEM((tm, d), F32)],
        compiler_params=_params("arbitrary", "arbitrary", "arbitrary"),
        name="dense_ffn",
    )(mod_l, ng, x, w_gate, w_up, w_down)


def _rope(y, cos, sin_up, sin_dn):
    w = y.shape[1]
    reps = w // LANES
    tile = lambda t: jnp.concatenate([t] * reps, axis=1)
    up = pltpu.roll(y, w - HEAD_DIM // 4, axis=1)
    dn = pltpu.roll(y, HEAD_DIM // 4, axis=1)
    return y * tile(cos) + up * tile(sin_up) + dn * tile(sin_dn)


def _qkv_body(mod_ref, ng_ref, x_ref, w_ref, cos_ref, sup_ref, sdn_ref, *out_refs, nq, nk):
    hn = _rms_mod(x_ref[...], ng_ref[...], mod_ref[1:2, :], mod_ref[0:1, :])
    y = jnp.dot(hn.astype(BF16), w_ref[...], preferred_element_type=F32)
    cos, sup, sdn = cos_ref[...], sup_ref[...], sdn_ref[...]
    if nq:
        q_ref, k_ref, v_ref = out_refs
        q_ref[...] = (_rope(y[:, :nq], cos, sup, sdn) * HEAD_DIM ** -0.5).astype(BF16)
    else:
        k_ref, v_ref = out_refs
    k = _rope(y[:, nq:nq + nk], cos, sup, sdn)
    low = (lax.broadcasted_iota(jnp.int32, (k.shape[0], LANES), 1) < HEAD_DIM)
    cols = []
    for p in range(nk // LANES):
        pair = k[:, p * LANES:(p + 1) * LANES]
        cols += [jnp.where(low, pair, 0.0), jnp.where(low, 0.0, pair)]
    k_ref[...] = jnp.concatenate(cols, axis=1).astype(BF16)
    v_ref[...] = y[:, nq + nk:].astype(BF16)


def _qkv(x, mod_l, mod_row, ng, w, tables, nq, nk, nv, tm):
    b, s, d = x.shape
    tm = min(tm, s)
    widths = ([nq] if nq else []) + [2 * nk, nv]
    tab_spec = pl.BlockSpec((tm, LANES), lambda bi, i: (i, 0))
    return pl.pallas_call(
        functools.partial(_qkv_body, nq=nq, nk=nk),
        grid=(b, s // tm),
        in_specs=[
            pl.BlockSpec((None, 6, d), lambda bi, i: (mod_row(bi), 0, 0)),
            _full((1, d)),
            pl.BlockSpec((None, tm, d), lambda bi, i: (bi, i, 0)),
            _full(w.shape), tab_spec, tab_spec, tab_spec,
        ],
        out_specs=[pl.BlockSpec((None, tm, n), lambda bi, i: (bi, i, 0)) for n in widths],
        out_shape=[jax.ShapeDtypeStruct((b, s, n), BF16) for n in widths],
        compiler_params=_params("arbitrary", "arbitrary"),
        name="qkv_rope" if nq else "kv_ctx",
    )(mod_l, ng, x, w, *tables)


def _attn_body(q_ref, kt_ref, kp_ref, kn_ref, vt_ref, vp_ref, vn_ref, kc_ref, vc_ref, sink_ref,
               o_ref, kwin_ref, vwin_ref, vctx_ref, bias_ref, *, seq):
    tq = q_ref.shape[0]
    nq = tq // WINDOW
    ctx = kc_ref.shape[0]
    kw = 3 * WINDOW
    rows = Q_PER_KV * WINDOW
    n_pairs = vt_ref.shape[1] // LANES
    i = pl.program_id(1)
    first_step = (pl.program_id(0) == 0) & (i == 0)

    @pl.when(first_step)
    def _():
        qi = lax.broadcasted_iota(jnp.int32, (WINDOW, kw), 0)
        kr = lax.broadcasted_iota(jnp.int32, (WINDOW, kw), 1)
        band = jnp.abs(qi - kr + WINDOW) <= WINDOW
        for e in range(4):
            ok = band
            if e & 1:
                ok = ok & (kr >= WINDOW)
            if e & 2:
                ok = ok & (kr < 2 * WINDOW)
            bias_ref[e] = jnp.where(ok, 0.0, NEG_INF)

    kwin_ref[0:WINDOW, :] = kp_ref[...]
    kwin_ref[WINDOW:WINDOW + tq, :] = kt_ref[...]
    kwin_ref[WINDOW + tq:, :] = kn_ref[...]
    for p in range(n_pairs):
        src = slice(p * LANES, (p + 1) * LANES)
        dst = slice(2 * p * LANES, (2 * p + 1) * LANES)
        one = slice((2 * p + 1) * LANES, (2 * p + 2) * LANES)
        vwin_ref[0:WINDOW, dst] = vp_ref[:, src]
        vwin_ref[WINDOW:WINDOW + tq, dst] = vt_ref[:, src]
        vwin_ref[WINDOW + tq:, dst] = vn_ref[:, src]
        vwin_ref[:, one] = jnp.ones((tq + 2 * WINDOW, LANES), BF16)
        vctx_ref[0:ctx, dst] = vc_ref[:, src]
        vctx_ref[ctx:, dst] = jnp.zeros((LANES, LANES), BF16)
        sink_row = lax.broadcasted_iota(jnp.int32, (ctx + LANES, LANES), 0) <= ctx
        vctx_ref[:, one] = jnp.where(sink_row, 1.0, 0.0).astype(BF16)
    nt = (((1,), (1,)), ((), ()))
    low = lax.broadcasted_iota(jnp.int32, (rows, LANES), 1) < HEAD_DIM
    n_blocks = seq // WINDOW

    def block(jj, carry):
        r0 = pl.multiple_of(jj * WINDOW, WINDOW)
        blk = i * nq + jj
        edge = jnp.where(blk == 0, 1, 0) + jnp.where(blk == n_blocks - 1, 2, 0)
        bias = bias_ref[edge]
        bias4 = jnp.concatenate([bias] * Q_PER_KV, axis=0)
        for p in range(n_pairs):
            q4 = jnp.concatenate(
                [q_ref[pl.ds(r0, WINDOW), (p * Q_PER_KV + g) * LANES:(p * Q_PER_KV + g + 1) * LANES]
                 for g in range(Q_PER_KV)], axis=0)
            v_all = jnp.concatenate([vwin_ref[pl.ds(r0, kw), 2 * p * LANES:(2 * p + 2) * LANES],
                                     vctx_ref[:, 2 * p * LANES:(2 * p + 2) * LANES]], axis=0)
            halves = []
            for half in range(2):
                h = 2 * p + half
                s_loc = lax.dot_general(q4, kwin_ref[pl.ds(r0, kw), h * LANES:(h + 1) * LANES], nt,
                                        preferred_element_type=F32) + bias4
                s_ctx = lax.dot_general(q4, kc_ref[:, h * LANES:(h + 1) * LANES], nt,
                                        preferred_element_type=F32)
                s = jnp.concatenate([s_loc, s_ctx, sink_ref[h]], axis=1)
                m = jnp.max(s, axis=1, keepdims=True)
                pe = jnp.exp(s - m).astype(BF16)
                o = jnp.dot(pe, v_all, preferred_element_type=F32)
                halves.append(o[:, :LANES] / o[:, LANES:])
            o4 = jnp.where(low, halves[0], halves[1]).astype(o_ref.dtype)
            for g in range(Q_PER_KV):
                c0 = (p * Q_PER_KV + g) * LANES
                o_ref[pl.ds(r0, WINDOW), c0:c0 + LANES] = o4[g * WINDOW:(g + 1) * WINDOW, :]
        return carry

    lax.fori_loop(0, nq, block, 0, unroll=8)


def _attention(q, k, v, kc, vc, sink, tq):
    b, s, dq = q.shape
    nk, nv = k.shape[2], v.shape[2]
    ctx = kc.shape[1]
    nq = tq // WINDOW
    nb = s // WINDOW
    n_kv = sink.shape[0] // Q_PER_KV
    lane0 = jnp.arange(LANES) == 0
    sink_rows = jnp.repeat(sink.reshape(n_kv, Q_PER_KV), WINDOW, axis=1)
    sink_blk = jnp.where(lane0[None, None, :], sink_rows[:, :, None], NEG_INF).astype(F32)
    tile = lambda n: pl.BlockSpec((None, tq, n), lambda bi, i: (bi, i, 0))
    prev = lambda n: pl.BlockSpec((None, WINDOW, n), lambda bi, i: (bi, jnp.maximum(i * nq - 1, 0), 0))
    nxt = lambda n: pl.BlockSpec((None, WINDOW, n), lambda bi, i: (bi, jnp.minimum((i + 1) * nq, nb - 1), 0))
    cblk = lambda n: pl.BlockSpec((None, ctx, n), lambda bi, i: (bi, 0, 0))
    return pl.pallas_call(
        functools.partial(_attn_body, seq=s),
        grid=(b, s // tq),
        in_specs=[
            tile(dq), tile(nk), prev(nk), nxt(nk), tile(nv), prev(nv), nxt(nv), cblk(nk), cblk(nv),
            _full(sink_blk.shape),
        ],
        out_specs=tile(dq),
        out_shape=jax.ShapeDtypeStruct(q.shape, BF16),
        scratch_shapes=[pltpu.VMEM((tq + 2 * WINDOW, nk), BF16),
                        pltpu.VMEM((tq + 2 * WINDOW, 2 * nv), BF16),
                        pltpu.VMEM((ctx + LANES, 2 * nv), BF16),
                        pltpu.VMEM((4, WINDOW, 3 * WINDOW), F32)],
        compiler_params=_params("arbitrary", "arbitrary"),
        name="window_attn",
    )(q, k, k, k, v, v, v, kc, vc, sink_blk)


R_E1, R_E2, R_W1, R_W2, R_RANK1, R_RANK2 = range(6)
ROUTE_COLS = 8
T_E1, T_E2, T_RANK1_HI, T_RANK1_LO, T_RANK2_HI, T_RANK2_LO = range(6)
RANK_RADIX = 16
ROUTE_SUB = 512


def _attn_out_body(mod_ref, ng_ref, h_ref, o_ref, wout_ref, wrt_ref, h1_ref, xn_ref, route_ref, cnt_ref,
                   route_t_ref):
    tm = h_ref.shape[0]
    n_exp = wrt_ref.shape[0]
    sub = min(ROUTE_SUB, tm)
    idx = lax.broadcasted_iota(jnp.int32, (sub, n_exp), 1)
    rcol = lax.broadcasted_iota(jnp.int32, (sub, ROUTE_COLS), 1)
    row = lax.broadcasted_iota(jnp.int32, (sub, sub), 0)
    col = lax.broadcasted_iota(jnp.int32, (sub, sub), 1)
    earlier = jnp.where(col < row, 1.0, 0.0).astype(BF16)
    eye = jnp.where(lax.broadcasted_iota(jnp.int32, (ROUTE_COLS, ROUTE_COLS), 0)
                    == lax.broadcasted_iota(jnp.int32, (ROUTE_COLS, ROUTE_COLS), 1), 1.0, 0.0).astype(BF16)
    nt = (((1,), (1,)), ((), ()))
    seen = jnp.zeros((1, n_exp), F32)
    for sb in range(tm // sub):
        rows = slice(sb * sub, (sb + 1) * sub)
        dh = jnp.dot(o_ref[rows, :], wout_ref[...], preferred_element_type=F32)
        h1 = h_ref[rows, :] + mod_ref[2:3, :] * dh
        h1_ref[rows, :] = h1
        xn = _rms_mod(h1, ng_ref[...], mod_ref[4:5, :], mod_ref[3:4, :])
        xn_ref[rows, :] = xn.astype(BF16)
        logits = jnp.zeros((sub, n_exp), F32)
        for e in range(n_exp):
            logit = jnp.sum(xn * wrt_ref[e:e + 1, :], axis=1, keepdims=True)
            logits = jnp.where(idx == e, logit, logits)
        m1 = jnp.max(logits, axis=1, keepdims=True)
        i1 = jnp.min(jnp.where(logits == m1, idx, n_exp), axis=1, keepdims=True)
        rest = jnp.where(idx == i1, -jnp.inf, logits)
        m2 = jnp.max(rest, axis=1, keepdims=True)
        i2 = jnp.min(jnp.where(rest == m2, idx, n_exp), axis=1, keepdims=True)
        e2 = jnp.exp(m2 - m1)
        w1 = 1.0 / (1.0 + e2)
        w2 = e2 / (1.0 + e2)
        chosen = jnp.where((idx == i1) | (idx == i2), 1.0, 0.0)
        before = jnp.dot(earlier, chosen.astype(BF16), preferred_element_type=F32) + seen
        seen = seen + jnp.sum(chosen, axis=0, keepdims=True)
        rank1 = jnp.sum(jnp.where(idx == i1, before, 0.0), axis=1, keepdims=True)
        rank2 = jnp.sum(jnp.where(idx == i2, before, 0.0), axis=1, keepdims=True)
        rec = jnp.zeros((sub, ROUTE_COLS), F32)
        for c, val in ((R_E1, i1.astype(F32)), (R_E2, i2.astype(F32)), (R_W1, w1), (R_W2, w2),
                       (R_RANK1, rank1), (R_RANK2, rank2)):
            rec = jnp.where(rcol == c, val, rec)
        route_ref[rows, :] = rec
        hi1 = jnp.floor(rank1 * (1.0 / RANK_RADIX))
        hi2 = jnp.floor(rank2 * (1.0 / RANK_RADIX))
        ints = jnp.zeros((sub, ROUTE_COLS), F32)
        for c, val in ((T_E1, i1.astype(F32)), (T_E2, i2.astype(F32)), (T_RANK1_HI, hi1), (T_RANK2_HI, hi2),
                       (T_RANK1_LO, rank1 - RANK_RADIX * hi1), (T_RANK2_LO, rank2 - RANK_RADIX * hi2)):
            ints = jnp.where(rcol == c, val, ints)
        route_t_ref[:, rows] = lax.dot_general(eye, ints.astype(BF16), nt, preferred_element_type=F32)
    cnt_ref[...] = seen


def _attn_out(h, o, mod_l, ng, w_out, w_router_t, tm):
    b, s, d = h.shape
    n_exp = w_router_t.shape[0]
    tok = lambda n: pl.BlockSpec((None, tm, n), lambda bi, i: (bi, i, 0))
    return pl.pallas_call(
        _attn_out_body,
        grid=(b, s // tm),
        in_specs=[
            pl.BlockSpec((None, 6, d), lambda bi, i: (bi, 0, 0)),
            _full((1, d)), tok(d), tok(o.shape[2]), _full(w_out.shape), _full(w_router_t.shape),
        ],
        out_specs=[tok(d), tok(d), tok(ROUTE_COLS),
                   pl.BlockSpec((None, None, 1, n_exp), lambda bi, i: (bi, i, 0, 0)),
                   pl.BlockSpec((None, None, ROUTE_COLS, tm), lambda bi, i: (bi, i, 0, 0))],
        out_shape=[jax.ShapeDtypeStruct(h.shape, F32), jax.ShapeDtypeStruct(h.shape, BF16),
                   jax.ShapeDtypeStruct((b, s, ROUTE_COLS), F32),
                   jax.ShapeDtypeStruct((b, s // tm, 1, n_exp), F32),
                   jax.ShapeDtypeStruct((b, s // tm, ROUTE_COLS, tm), F32)],
        compiler_params=_params("arbitrary", "arbitrary"),
        name="attn_out_router",
    )(mod_l, ng, h, o, w_out, w_router_t)


ROW_ALIGN = 8
GATE_LANES = LANES
G_W1, G_W2, G_E1 = 0, 3, 6


def _chunk_sizes(max_rows):
    sizes = []
    size = ROW_ALIGN
    while size <= max_rows:
        sizes.append(size)
        size *= 2
    return tuple(reversed(sizes))


def _chunked_copies(n, src, src_row, dst, dst_row, sems, sem_index, sizes):
    pairs = []
    for k, size in enumerate(sizes):
        done = n & (-2 * size)
        copy = pltpu.make_async_copy(
            src.at[pl.ds(pl.multiple_of(src_row + done, ROW_ALIGN), size)],
            dst.at[pl.ds(pl.multiple_of(dst_row + done, ROW_ALIGN), size)],
            sems.at[(*sem_index, k)])
        pairs.append(((n & size) != 0, copy))
    return pairs


def _start_all(pairs):
    for cond, copy in pairs:
        pl.when(cond)(copy.start)


def _wait_all(pairs):
    for cond, copy in pairs:
        pl.when(cond)(copy.wait)


def _selection(route, ls_ref, base, n_exp, r_cap):
    tm = route.shape[0]
    e1 = route[:, R_E1:R_E1 + 1].astype(jnp.int32)
    e2 = route[:, R_E2:R_E2 + 1].astype(jnp.int32)
    row1 = route[:, R_RANK1:R_RANK1 + 1].astype(jnp.int32)
    row2 = route[:, R_RANK2:R_RANK2 + 1].astype(jnp.int32)
    for e in range(n_exp):
        start = ls_ref[base + e]
        row1 = row1 + jnp.where(e1 == e, start, 0)
        row2 = row2 + jnp.where(e2 == e, start, 0)
    col = lax.broadcasted_iota(jnp.int32, (tm, r_cap), 1)
    return jnp.where((col == row1) | (col == row2), 1.0, 0.0).astype(BF16)


def _split3(w):
    a = w.astype(BF16).astype(F32)
    r = w - a
    b = r.astype(BF16).astype(F32)
    return a, b, r - b


def _selection_t(route_t, ls_ref, base, n_exp, r_cap):
    tm = route_t.shape[1]
    e1 = route_t[T_E1:T_E1 + 1, :].astype(jnp.int32)
    e2 = route_t[T_E2:T_E2 + 1, :].astype(jnp.int32)
    row1 = (route_t[T_RANK1_HI:T_RANK1_HI + 1, :] * RANK_RADIX + route_t[T_RANK1_LO:T_RANK1_LO + 1, :]).astype(jnp.int32)
    row2 = (route_t[T_RANK2_HI:T_RANK2_HI + 1, :] * RANK_RADIX + route_t[T_RANK2_LO:T_RANK2_LO + 1, :]).astype(jnp.int32)
    for e in range(n_exp):
        start = ls_ref[base + e]
        row1 = row1 + jnp.where(e1 == e, start, 0)
        row2 = row2 + jnp.where(e2 == e, start, 0)
    ridx = lax.broadcasted_iota(jnp.int32, (r_cap, tm), 0)
    return jnp.where((ridx == row1) | (ridx == row2), 1.0, 0.0).astype(BF16)


def _dispatch_body(ls_ref, off_ref, n8_ref, tail_off_ref, tail_n_ref, na_ref, x_ref, route_ref, route_t_ref,
                   xs_ref, local_ref, sems, *, n_exp, sizes, tm_moe, min_active):
    t = pl.program_id(0)
    base = t * n_exp
    tm, d = x_ref.shape
    r_cap = local_ref.shape[1]
    route = route_ref[...]
    sel = _selection_t(route_t_ref[...], ls_ref, base, n_exp, r_cap)
    lane = lax.broadcasted_iota(jnp.int32, (tm, GATE_LANES), 1)
    extra = jnp.zeros((tm, GATE_LANES), F32)
    terms = _split3(route[:, R_W1:R_W1 + 1]) + _split3(route[:, R_W2:R_W2 + 1]) + (route[:, R_E1:R_E1 + 1],)
    for k, term in enumerate(terms):
        extra = jnp.where(lane == k, term, extra)
    xa = jnp.concatenate([x_ref[...], extra.astype(BF16)], axis=1)
    rows = jnp.dot(sel, xa, preferred_element_type=F32)
    ridx = lax.broadcasted_iota(jnp.int32, (r_cap, 1), 0)
    owner = jnp.zeros((r_cap, 1), jnp.int32)
    for e in range(1, n_exp):
        owner = owner + jnp.where(ridx >= ls_ref[base + e], 1, 0)
    g = rows[:, d:]
    gate1 = g[:, G_W1:G_W1 + 1] + g[:, G_W1 + 1:G_W1 + 2] + g[:, G_W1 + 2:G_W1 + 3]
    gate2 = g[:, G_W2:G_W2 + 1] + g[:, G_W2 + 1:G_W2 + 2] + g[:, G_W2 + 2:G_W2 + 3]
    gate = jnp.where(g[:, G_E1:G_E1 + 1].astype(jnp.int32) == owner, gate1, gate2)
    slot = lax.rem(t, 2)

    def run_copies(tile, slot_):
        b0 = tile * n_exp
        pairs = []
        for e in range(n_exp):
            pairs += _chunked_copies(n8_ref[b0 + e], local_ref.at[slot_], ls_ref[b0 + e], xs_ref, off_ref[b0 + e],
                                     sems, (slot_, e), sizes[0])
        return pairs

    @pl.when(t >= 2)
    def _():
        _wait_all(run_copies(jnp.maximum(t - 2, 0), slot))

    local_ref[slot, :, :d] = rows[:, :d]
    local_ref[slot, :, d:] = jnp.broadcast_to(gate, (r_cap, GATE_LANES))
    _start_all(run_copies(t, slot))

    @pl.when(t == pl.num_programs(0) - 1)
    def _():
        @pl.when(t >= 1)
        def _():
            _wait_all(run_copies(jnp.maximum(t - 1, 0), 1 - slot))

        _wait_all(run_copies(t, slot))
        zeros_ref = local_ref.at[0]
        zeros_ref[...] = jnp.zeros_like(zeros_ref)
        tails = []
        for e in range(n_exp):
            tails += _chunked_copies(tail_n_ref[e], zeros_ref, 0, xs_ref, tail_off_ref[e], sems, (0, e), sizes[1])
        _start_all(tails)
        _wait_all(tails)
        n_tiles = xs_ref.shape[0] // tm_moe
        spare = []
        for k in range(n_tiles - min_active):
            tile = na_ref[0] + k
            copy = pltpu.make_async_copy(
                zeros_ref.at[pl.ds(0, tm_moe)],
                xs_ref.at[pl.ds(pl.multiple_of(jnp.minimum(tile, n_tiles - 1) * tm_moe, tm_moe), tm_moe)],
                sems.at[0, k % n_exp, k // n_exp])
            spare.append((tile < n_tiles, copy))
        _start_all(spare)
        _wait_all(spare)


def _dispatch(xn, route, route_t, tables, n_rows, tm, tm_moe):
    n_tok, d = xn.shape
    n_exp = tables["tail_n"].shape[0]
    r_cap = 2 * tm + LANES
    sizes = _chunk_sizes(tm), _chunk_sizes(tm_moe - ROW_ALIGN)
    assert tm_moe <= r_cap
    min_active = TOP_K * n_tok // tm_moe
    n_sems = max(len(sizes[0]), len(sizes[1]))
    assert n_rows // tm_moe - min_active <= n_exp * n_sems
    grid_spec = pltpu.PrefetchScalarGridSpec(
        num_scalar_prefetch=6,
        grid=(n_tok // tm,),
        in_specs=[pl.BlockSpec((tm, d), lambda t, *_: (t, 0)),
                  pl.BlockSpec((tm, ROUTE_COLS), lambda t, *_: (t, 0)),
                  pl.BlockSpec((None, ROUTE_COLS, tm), lambda t, *_: (t, 0, 0))],
        out_specs=pl.BlockSpec(memory_space=pl.ANY),
        scratch_shapes=[pltpu.VMEM((2, r_cap, d + GATE_LANES), F32),
                        pltpu.SemaphoreType.DMA((2, n_exp, n_sems))],
    )
    return pl.pallas_call(
        functools.partial(_dispatch_body, n_exp=n_exp, sizes=sizes, tm_moe=tm_moe, min_active=min_active),
        grid_spec=grid_spec,
        out_shape=jax.ShapeDtypeStruct((n_rows, d + GATE_LANES), F32),
        compiler_params=_params("arbitrary"),
        name="moe_dispatch",
    )(tables["ls"], tables["off"], tables["n8"], tables["tail_off"], tables["tail_n"], tables["n_active"],
      xn, route, route_t)


def _moe_body(te_ref, na_ref, x_ref, wg_ref, wu_ref, wd_ref, o_ref, xb_ref, acc_ref):
    i = pl.program_id(0)
    j = pl.program_id(1)
    d = o_ref.shape[1]

    @pl.when(i < na_ref[0])
    def _():
        @pl.when(j == 0)
        def _():
            xb_ref[...] = x_ref[:, :d].astype(BF16)
            acc_ref[...] = jnp.zeros_like(acc_ref)

        acc_ref[...] += _swiglu_part(xb_ref[...], wg_ref[...], wu_ref[...], wd_ref[...])

        @pl.when(j == pl.num_programs(1) - 1)
        def _():
            o_ref[...] = acc_ref[...] * x_ref[:, d:d + 1]

    @pl.when((i >= na_ref[0]) & (j == 0))
    def _():
        o_ref[...] = jnp.zeros_like(o_ref)


def _moe(xs, tile_expert, n_active, w_gate, w_up, w_down, tm, tf):
    n_rows, width = xs.shape
    n_exp, d, f = w_gate.shape
    nj = f // tf
    row = lambda i, j, te, na: (i, 0)
    def jj(i, j, na):
        tile = jnp.minimum(i, na[0] - 1)
        visit = jnp.where(i < na[0], j, nj - 1)
        return jnp.where(tile % 2 == 1, nj - 1 - visit, visit)
    grid_spec = pltpu.PrefetchScalarGridSpec(
        num_scalar_prefetch=2,
        grid=(n_rows // tm, nj),
        in_specs=[
            pl.BlockSpec((tm, width), row),
            pl.BlockSpec((None, d, tf), lambda i, j, te, na: (te[i], 0, jj(i, j, na))),
            pl.BlockSpec((None, d, tf), lambda i, j, te, na: (te[i], 0, jj(i, j, na))),
            pl.BlockSpec((None, tf, d), lambda i, j, te, na: (te[i], jj(i, j, na), 0)),
        ],
        out_specs=pl.BlockSpec((tm, d), row),
        scratch_shapes=[pltpu.VMEM((tm, d), BF16), pltpu.VMEM((tm, d), F32)],
    )
    return pl.pallas_call(
        _moe_body,
        grid_spec=grid_spec,
        out_shape=jax.ShapeDtypeStruct((n_rows, d), F32),
        compiler_params=_params("arbitrary", "arbitrary"),
        name="moe_ffn",
    )(tile_expert, n_active, xs, w_gate, w_up, w_down)


def _combine_body(ls_ref, off_ref, n8_ref, route_ref, h_ref, mod_ref, fg_ref, ys_ref, o_ref,
                  local_ref, sems, *, n_exp, sizes):
    n_steps = pl.num_programs(0) * pl.num_programs(1)
    t = pl.program_id(0) * pl.num_programs(1) + pl.program_id(1)
    base = t * n_exp
    r_cap = local_ref.shape[1]
    slot = lax.rem(t, 2)

    def run_copies(tile, slot_):
        b0 = tile * n_exp
        pairs = []
        for e in range(n_exp):
            pairs += _chunked_copies(n8_ref[b0 + e], ys_ref, off_ref[b0 + e], local_ref.at[slot_], ls_ref[b0 + e],
                                     sems, (slot_, e), sizes)
        return pairs

    def fetch(tile, slot_):
        local_ref[slot_] = jnp.zeros(local_ref.shape[1:], local_ref.dtype)
        _start_all(run_copies(tile, slot_))

    @pl.when(t == 0)
    def _():
        fetch(t, slot)

    @pl.when(t + 1 < n_steps)
    def _():
        fetch(jnp.minimum(t + 1, n_steps - 1), 1 - slot)

    sel = _selection(route_ref[...], ls_ref, base, n_exp, r_cap)
    _wait_all(run_copies(t, slot))
    y = jnp.dot(sel, local_ref[slot].astype(BF16), preferred_element_type=F32)
    h2 = h_ref[...] + mod_ref[5:6, :] * y
    ms = jnp.mean(h2 * h2, axis=-1, keepdims=True)
    o_ref[...] = h2 * lax.rsqrt(ms + EPS) * fg_ref[...]


def _combine(ys, route, h1, mod_l, final_g, tables, tm):
    b, s, d = h1.shape
    n_exp = tables["tail_n"].shape[0]
    r_cap = 2 * tm + LANES
    sizes = _chunk_sizes(tm)
    tok = lambda n: pl.BlockSpec((None, tm, n), lambda bi, i, *_: (bi, i, 0))
    grid_spec = pltpu.PrefetchScalarGridSpec(
        num_scalar_prefetch=3,
        grid=(b, s // tm),
        in_specs=[tok(ROUTE_COLS), tok(d),
                  pl.BlockSpec((None, 6, d), lambda bi, i, *_: (bi, 0, 0)),
                  pl.BlockSpec((1, d), lambda bi, i, *_: (0, 0)),
                  pl.BlockSpec(memory_space=pl.ANY)],
        out_specs=tok(d),
        scratch_shapes=[pltpu.VMEM((2, r_cap, d), F32), pltpu.SemaphoreType.DMA((2, n_exp, len(sizes)))],
    )
    return pl.pallas_call(
        functools.partial(_combine_body, n_exp=n_exp, sizes=sizes),
        grid_spec=grid_spec,
        out_shape=jax.ShapeDtypeStruct(h1.shape, F32),
        compiler_params=_params("arbitrary", "arbitrary"),
        name="moe_combine",
    )(tables["ls"], tables["off"], tables["n8"], route, h1, mod_l, final_g, ys)


def _routing_tables(cnt, tm_moe, n_tiles_moe):
    n = cnt.astype(jnp.int32)
    n8 = (n + ROW_ALIGN - 1) // ROW_ALIGN * ROW_ALIGN
    ls = jnp.cumsum(n8, axis=1) - n8
    rows = jnp.sum(n8, axis=0)
    rows_pad = (rows + tm_moe - 1) // tm_moe * tm_moe
    ends = jnp.cumsum(rows_pad)
    ebase = ends - rows_pad
    off = ebase[None, :] + jnp.cumsum(n8, axis=0) - n8
    n_active = ends[-1] // tm_moe
    tile_start = jnp.arange(n_tiles_moe, dtype=jnp.int32) * tm_moe
    te = jnp.sum((tile_start[:, None] >= ends[None, :]).astype(jnp.int32), axis=1)
    te = jnp.where(tile_start < ends[-1], te, te[n_active - 1])
    flat = lambda a: a.reshape(-1).astype(jnp.int32)
    return dict(ls=flat(ls), off=flat(off), n8=flat(n8), tail_off=flat(ebase + rows),
                tail_n=flat(rows_pad - rows), tile_expert=te, n_active=flat(n_active))


def _rope_tables(seq):
    pos = jnp.arange(seq, dtype=jnp.int32)
    lane = jnp.arange(LANES, dtype=jnp.int32) % HEAD_DIM
    quarter = HEAD_DIM // 4
    inv = ROPE_BASE ** (-(lane % quarter).astype(F32) / quarter)
    coord = jnp.where((lane < HEAD_DIM // 2)[None, :], (pos // GRID_W)[:, None], (pos % GRID_W)[:, None])
    ang = coord.astype(F32) * inv[None, :]
    second = ((lane % (HEAD_DIM // 2)) >= quarter)[None, :]
    sin = jnp.sin(ang)
    return jnp.cos(ang), jnp.where(second, 0.0, -sin), jnp.where(second, sin, 0.0)


def _head_order(n_q_heads):
    order = []
    for p in range(n_q_heads // (2 * Q_PER_KV)):
        for g in range(Q_PER_KV):
            order += [2 * p * Q_PER_KV + g, (2 * p + 1) * Q_PER_KV + g]
    return order


def kernel(x, c, ctx, c_ctx, ada_w, ada_b, norm_g, final_g, a_w_in, a_v_g, a_v_b, a_w_s, a_b_s, a_w_out,
           b_w_qkv, b_sink, b_w_out, ffn_w_gate, ffn_w_up, ffn_w_down,
           moe_w_router, moe_w_gate, moe_w_up, moe_w_down):
    batch, seq, d = x.shape
    ctx_len = ctx.shape[1]
    depth = ada_w.shape[0]
    assert depth == 2 and ctx_len % CHUNK == 0
    assert all(seq % t == 0 for t in (TILE_MIXER_ROWS, TILE_QKV_ROWS, TILE_ATTN_ROWS, TILE_TOKEN_ROWS,
                                      TILE_FFN_ROWS))

    mod_rows = 8
    c_rows = jnp.concatenate([c, c_ctx[None, :], jnp.zeros((mod_rows - batch - 1, d), F32)], axis=0)
    mod = _ada_mod(c_rows, ada_w, ada_b).reshape(depth, mod_rows, 6, d)
    lat_row = lambda bi: bi
    ctx_row = lambda bi: batch

    a_width = a_w_out.shape[1]
    groups = a_w_s.shape[1]
    w_in = a_w_in[0].astype(BF16)
    w_s = a_w_s[0]
    wpair = jnp.concatenate([w_s[0::2], w_s[1::2]], axis=2).astype(BF16)
    bias = jnp.repeat(a_b_s[0].T, a_width // groups, axis=1)
    w_out_a = a_w_out[0].astype(BF16)
    ng0 = norm_g[0, 0][None, :]
    ng1 = norm_g[0, 1][None, :]
    wg0, wu0, wd0 = (w[0].astype(BF16) for w in (ffn_w_gate, ffn_w_up, ffn_w_down))
    vg, vb = a_v_g[0][None, :], a_v_b[0][None, :]

    def layer0(t, row):
        t = _mixer_a(t, mod[0], row, ng0, w_in, vg, vb, wpair, bias, w_out_a, tm=TILE_MIXER_ROWS)
        return _dense_ffn(t, mod[0], row, ng1, wg0, wu0, wd0, tm=TILE_FFN_ROWS, tf=wg0.shape[1])

    h = layer0(x, lat_row)
    z = layer0(ctx.reshape(1, batch * ctx_len, d), ctx_row)

    w_qkv = b_w_qkv[0]
    n_heads = b_sink.shape[1]
    nq = n_heads * HEAD_DIM
    n_kv = n_heads // Q_PER_KV
    nkv = n_kv * HEAD_DIM
    order = jnp.array(_head_order(n_heads), dtype=jnp.int32)
    wq = w_qkv[:, :nq].reshape(d, n_heads, HEAD_DIM)[:, order].reshape(d, nq)
    w_lat = jnp.concatenate([wq, w_qkv[:, nq:]], axis=1).astype(BF16)
    w_ctx = w_qkv[:, nq:].astype(BF16)
    w_out_b = b_w_out[0].reshape(n_heads, HEAD_DIM, d)[order].reshape(nq, d).astype(BF16)
    sink = b_sink[0]
    ng0 = norm_g[1, 0][None, :]
    ng1 = norm_g[1, 1][None, :]

    tables = _rope_tables(seq)
    q, k, v = _qkv(h, mod[1], lat_row, ng0, w_lat, tables, nq, nkv, nkv, tm=TILE_QKV_ROWS)
    n_ctx = batch * ctx_len
    no_rope = (jnp.ones((n_ctx, LANES), F32), jnp.zeros((n_ctx, LANES), F32), jnp.zeros((n_ctx, LANES), F32))
    kc, vc = _qkv(z, mod[1], ctx_row, ng0, w_ctx, no_rope, 0, nkv, nkv, tm=TILE_QKV_ROWS)
    kc = kc.reshape(batch, ctx_len, 2 * nkv)
    vc = vc.reshape(batch, ctx_len, nkv)
    o = _attention(q, k, v, kc, vc, sink, tq=TILE_ATTN_ROWS)
    tm_tok, tm_moe = TILE_TOKEN_ROWS, TILE_FFN_ROWS
    h1, xn, route, cnt, route_t = _attn_out(h, o, mod[1], ng1, w_out_b, moe_w_router[0].T, tm=tm_tok)
    n_exp = moe_w_router.shape[2]
    n_tok = batch * seq
    n_tok_tiles = n_tok // tm_tok
    max_rows = TOP_K * n_tok + n_tok_tiles * n_exp * (ROW_ALIGN - 1) + n_exp * (tm_moe - 1)
    n_moe_tiles = -(-max_rows // tm_moe)
    tables = _routing_tables(cnt.reshape(n_tok_tiles, n_exp), tm_moe, n_moe_tiles)
    xs = _dispatch(xn.reshape(n_tok, d), route.reshape(n_tok, ROUTE_COLS),
                   route_t.reshape(n_tok_tiles, ROUTE_COLS, tm_tok), tables, n_moe_tiles * tm_moe, tm_tok, tm_moe)
    wg1, wu1, wd1 = (w[0].astype(BF16) for w in (moe_w_gate, moe_w_up, moe_w_down))
    ys = _moe(xs, tables["tile_expert"], tables["n_active"], wg1, wu1, wd1, tm=tm_moe, tf=TILE_FFN_LANES)
    return _combine(ys, route, h1, mod[1], final_g[None, :], tables, tm_tok)
```
